```python
import math
import jax, jax.numpy as jnp
from jax import lax
import numpy as np

D_MODEL = 4096
BATCH = 4
SEQ = 2048
DEPTH = 1

HEAD_DIM = 128
N_ATT_HEADS = 16
N_KV_GROUPS = 4
HEADS_PER_GROUP = N_ATT_HEADS // N_KV_GROUPS
D_ATT = N_ATT_HEADS * HEAD_DIM
D_KV = N_KV_GROUPS * HEAD_DIM
N_POOL_GROUPS = 4
POOL_WINDOWS = (2, 4, 8, 16)
D_POOL = D_MODEL - D_ATT
POOL_GROUP_DIM = D_POOL // N_POOL_GROUPS
D_MIX = D_ATT + D_POOL
CMP_BLOCK = 32
CMP_STRIDE = 16
CMP_HIDDEN = HEAD_DIM
SEL_BLOCK = 64
SEL_TOP = 8
WINDOW = 512
Q_BLOCK = 128
REL_BUCKETS = 32
REL_MAX_DIST = 128
D_FF = 11008
EPS = 1e-6
NEG = -1e30
FORCE_BONUS = 1e4
D_IN = D_ATT + 6 * D_KV + 3 * N_ATT_HEADS + D_POOL

kernel_name = "hymba_nsa_pool_macaron"


def rmsnorm(x, g):
    xf = x.astype(jnp.float32)
    y = xf * lax.rsqrt(jnp.mean(xf * xf, axis=-1, keepdims=True) + EPS)
    return (y * g.astype(jnp.float32)).astype(x.dtype)


def swiglu(x, w_gate, w_up, w_down):
    return (jax.nn.silu(x @ w_gate) * (x @ w_up)) @ w_down


def rel_bucket(dist):
    max_exact = REL_BUCKETS // 2
    n = jnp.maximum(dist, 0)
    nf = jnp.maximum(n, 1).astype(jnp.float32)
    large = max_exact + (jnp.log(nf / max_exact) / math.log(REL_MAX_DIST / max_exact)
                         * (REL_BUCKETS - max_exact)).astype(jnp.int32)
    large = jnp.minimum(large, REL_BUCKETS - 1)
    return jnp.where(n < max_exact, n, large)


def compress(kv, pos, w1, w2):
    B, S, G, dk = kv.shape
    n_cmp = (S - CMP_BLOCK) // CMP_STRIDE + 1
    idx = jnp.arange(n_cmp)[:, None] * CMP_STRIDE + jnp.arange(CMP_BLOCK)[None, :]
    blk = kv[:, idx] + pos[None, None, :, None, :]
    blk = blk.transpose(0, 3, 1, 2, 4).reshape(B, G, n_cmp, CMP_BLOCK * dk)
    return jax.nn.silu(blk @ w1) @ w2


def nsa_attention(q, k_cmp, v_cmp, k_sel, v_sel, k_win, v_win, gates,
                  cmp_pos_k, w_ck1, w_ck2, cmp_pos_v, w_cv1, w_cv2, rel_table):
    B, S, G, HPG, dk = q.shape
    H = G * HPG
    pos = jnp.arange(S)

    kc = compress(k_cmp, cmp_pos_k, w_ck1, w_ck2)
    vc = compress(v_cmp, cmp_pos_v, w_cv1, w_cv2)
    n_cmp = kc.shape[2]
    cmp_end = jnp.arange(n_cmp) * CMP_STRIDE + CMP_BLOCK - 1
    dist_c = pos[:, None] - cmp_end[None, :]
    valid_c = dist_c >= 0
    bias_c = rel_table[rel_bucket(dist_c)].transpose(2, 0, 1).reshape(G, HPG, S, n_cmp)
    logit_c = jnp.einsum('bsghd,bgcd->bghsc', q, kc).astype(jnp.float32)
    logit_c = jnp.where(valid_c, logit_c + bias_c.astype(jnp.float32), NEG)
    any_valid = jnp.any(valid_c, axis=-1)[:, None].astype(jnp.float32)
    p_c = jax.nn.softmax(logit_c, axis=-1) * any_valid
    o_cmp = jnp.einsum('bghsc,bgcd->bsghd', p_c.astype(vc.dtype), vc)

    n_sb = S // SEL_BLOCK
    cs = jnp.arange(n_cmp) * CMP_STRIDE
    ss = jnp.arange(n_sb) * SEL_BLOCK
    overlap = ((cs[:, None] < ss[None, :] + SEL_BLOCK) &
               (cs[:, None] + CMP_BLOCK > ss[None, :])).astype(jnp.float32)
    imp = jnp.einsum('bghsc,cj->bgsj', p_c, overlap)
    cur = pos // SEL_BLOCK
    jb = jnp.arange(n_sb)
    forced = (jb[None] == 0) | (jb[None] == cur[:, None]) | (jb[None] == cur[:, None] - 1)
    future = jb[None] > cur[:, None]
    score = jnp.where(future, -1e9, imp + jnp.where(forced, FORCE_BONUS, 0.0))
    top = min(SEL_TOP, n_sb)
    _, sel_idx = lax.top_k(score, top)

    ksb = k_sel.reshape(B, n_sb, SEL_BLOCK, G, dk).transpose(0, 3, 1, 2, 4)
    vsb = v_sel.reshape(B, n_sb, SEL_BLOCK, G, dk).transpose(0, 3, 1, 2, 4)
    n_qb = S // Q_BLOCK
    q_blocks = q.reshape(B, n_qb, Q_BLOCK, G, HPG, dk).transpose(1, 0, 3, 4, 2, 5)
    idx_blocks = sel_idx.reshape(B, G, n_qb, Q_BLOCK, top).transpose(2, 0, 1, 3, 4)
    qpos_blocks = pos.reshape(n_qb, Q_BLOCK)
    tab = rel_table.T.reshape(G, HPG, REL_BUCKETS)
    b_ix = jnp.arange(B)[:, None, None, None]
    g_ix = jnp.arange(G)[None, :, None, None]
    g5 = jnp.arange(G)[None, :, None, None, None]
    h5 = jnp.arange(HPG)[None, None, :, None, None]

    def sel_block(args):
        qb, ib, qpos = args
        kg = ksb[b_ix, g_ix, ib].reshape(B, G, Q_BLOCK, top * SEL_BLOCK, dk)
        vg = vsb[b_ix, g_ix, ib].reshape(B, G, Q_BLOCK, top * SEL_BLOCK, dk)
        kpos = (ib[..., None] * SEL_BLOCK + jnp.arange(SEL_BLOCK)).reshape(B, G, Q_BLOCK, top * SEL_BLOCK)
        dist = qpos[None, None, :, None] - kpos
        bias = tab[g5, h5, rel_bucket(dist)[:, :, None]]
        logit = jnp.einsum('bghqd,bgqkd->bghqk', qb, kg).astype(jnp.float32)
        logit = jnp.where((dist >= 0)[:, :, None], logit + bias.astype(jnp.float32), NEG)
        p = jax.nn.softmax(logit, axis=-1)
        return jnp.einsum('bghqk,bgqkd->bqghd', p.astype(vg.dtype), vg)

    o_sel = lax.map(sel_block, (q_blocks, idx_blocks, qpos_blocks))
    o_sel = o_sel.transpose(1, 0, 2, 3, 4, 5).reshape(B, S, G, HPG, dk)

    span = WINDOW + Q_BLOCK
    kp = jnp.pad(k_win, ((0, 0), (WINDOW, 0), (0, 0), (0, 0)))
    vp = jnp.pad(v_win, ((0, 0), (WINDOW, 0), (0, 0), (0, 0)))
    widx = jnp.arange(n_qb)[:, None] * Q_BLOCK + jnp.arange(span)[None, :]
    kw = kp[:, widx]
    vw = vp[:, widx]
    qw = q.reshape(B, n_qb, Q_BLOCK, G, HPG, dk)
    dist_w = jnp.arange(Q_BLOCK)[:, None] + WINDOW - jnp.arange(span)[None, :]
    valid_w = ((dist_w >= 0) & (dist_w < WINDOW))[None] & ((widx - WINDOW) >= 0)[:, None, :]
    bias_w = rel_table[rel_bucket(dist_w)].transpose(2, 0, 1).reshape(G, HPG, 1, Q_BLOCK, span)
    logit_w = jnp.einsum('bnqghd,bnkgd->bghnqk', qw, kw).astype(jnp.float32)
    logit_w = jnp.where(valid_w, logit_w + bias_w.astype(jnp.float32), NEG)
    p_w = jax.nn.softmax(logit_w, axis=-1)
    o_win = jnp.einsum('bghnqk,bnkgd->bnqghd', p_w.astype(vw.dtype), vw).reshape(B, S, G, HPG, dk)

    g = jax.nn.sigmoid(gates).reshape(B, S, 3, G, HPG, 1)
    o = g[:, :, 0] * o_cmp + g[:, :, 1] * o_sel + g[:, :, 2] * o_win
    return o.reshape(B, S, H * dk)


def pool_mixer(u, w_pool, pool_scale):
    B, S, _ = u.shape
    uf = u.astype(jnp.float32).reshape(B, S, N_POOL_GROUPS, POOL_GROUP_DIM)
    c = jnp.cumsum(uf, axis=1)
    t = jnp.arange(S)
    means = []
    for gi, w in enumerate(POOL_WINDOWS):
        cg = c[:, :, gi]
        prev = jnp.pad(cg, ((0, 0), (w, 0), (0, 0)))[:, :S]
        cnt = jnp.minimum(t + 1, w).astype(jnp.float32)[None, :, None]
        means.append((cg - prev) / cnt)
    d = (jnp.stack(means, axis=2) - uf).astype(u.dtype)
    y = jnp.einsum('bsgc,gcd->bsgd', d, w_pool).reshape(B, S, D_POOL)
    return y * pool_scale


def setup_inputs(seed: int = 0) -> dict:
    key = jax.random.key(seed)
    ks = jax.random.split(key, 24)
    f32 = jnp.float32

    def nrm(k, shape, scale):
        return jax.random.normal(k, shape, f32) * scale

    def gain(k, shape):
        return 1.0 + 0.05 * jax.random.normal(k, shape, f32)

    L = DEPTH
    return {
        "x": jax.random.normal(ks[0], (BATCH, SEQ, D_MODEL), f32),
        "norm_ffn1": gain(ks[1], (L, D_MODEL)),
        "w_ffn1_gate": nrm(ks[2], (L, D_MODEL, D_FF), D_MODEL ** -0.5),
        "w_ffn1_up": nrm(ks[3], (L, D_MODEL, D_FF), D_MODEL ** -0.5),
        "w_ffn1_down": nrm(ks[4], (L, D_FF, D_MODEL), D_FF ** -0.5),
        "norm_mix": gain(ks[5], (L, D_MODEL)),
        "w_in": nrm(ks[6], (L, D_MODEL, D_IN), D_MODEL ** -0.5),
        "cmp_pos_k": nrm(ks[7], (L, CMP_BLOCK, HEAD_DIM), 0.1),
        "w_cmp_k1": nrm(ks[8], (L, CMP_BLOCK * HEAD_DIM, CMP_HIDDEN), (CMP_BLOCK * HEAD_DIM) ** -0.5),
        "w_cmp_k2": nrm(ks[9], (L, CMP_HIDDEN, HEAD_DIM), CMP_HIDDEN ** -0.5),
        "cmp_pos_v": nrm(ks[10], (L, CMP_BLOCK, HEAD_DIM), 0.1),
        "w_cmp_v1": nrm(ks[11], (L, CMP_BLOCK * HEAD_DIM, CMP_HIDDEN), (CMP_BLOCK * HEAD_DIM) ** -0.5),
        "w_cmp_v2": nrm(ks[12], (L, CMP_HIDDEN, HEAD_DIM), CMP_HIDDEN ** -0.5),
        "w_pool": nrm(ks[13], (L, N_POOL_GROUPS, POOL_GROUP_DIM, POOL_GROUP_DIM), POOL_GROUP_DIM ** -0.5),
        "pool_scale": gain(ks[14], (L, D_POOL)),
        "w_out": nrm(ks[15], (L, D_MIX, D_MODEL), D_MIX ** -0.5),
        "rel_table": nrm(ks[16], (REL_BUCKETS, N_ATT_HEADS), 0.5),
        "norm_ffn2": gain(ks[17], (L, D_MODEL)),
        "w_ffn2_gate": nrm(ks[18], (L, D_MODEL, D_FF), D_MODEL ** -0.5),
        "w_ffn2_up": nrm(ks[19], (L, D_MODEL, D_FF), D_MODEL ** -0.5),
        "w_ffn2_down": nrm(ks[20], (L, D_FF, D_MODEL), D_FF ** -0.5),
        "norm_final": gain(ks[21], (D_MODEL,)),
    }


def reference(x, norm_ffn1, w_ffn1_gate, w_ffn1_up, w_ffn1_down, norm_mix, w_in,
              cmp_pos_k, w_cmp_k1, w_cmp_k2, cmp_pos_v, w_cmp_v1, w_cmp_v2,
              w_pool, pool_scale, w_out, rel_table, norm_ffn2, w_ffn2_gate,
              w_ffn2_up, w_ffn2_down, norm_final):
    B, S, _ = x.shape
    splits = np.cumsum([D_ATT] + [D_KV] * 6 + [3 * N_ATT_HEADS]).tolist()
    for l in range(DEPTH):
        x = x + 0.5 * swiglu(rmsnorm(x, norm_ffn1[l]), w_ffn1_gate[l], w_ffn1_up[l], w_ffn1_down[l])
        h = rmsnorm(x, norm_mix[l])
        proj = h @ w_in[l]
        q, kc, vc, ksl, vsl, kwn, vwn, gates, u = jnp.split(proj, splits, axis=-1)
        q = q.reshape(B, S, N_KV_GROUPS, HEADS_PER_GROUP, HEAD_DIM) * (HEAD_DIM ** -0.5)
        kv = lambda t: t.reshape(B, S, N_KV_GROUPS, HEAD_DIM)
        y_att = nsa_attention(q, kv(kc), kv(vc), kv(ksl), kv(vsl), kv(kwn), kv(vwn), gates,
                              cmp_pos_k[l], w_cmp_k1[l], w_cmp_k2[l],
                              cmp_pos_v[l], w_cmp_v1[l], w_cmp_v2[l], rel_table)
        y_pool = pool_mixer(u, w_pool[l], pool_scale[l])
        x = x + jnp.concatenate([y_att, y_pool], axis=-1) @ w_out[l]
        x = x + 0.5 * swiglu(rmsnorm(x, norm_ffn2[l]), w_ffn2_gate[l], w_ffn2_up[l], w_ffn2_down[l])
    return rmsnorm(x, norm_final)
```

```python
import functools
import math

import jax
import jax.numpy as jnp
import numpy as np
from jax import lax
from jax.experimental import pallas as pl
from jax.experimental.pallas import tpu as pltpu

HEAD_DIM = 128
N_ATT_HEADS = 16
N_KV_GROUPS = 4
HEADS_PER_GROUP = N_ATT_HEADS // N_KV_GROUPS
D_ATT = N_ATT_HEADS * HEAD_DIM
D_KV = N_KV_GROUPS * HEAD_DIM
N_POOL_GROUPS = 4
POOL_WINDOWS = (2, 4, 8, 16)
CMP_BLOCK = 32
CMP_STRIDE = 16
SEL_BLOCK = 64
SEL_TOP = 8
WINDOW = 512
REL_BUCKETS = 32
REL_MAX_DIST = 128
EPS = 1e-6
NEG = -1e30
FORCE_BONUS = 1e4
FUTURE_SCORE = -1e9

LANES = 128
MXU_DIM = 256
VMEM_LIMIT_BYTES = 56 * 2 ** 20

FFN_TM = 512
FFN_TF = MXU_DIM
FFN_TN = 1024
NORM_ROWS = 128
PROJ_TM = 512
PROJ_TN = MXU_DIM
ATT_BLK = 128
OUT_TM = 512
OUT_TN = 1024

BF16 = jnp.bfloat16
F32 = jnp.float32


def _dot(a, b):
    return jnp.dot(a, b, preferred_element_type=F32)


def _dot_nt(a, b):
    return lax.dot_general(a, b, (((1,), (1,)), ((), ())), preferred_element_type=F32)


def _rms(x, g):
    return x * lax.rsqrt(jnp.mean(x * x, axis=-1, keepdims=True) + EPS) * g


def _ffn_body(x_ref, g_ref, wgu_ref, wd_ref, gf_ref, o_ref, h_ref, *, final_norm):
    j = pl.program_id(1)

    @pl.when(j == 0)
    def _():
        for r in range(0, x_ref.shape[0], NORM_ROWS):
            x = x_ref[r:r + NORM_ROWS, :]
            h_ref[r:r + NORM_ROWS, :] = _rms(x, g_ref[...]).astype(BF16)
            o_ref[r:r + NORM_ROWS, :] = x

    gu = _dot(h_ref[...], wgu_ref[0])
    gate = gu[:, :FFN_TF]
    up = gu[:, FFN_TF:]
    act = (0.5 * (gate * jax.nn.sigmoid(gate)) * up).astype(BF16)
    for c in range(0, o_ref.shape[1], FFN_TN):
        o_ref[:, c:c + FFN_TN] += _dot(act, wd_ref[0, :, c:c + FFN_TN])

    if final_norm:
        @pl.when(j == pl.num_programs(1) - 1)
        def _():
            for r in range(0, o_ref.shape[0], NORM_ROWS):
                o_ref[r:r + NORM_ROWS, :] = _rms(o_ref[r:r + NORM_ROWS, :], gf_ref[...])


def _ffn(x, g, wgu, wd, gf, *, final_norm):
    n, d = x.shape
    n_f = wgu.shape[0]
    return pl.pallas_call(
        functools.partial(_ffn_body, final_norm=final_norm),
        grid=(n // FFN_TM, n_f),
        in_specs=[
            pl.BlockSpec((FFN_TM, d), lambda i, j: (i, 0)),
            pl.BlockSpec((1, d), lambda i, j: (0, 0)),
            pl.BlockSpec((1, d, 2 * FFN_TF), lambda i, j: (j, 0, 0)),
            pl.BlockSpec((1, FFN_TF, d), lambda i, j: (j, 0, 0)),
            pl.BlockSpec((1, d), lambda i, j: (0, 0)),
        ],
        out_specs=pl.BlockSpec((FFN_TM, d), lambda i, j: (i, 0)),
        out_shape=jax.ShapeDtypeStruct((n, d), F32),
        scratch_shapes=[pltpu.VMEM((FFN_TM, d), BF16)],
        compiler_params=pltpu.CompilerParams(
            dimension_semantics=("parallel", "arbitrary"), vmem_limit_bytes=VMEM_LIMIT_BYTES),
        name="ffn_final" if final_norm else "ffn",
    )(x, g, wgu, wd, gf)


def _ffn_weights(w_gate, w_up, w_down):
    d, d_ff = w_gate.shape
    n_f = d_ff // FFN_TF
    wg = w_gate.astype(BF16).reshape(d, n_f, 1, FFN_TF)
    wu = w_up.astype(BF16).reshape(d, n_f, 1, FFN_TF)
    wgu = jnp.concatenate([wg, wu], axis=2).transpose(1, 0, 2, 3).reshape(n_f, d, 2 * FFN_TF)
    wd = w_down.astype(BF16).reshape(n_f, FFN_TF, d)
    return wgu, wd


def _inproj_body(x_ref, g_ref, w_ref, o_ref, h_ref):
    @pl.when(pl.program_id(1) == 0)
    def _():
        for r in range(0, x_ref.shape[0], NORM_ROWS):
            h_ref[r:r + NORM_ROWS, :] = _rms(x_ref[r:r + NORM_ROWS, :], g_ref[...]).astype(BF16)

    o_ref[...] = _dot(h_ref[...], w_ref[0])


def _inproj(x, g, w):
    n, d = x.shape
    n_t = w.shape[0]
    return pl.pallas_call(
        _inproj_body,
        grid=(n // PROJ_TM, n_t),
        in_specs=[
            pl.BlockSpec((PROJ_TM, d), lambda i, j: (i, 0)),
            pl.BlockSpec((1, d), lambda i, j: (0, 0)),
            pl.BlockSpec((1, d, PROJ_TN), lambda i, j: (j, 0, 0)),
        ],
        out_specs=pl.BlockSpec((PROJ_TM, PROJ_TN), lambda i, j: (i, j)),
        out_shape=jax.ShapeDtypeStruct((n, n_t * PROJ_TN), F32),
        scratch_shapes=[pltpu.VMEM((PROJ_TM, d), BF16)],
        compiler_params=pltpu.CompilerParams(
            dimension_semantics=("parallel", "arbitrary"), vmem_limit_bytes=VMEM_LIMIT_BYTES),
        name="in_proj",
    )(x, g, w)


COL_Q = 0
COL_KV = D_ATT
COL_U = D_ATT + 6 * D_KV
D_POOL_COLS = 2048
COL_GATES = COL_U + D_POOL_COLS
N_GATES = 3 * N_ATT_HEADS


def _inproj_weights(w_in):
    d, d_in = w_in.shape
    d_pool = d_in - COL_U - N_GATES
    assert d_pool == D_POOL_COLS
    n_cols = COL_GATES + N_GATES
    n_t = -(-n_cols // PROJ_TN)
    pad = n_t * PROJ_TN - n_cols
    w = jnp.concatenate(
        [w_in[:, :COL_U], w_in[:, COL_U + N_GATES:], w_in[:, COL_U:COL_U + N_GATES],
         jnp.zeros((d, pad), w_in.dtype)], axis=1).astype(BF16)
    return w.reshape(d, n_t, PROJ_TN).transpose(1, 0, 2)


def _rel_bucket_np(n):
    max_exact = REL_BUCKETS // 2
    n = np.maximum(n, 0)
    nf = np.maximum(n, 1).astype(np.float32)
    large = max_exact + (np.log(nf / max_exact) / math.log(REL_MAX_DIST / max_exact)
                         * (REL_BUCKETS - max_exact)).astype(np.int32)
    large = np.minimum(large, REL_BUCKETS - 1)
    return np.where(n < max_exact, n, large)


def _bias_tables(rel_table, seq):
    blk = ATT_BLK
    i = np.arange(blk)[:, None]
    j = np.arange(blk)[None, :]
    assert int(_rel_bucket_np(np.array(blk + 1))) == REL_BUCKETS - 1
    dist = np.stack([i - j, blk + i - j, np.full((blk, blk), 2 * blk)])
    tab_t = rel_table.T
    toep = tab_t[:, _rel_bucket_np(dist)].transpose(1, 0, 2, 3)
    n_cmp_pad = seq // CMP_STRIDE
    dist_c = np.arange(seq)[:, None] - (np.arange(n_cmp_pad)[None, :] * CMP_STRIDE + CMP_BLOCK - 1)
    bias_c = tab_t[:, _rel_bucket_np(dist_c)]
    return toep.astype(F32), bias_c.astype(F32)


def _compress(kv_ref, pos_ref, w1_ref, w2_ref):
    seq, dk = kv_ref.shape
    n_half = seq // CMP_STRIDE
    first = jnp.zeros((n_half, w1_ref.shape[1]), F32)
    second = jnp.zeros((n_half, w1_ref.shape[1]), F32)
    for l in range(CMP_STRIDE):
        rows = kv_ref[pl.ds(l, n_half, stride=CMP_STRIDE), :]
        a = (rows + pos_ref[l:l + 1, :]).astype(BF16)
        b = (rows + pos_ref[CMP_STRIDE + l:CMP_STRIDE + l + 1, :]).astype(BF16)
        first += _dot(a, w1_ref[l * dk:(l + 1) * dk, :])
        second += _dot(b, w1_ref[(CMP_STRIDE + l) * dk:(CMP_STRIDE + l + 1) * dk, :])
    pre = first + pltpu.roll(second, n_half - 1, 0)
    hid = (pre * jax.nn.sigmoid(pre)).astype(BF16)
    return _dot(hid, w2_ref[...])


def _nsa_body(q_ref, kc_ref, vc_ref, ks_ref, vs_ref, kw_ref, vw_ref, gates_ref,
              posk_ref, wk1_ref, wk2_ref, posv_ref, wv1_ref, wv2_ref,
              biasc_ref, toep_ref, overlap_ref, expand_ref,
              o_ref, kcmp_ref, vcmp_ref, mask_ref):
    g = pl.program_id(1)
    qi = pl.program_id(2)
    tq = ATT_BLK
    hpg = HEADS_PER_GROUP
    rows = hpg * tq
    n_cmp = kcmp_ref.shape[0]

    @pl.when(qi == 0)
    def _():
        kcmp_ref[...] = _compress(kc_ref, posk_ref, wk1_ref, wk2_ref).astype(BF16)
        vcmp_ref[...] = _compress(vc_ref, posv_ref, wv1_ref, wv2_ref).astype(BF16)

    q = q_ref[...] * (HEAD_DIM ** -0.5)
    qs = jnp.concatenate([q[:, h * HEAD_DIM:(h + 1) * HEAD_DIM] for h in range(hpg)],
                         axis=0).astype(BF16)

    assert tq & (tq - 1) == 0 and SEL_BLOCK & (SEL_BLOCK - 1) == 0
    t_rows = qi * tq + (lax.broadcasted_iota(jnp.int32, (rows, 1), 0) & (tq - 1))
    c_idx = lax.broadcasted_iota(jnp.int32, (rows, n_cmp), 1)
    valid_c = t_rows - (c_idx * CMP_STRIDE + CMP_BLOCK - 1) >= 0
    logit = _dot_nt(qs, kcmp_ref[...]) + biasc_ref[...].reshape(rows, n_cmp)
    logit = jnp.where(valid_c, logit, NEG)
    e = jnp.exp(logit - jnp.max(logit, axis=-1, keepdims=True))
    any_valid = (t_rows >= CMP_BLOCK - 1).astype(F32)
    p = e / jnp.sum(e, axis=-1, keepdims=True) * any_valid
    o_cmp = _dot(p.astype(BF16), vcmp_ref[...])

    psum = p[0:tq]
    for h in range(1, hpg):
        psum = psum + p[h * tq:(h + 1) * tq]
    p_hi = psum.astype(BF16)
    r1 = psum - p_hi.astype(F32)
    p_mid = r1.astype(BF16)
    p_lo = (r1 - p_mid.astype(F32)).astype(BF16)
    ov = overlap_ref[...]
    imp = _dot(p_hi, ov) + _dot(p_mid, ov) + _dot(p_lo, ov)
    n_sb = expand_ref.shape[1] // SEL_BLOCK
    t_q = qi * tq + lax.broadcasted_iota(jnp.int32, (tq, 1), 0)
    cur = lax.shift_right_logical(t_q, int(math.log2(SEL_BLOCK)))
    jb = lax.broadcasted_iota(jnp.int32, (tq, LANES), 1)
    forced = (jb == 0) | (jb == cur) | (jb == cur - 1)
    score = jnp.where(jb > cur, FUTURE_SCORE, imp + jnp.where(forced, FORCE_BONUS, 0.0))
    score = jnp.where(jb < n_sb, score, -jnp.inf)
    rank = jnp.zeros((tq, LANES), jnp.int32)
    for j2 in range(n_sb):
        col = score[:, j2:j2 + 1]
        rank += ((col > score) | ((col == score) & (j2 < jb))).astype(jnp.int32)
    sel = ((rank < SEL_TOP) & (jb < n_sb)).astype(BF16)
    mask_ref[...] = _dot(sel, expand_ref[...])

    ij = ((lax.broadcasted_iota(jnp.int32, (rows, ATT_BLK), 0) & (tq - 1))
          - lax.broadcasted_iota(jnp.int32, (rows, ATT_BLK), 1))

    def sweep(k_ref, v_ref, lo, selected):
        def step(kb, carry):
            m, l, acc = carry
            start = pl.multiple_of(kb * ATT_BLK, ATT_BLK)
            k = k_ref[pl.ds(start, ATT_BLK), :].astype(BF16)
            v = v_ref[pl.ds(start, ATT_BLK), :].astype(BF16)
            dblk = qi - kb
            s = _dot_nt(qs, k) + toep_ref[jnp.minimum(dblk, 2)].reshape(rows, ATT_BLK)
            dist = ij + dblk * ATT_BLK
            ok = dist >= 0
            if selected:
                mk = mask_ref[:, pl.ds(start, ATT_BLK)]
                ok = ok & (jnp.concatenate([mk] * hpg, axis=0) > 0.5)
            else:
                ok = ok & (dist < WINDOW)
            s = jnp.where(ok, s, NEG)
            m_new = jnp.maximum(m, jnp.max(s, axis=-1, keepdims=True))
            pr = jnp.where(ok, jnp.exp(s - m_new), 0.0)
            alpha = jnp.exp(m - m_new)
            l = alpha * l + jnp.sum(pr, axis=-1, keepdims=True)
            acc = alpha * acc + _dot(pr.astype(BF16), v)
            return m_new, l, acc

        init = (jnp.full((rows, 1), NEG, F32), jnp.zeros((rows, 1), F32), jnp.zeros((rows, HEAD_DIM), F32))
        _, l, acc = lax.fori_loop(lo, qi + 1, step, init)
        return acc / l

    o_sel = sweep(ks_ref, vs_ref, 0, True)
    o_win = sweep(kw_ref, vw_ref, jnp.maximum(qi - WINDOW // ATT_BLK, 0), False)

    gates = jax.nn.sigmoid(gates_ref[...])
    outs = []
    for h in range(hpg):
        head = g * hpg + h
        sl = slice(h * tq, (h + 1) * tq)
        lane = lax.broadcasted_iota(jnp.int32, (tq, LANES), 1)

        def gate_col(branch):
            return jnp.sum(jnp.where(lane == branch * N_ATT_HEADS + head, gates, 0.0), axis=-1, keepdims=True)

        outs.append(gate_col(0) * o_cmp[sl] + gate_col(1) * o_sel[sl] + gate_col(2) * o_win[sl])
    o_ref[...] = jnp.concatenate(outs, axis=1).astype(o_ref.dtype)


def _nsa(proj, batch, seq, cmp_pos_k, w_ck1, w_ck2, cmp_pos_v, w_cv1, w_cv2, toep, bias_c):
    n_qt = seq // ATT_BLK
    hpg = HEADS_PER_GROUP
    n_cmp_pad = seq // CMP_STRIDE
    n_sb = seq // SEL_BLOCK
    assert n_cmp_pad == LANES and n_sb <= LANES and CMP_BLOCK == 2 * CMP_STRIDE

    cs = np.arange(n_cmp_pad)[:, None] * CMP_STRIDE
    ss = np.arange(LANES)[None, :] * SEL_BLOCK
    overlap = ((cs < ss + SEL_BLOCK) & (cs + CMP_BLOCK > ss) & (np.arange(LANES)[None, :] < n_sb)
               & (np.arange(n_cmp_pad)[:, None] < n_cmp_pad - 1))
    expand = (np.arange(LANES)[:, None] == np.arange(seq)[None, :] // SEL_BLOCK)
    overlap = jnp.asarray(overlap, BF16)
    expand = jnp.asarray(expand, BF16)

    kv_col0 = COL_KV // HEAD_DIM

    def kv_spec(idx):
        return pl.BlockSpec((seq, HEAD_DIM), lambda b, g, qi, idx=idx: (b, kv_col0 + idx * N_KV_GROUPS + g))

    def whole(arr):
        return pl.BlockSpec(arr.shape, lambda b, g, qi, nd=arr.ndim: (0,) * nd)

    in_specs = [
        pl.BlockSpec((ATT_BLK, hpg * HEAD_DIM), lambda b, g, qi: (b * n_qt + qi, g)),
        kv_spec(0), kv_spec(1), kv_spec(2), kv_spec(3), kv_spec(4), kv_spec(5),
        pl.BlockSpec((ATT_BLK, LANES), lambda b, g, qi: (b * n_qt + qi, COL_GATES // LANES)),
        whole(cmp_pos_k), whole(w_ck1), whole(w_ck2), whole(cmp_pos_v), whole(w_cv1), whole(w_cv2),
        pl.BlockSpec((hpg, ATT_BLK, n_cmp_pad), lambda b, g, qi: (g, qi, 0)),
        pl.BlockSpec((3, hpg, ATT_BLK, ATT_BLK), lambda b, g, qi: (0, g, 0, 0)),
        whole(overlap), whole(expand),
    ]
    return pl.pallas_call(
        _nsa_body,
        grid=(batch, N_KV_GROUPS, n_qt),
        in_specs=in_specs,
        out_specs=pl.BlockSpec((ATT_BLK, hpg * HEAD_DIM), lambda b, g, qi: (b * n_qt + qi, g)),
        out_shape=jax.ShapeDtypeStruct((batch * seq, D_ATT), BF16),
        scratch_shapes=[pltpu.VMEM((n_cmp_pad, HEAD_DIM), BF16), pltpu.VMEM((n_cmp_pad, HEAD_DIM), BF16),
                        pltpu.VMEM((ATT_BLK, seq), F32)],
        compiler_params=pltpu.CompilerParams(
            dimension_semantics=("parallel", "parallel", "arbitrary"), vmem_limit_bytes=VMEM_LIMIT_BYTES),
        name="nsa",
    )(proj, proj, proj, proj, proj, proj, proj, proj,
      cmp_pos_k, w_ck1.astype(BF16), w_ck2.astype(BF16), cmp_pos_v, w_cv1.astype(BF16), w_cv2.astype(BF16),
      bias_c, toep, overlap, expand)


def _pool_body(u_ref, w_ref, scale_ref, o_ref):
    gi = pl.program_id(1)
    u = u_ref[...]
    seq = u.shape[0]
    t = lax.broadcasted_iota(jnp.int32, (seq, 1), 0)

    def shifted(x, k):
        return jnp.where(t >= k, pltpu.roll(x, k, 0), 0.0)

    sums = [u]
    for n in range(int(math.log2(POOL_WINDOWS[-1]))):
        sums.append(sums[-1] + shifted(sums[-1], 2 ** n))
    wsum = sums[int(math.log2(POOL_WINDOWS[0]))]
    for idx in range(1, len(POOL_WINDOWS)):
        wsum = jnp.where(gi >= idx, sums[int(math.log2(POOL_WINDOWS[idx]))], wsum)
    window = POOL_WINDOWS[0]
    for idx in range(1, len(POOL_WINDOWS)):
        window = jnp.where(gi >= idx, POOL_WINDOWS[idx], window)
    cnt = jnp.minimum(t + 1, window).astype(F32)
    d = (wsum / cnt - u).astype(BF16)
    o_ref[...] = (_dot(d, w_ref[0]) * scale_ref[...]).astype(o_ref.dtype)


def _pool(proj, batch, seq, w_pool, pool_scale):
    n_g, dg, _ = w_pool.shape
    assert all(w == 2 ** int(math.log2(w)) for w in POOL_WINDOWS) and list(POOL_WINDOWS) == sorted(POOL_WINDOWS)
    col0 = COL_U // dg
    return pl.pallas_call(
        _pool_body,
        grid=(batch, n_g),
        in_specs=[
            pl.BlockSpec((seq, dg), lambda b, gi: (b, col0 + gi)),
            pl.BlockSpec((1, dg, dg), lambda b, gi: (gi, 0, 0)),
            pl.BlockSpec((1, dg), lambda b, gi: (0, gi)),
        ],
        out_specs=pl.BlockSpec((seq, dg), lambda b, gi: (b, gi)),
        out_shape=jax.ShapeDtypeStruct((batch * seq, n_g * dg), BF16),
        compiler_params=pltpu.CompilerParams(
            dimension_semantics=("parallel", "parallel"), vmem_limit_bytes=VMEM_LIMIT_BYTES),
        name="pool",
    )(proj, w_pool.astype(BF16), pool_scale.reshape(1, -1))


def _outproj_body(x_ref, ya_ref, yp_ref, wa_ref, wp_ref, o_ref):
    o_ref[...] = x_ref[...] + _dot(ya_ref[...], wa_ref[...]) + _dot(yp_ref[...], wp_ref[...])


def _outproj(x, y_att, y_pool, w_out):
    n, d = x.shape
    da = y_att.shape[1]
    dp = y_pool.shape[1]
    w = w_out.astype(BF16)
    return pl.pallas_call(
        _outproj_body,
        grid=(d // OUT_TN, n // OUT_TM),
        in_specs=[
            pl.BlockSpec((OUT_TM, OUT_TN), lambda j, i: (i, j)),
            pl.BlockSpec((OUT_TM, da), lambda j, i: (i, 0)),
            pl.BlockSpec((OUT_TM, dp), lambda j, i: (i, 0)),
            pl.BlockSpec((da, OUT_TN), lambda j, i: (0, j)),
            pl.BlockSpec((dp, OUT_TN), lambda j, i: (da // dp, j)),
        ],
        out_specs=pl.BlockSpec((OUT_TM, OUT_TN), lambda j, i: (i, j)),
        out_shape=jax.ShapeDtypeStruct((n, d), F32),
        compiler_params=pltpu.CompilerParams(
            dimension_semantics=("parallel", "parallel"), vmem_limit_bytes=VMEM_LIMIT_BYTES),
        name="out_proj",
    )(x, y_att, y_pool, w, w)


def kernel(x, norm_ffn1, w_ffn1_gate, w_ffn1_up, w_ffn1_down, norm_mix, w_in, cmp_pos_k, w_cmp_k1, w_cmp_k2,
           cmp_pos_v, w_cmp_v1, w_cmp_v2, w_pool, pool_scale, w_out, rel_table, norm_ffn2, w_ffn2_gate,
           w_ffn2_up, w_ffn2_down, norm_final):
    batch, seq, d = x.shape
    depth = norm_ffn1.shape[0]
    xf = x.reshape(batch * seq, d)
    toep, bias_c = _bias_tables(rel_table, seq)
    gf = norm_final.reshape(1, d)
    for l in range(depth):
        wgu, wd = _ffn_weights(w_ffn1_gate[l], w_ffn1_up[l], w_ffn1_down[l])
        xf = _ffn(xf, norm_ffn1[l].reshape(1, d), wgu, wd, gf, final_norm=False)
        proj = _inproj(xf, norm_mix[l].reshape(1, d), _inproj_weights(w_in[l]))
        y_att = _nsa(proj, batch, seq, cmp_pos_k[l], w_cmp_k1[l], w_cmp_k2[l],
                     cmp_pos_v[l], w_cmp_v1[l], w_cmp_v2[l], toep, bias_c)
        y_pool = _pool(proj, batch, seq, w_pool[l], pool_scale[l])
        xf = _outproj(xf, y_att, y_pool, w_out[l])
        wgu, wd = _ffn_weights(w_ffn2_gate[l], w_ffn2_up[l], w_ffn2_down[l])
        xf = _ffn(xf, norm_ffn2[l].reshape(1, d), wgu, wd, gf, final_norm=(l == depth - 1))
    if depth == 0:
        raise ValueError("depth must be positive")
    return xf.reshape(batch, seq, d)
```

```python
import functools
import math

import jax
import jax.numpy as jnp
import numpy as np
from jax import lax
from jax.experimental import pallas as pl
from jax.experimental.pallas import tpu as pltpu

HEAD_DIM = 128
N_ATT_HEADS = 16
N_KV_GROUPS = 4
HEADS_PER_GROUP = N_ATT_HEADS // N_KV_GROUPS
D_ATT = N_ATT_HEADS * HEAD_DIM
D_KV = N_KV_GROUPS * HEAD_DIM
N_POOL_GROUPS = 4
POOL_WINDOWS = (2, 4, 8, 16)
CMP_BLOCK = 32
CMP_STRIDE = 16
SEL_BLOCK = 64
SEL_TOP = 8
WINDOW = 512
REL_BUCKETS = 32
REL_MAX_DIST = 128
EPS = 1e-6
NEG = -1e30
FORCE_BONUS = 1e4
FUTURE_SCORE = -1e9

LANES = 128
MXU_DIM = 256
VMEM_LIMIT_BYTES = 56 * 2 ** 20

FFN_TM = 512
FFN_TF = MXU_DIM
FFN_TN = 1024
NORM_ROWS = 128
PROJ_TM = 512
PROJ_TN = MXU_DIM
ATT_BLK = 128
OUT_TM = 512
OUT_TN = 1024

BF16 = jnp.bfloat16
F32 = jnp.float32


def _dot(a, b):
    return jnp.dot(a, b, preferred_element_type=F32)


def _dot_nt(a, b):
    return lax.dot_general(a, b, (((1,), (1,)), ((), ())), preferred_element_type=F32)


def _rms(x, g):
    return x * lax.rsqrt(jnp.mean(x * x, axis=-1, keepdims=True) + EPS) * g


def _ffn_body(x_ref, g_ref, wg_ref, wu_ref, wd_ref, gf_ref, o_ref, h_ref, *, final_norm):
    j = pl.program_id(1)

    @pl.when(j == 0)
    def _():
        for r in range(0, x_ref.shape[0], NORM_ROWS):
            x = x_ref[r:r + NORM_ROWS, :]
            h_ref[r:r + NORM_ROWS, :] = _rms(x, g_ref[...]).astype(BF16)
            o_ref[r:r + NORM_ROWS, :] = x

    h = h_ref[...]
    gate = _dot(h, wg_ref[...])
    up = _dot(h, wu_ref[...])
    act = (0.5 * (gate * jax.nn.sigmoid(gate)) * up).astype(BF16)
    for c in range(0, o_ref.shape[1], FFN_TN):
        o_ref[:, c:c + FFN_TN] += _dot(act, wd_ref[:, c:c + FFN_TN])

    if final_norm:
        @pl.when(j == pl.num_programs(1) - 1)
        def _():
            for r in range(0, o_ref.shape[0], NORM_ROWS):
                o_ref[r:r + NORM_ROWS, :] = _rms(o_ref[r:r + NORM_ROWS, :], gf_ref[...])


def _ffn(x, g, w_gate, w_up, w_down, gf, *, final_norm):
    n, d = x.shape
    d_ff = w_gate.shape[1]
    return pl.pallas_call(
        functools.partial(_ffn_body, final_norm=final_norm),
        grid=(n // FFN_TM, d_ff // FFN_TF),
        in_specs=[
            pl.BlockSpec((FFN_TM, d), lambda i, j: (i, 0)),
            pl.BlockSpec((1, d), lambda i, j: (0, 0)),
            pl.BlockSpec((d, FFN_TF), lambda i, j: (0, j)),
            pl.BlockSpec((d, FFN_TF), lambda i, j: (0, j)),
            pl.BlockSpec((FFN_TF, d), lambda i, j: (j, 0)),
            pl.BlockSpec((1, d), lambda i, j: (0, 0)),
        ],
        out_specs=pl.BlockSpec((FFN_TM, d), lambda i, j: (i, 0)),
        out_shape=jax.ShapeDtypeStruct((n, d), F32),
        scratch_shapes=[pltpu.VMEM((FFN_TM, d), BF16)],
        compiler_params=pltpu.CompilerParams(
            dimension_semantics=("parallel", "arbitrary"), vmem_limit_bytes=VMEM_LIMIT_BYTES),
        name="ffn_final" if final_norm else "ffn",
    )(x, g, w_gate, w_up, w_down, gf)


COL_Q = 0
COL_KV = D_ATT
COL_U = D_ATT + 6 * D_KV
D_POOL_COLS = 2048
COL_GATES = COL_U + D_POOL_COLS
N_GATES = 3 * N_ATT_HEADS


def _inproj_body(x_ref, g_ref, wa_ref, wu_ref, wg_ref, o_ref, h_ref, *, n_a, n_u):
    j = pl.program_id(1)

    @pl.when(j == 0)
    def _():
        for r in range(0, x_ref.shape[0], NORM_ROWS):
            h_ref[r:r + NORM_ROWS, :] = _rms(x_ref[r:r + NORM_ROWS, :], g_ref[...]).astype(BF16)

    @pl.when(j < n_a)
    def _():
        o_ref[...] = _dot(h_ref[...], wa_ref[...])

    @pl.when((j >= n_a) & (j < n_a + n_u))
    def _():
        o_ref[...] = _dot(h_ref[...], wu_ref[...])

    @pl.when(j >= n_a + n_u)
    def _():
        o_ref[...] = _dot(h_ref[...], wg_ref[...])


def _inproj(x, g, w_in):
    n, d = x.shape
    d_in = w_in.shape[1]
    assert d_in - COL_U - N_GATES == D_POOL_COLS and N_GATES <= PROJ_TN
    wa = w_in[:, :COL_U].astype(BF16)
    wu = w_in[:, COL_U + N_GATES:].astype(BF16)
    wg = jnp.pad(w_in[:, COL_U:COL_U + N_GATES], ((0, 0), (0, PROJ_TN - N_GATES))).astype(BF16)
    n_a = COL_U // PROJ_TN
    n_u = D_POOL_COLS // PROJ_TN
    n_t = n_a + n_u + 1
    return pl.pallas_call(
        functools.partial(_inproj_body, n_a=n_a, n_u=n_u),
        grid=(n // PROJ_TM, n_t),
        in_specs=[
            pl.BlockSpec((PROJ_TM, d), lambda i, j: (i, 0)),
            pl.BlockSpec((1, d), lambda i, j: (0, 0)),
            pl.BlockSpec((d, PROJ_TN), lambda i, j: (0, jnp.minimum(j, n_a - 1))),
            pl.BlockSpec((d, PROJ_TN), lambda i, j: (0, jnp.clip(j - n_a, 0, n_u - 1))),
            pl.BlockSpec((d, PROJ_TN), lambda i, j: (0, 0)),
        ],
        out_specs=pl.BlockSpec((PROJ_TM, PROJ_TN), lambda i, j: (i, j)),
        out_shape=jax.ShapeDtypeStruct((n, n_t * PROJ_TN), F32),
        scratch_shapes=[pltpu.VMEM((PROJ_TM, d), BF16)],
        compiler_params=pltpu.CompilerParams(
            dimension_semantics=("parallel", "arbitrary"), vmem_limit_bytes=VMEM_LIMIT_BYTES),
        name="in_proj",
    )(x, g, wa, wu, wg)


def _rel_bucket_np(n):
    max_exact = REL_BUCKETS // 2
    n = np.maximum(n, 0)
    nf = np.maximum(n, 1).astype(np.float32)
    large = max_exact + (np.log(nf / max_exact) / math.log(REL_MAX_DIST / max_exact)
                         * (REL_BUCKETS - max_exact)).astype(np.int32)
    large = np.minimum(large, REL_BUCKETS - 1)
    return np.where(n < max_exact, n, large)


FAR_DIST = int(np.max(np.nonzero(_rel_bucket_np(np.arange(4 * REL_MAX_DIST)) < REL_BUCKETS - 1)[0])) + 1
CMP_LANE_OFF = -(-(FAR_DIST + CMP_BLOCK - 1) // CMP_STRIDE) - 1
TAB_NEAR, TAB_CMP, TAB_FAR = 0, 1, 2


def _bias_tables(rel_table):
    blk = ATT_BLK
    assert blk >= FAR_DIST and blk % CMP_STRIDE == 0 and blk == LANES
    far_bucket = REL_BUCKETS - 1
    i = np.arange(blk)[:, None]
    k = np.arange(LANES)[None, :]
    idx_near = _rel_bucket_np((i - k) % blk)
    dist_c = i - (k - CMP_LANE_OFF) * CMP_STRIDE - (CMP_BLOCK - 1)
    n_band = CMP_LANE_OFF + (blk - CMP_BLOCK) // CMP_STRIDE + 1
    idx_cmp = np.where(k < n_band, _rel_bucket_np(dist_c), far_bucket)
    idx_far = np.full((blk, LANES), far_bucket)
    idx = np.stack([idx_near, idx_cmp, idx_far]).reshape(-1)
    onehot = jnp.asarray(np.arange(REL_BUCKETS)[:, None] == idx[None, :], F32)
    tabs = jnp.einsum("kh,kn->hn", rel_table.astype(F32), onehot, precision=lax.Precision.HIGHEST)
    return tabs.reshape(rel_table.shape[1], 3, blk, LANES)


def _compress(kv_ref, pos_ref, w1_ref, w2_ref):
    seq, dk = kv_ref.shape
    n_half = seq // CMP_STRIDE
    first = jnp.zeros((n_half, w1_ref.shape[1]), F32)
    second = jnp.zeros((n_half, w1_ref.shape[1]), F32)
    for l in range(CMP_STRIDE):
        rows = kv_ref[pl.ds(l, n_half, stride=CMP_STRIDE), :]
        a = (rows + pos_ref[l:l + 1, :]).astype(BF16)
        b = (rows + pos_ref[CMP_STRIDE + l:CMP_STRIDE + l + 1, :]).astype(BF16)
        first += _dot(a, w1_ref[l * dk:(l + 1) * dk, :])
        second += _dot(b, w1_ref[(CMP_STRIDE + l) * dk:(CMP_STRIDE + l + 1) * dk, :])
    pre = first + pltpu.roll(second, n_half - 1, 0)
    hid = (pre * jax.nn.sigmoid(pre)).astype(BF16)
    return _dot(hid, w2_ref[...])


def _nsa_body(q_ref, kc_ref, vc_ref, ks_ref, vs_ref, kw_ref, vw_ref, gates_ref,
              posk_ref, wk1_ref, wk2_ref, posv_ref, wv1_ref, wv2_ref,
              tab_ref, overlap_ref, expand_ref,
              o_ref, kcmp_ref, vcmp_ref, mask_ref):
    g = pl.program_id(1)
    qi = pl.program_id(2)
    tq = ATT_BLK
    hpg = HEADS_PER_GROUP
    rows = hpg * tq
    n_cmp = kcmp_ref.shape[0]

    @pl.when(qi == 0)
    def _():
        kcmp_ref[...] = _compress(kc_ref, posk_ref, wk1_ref, wk2_ref).astype(BF16)
        vcmp_ref[...] = _compress(vc_ref, posv_ref, wv1_ref, wv2_ref).astype(BF16)

    q = q_ref[...] * (HEAD_DIM ** -0.5)
    qs = jnp.concatenate([q[:, h * HEAD_DIM:(h + 1) * HEAD_DIM] for h in range(hpg)],
                         axis=0).astype(BF16)

    assert tq & (tq - 1) == 0 and SEL_BLOCK & (SEL_BLOCK - 1) == 0
    t_rows = qi * tq + (lax.broadcasted_iota(jnp.int32, (rows, 1), 0) & (tq - 1))
    c_idx = lax.broadcasted_iota(jnp.int32, (rows, n_cmp), 1)
    valid_c = t_rows - (c_idx * CMP_STRIDE + CMP_BLOCK - 1) >= 0
    shift = (qi * (tq // CMP_STRIDE) + n_cmp - CMP_LANE_OFF) & (n_cmp - 1)
    bias_c = pltpu.roll(tab_ref[:, TAB_CMP].reshape(rows, n_cmp), shift, 1)
    logit = jnp.where(valid_c, _dot_nt(qs, kcmp_ref[...]) + bias_c, NEG)
    e = jnp.exp(logit - jnp.max(logit, axis=-1, keepdims=True))
    any_valid = (t_rows >= CMP_BLOCK - 1).astype(F32)
    p = e / jnp.sum(e, axis=-1, keepdims=True) * any_valid
    o_cmp = _dot(p.astype(BF16), vcmp_ref[...])

    psum = p[0:tq]
    for h in range(1, hpg):
        psum = psum + p[h * tq:(h + 1) * tq]
    p_hi = psum.astype(BF16)
    r1 = psum - p_hi.astype(F32)
    p_mid = r1.astype(BF16)
    p_lo = (r1 - p_mid.astype(F32)).astype(BF16)
    ov = overlap_ref[...]
    imp = _dot(p_hi, ov) + _dot(p_mid, ov) + _dot(p_lo, ov)
    n_sb = expand_ref.shape[1] // SEL_BLOCK
    t_q = qi * tq + lax.broadcasted_iota(jnp.int32, (tq, 1), 0)
    cur = lax.shift_right_logical(t_q, int(math.log2(SEL_BLOCK)))
    jb = lax.broadcasted_iota(jnp.int32, (tq, LANES), 1)
    forced = (jb == 0) | (jb == cur) | (jb == cur - 1)
    score = jnp.where(jb > cur, FUTURE_SCORE, imp + jnp.where(forced, FORCE_BONUS, 0.0))
    score = jnp.where(jb < n_sb, score, -jnp.inf)
    rank = jnp.zeros((tq, LANES), jnp.int32)
    for j2 in range(n_sb):
        col = score[:, j2:j2 + 1]
        rank += ((col > score) | ((col == score) & (j2 < jb))).astype(jnp.int32)
    sel = ((rank < SEL_TOP) & (jb < n_sb)).astype(BF16)
    mask_ref[...] = _dot(sel, expand_ref[...])

    ij = ((lax.broadcasted_iota(jnp.int32, (rows, ATT_BLK), 0) & (tq - 1))
          - lax.broadcasted_iota(jnp.int32, (rows, ATT_BLK), 1))

    def sweep(k_ref, v_ref, lo, selected):
        def step(kb, carry):
            m, l, acc = carry
            start = pl.multiple_of(kb * ATT_BLK, ATT_BLK)
            k = k_ref[pl.ds(start, ATT_BLK), :].astype(BF16)
            v = v_ref[pl.ds(start, ATT_BLK), :].astype(BF16)
            dist = ij + (qi - kb) * ATT_BLK
            bias = jnp.where(dist < ATT_BLK, tab_ref[:, TAB_NEAR].reshape(rows, ATT_BLK),
                             tab_ref[:, TAB_FAR].reshape(rows, ATT_BLK))
            s = _dot_nt(qs, k) + bias
            ok = dist >= 0
            if selected:
                mk = mask_ref[:, pl.ds(start, ATT_BLK)]
                ok = ok & (jnp.concatenate([mk] * hpg, axis=0) > 0.5)
            else:
                ok = ok & (dist < WINDOW)
            s = jnp.where(ok, s, NEG)
            m_new = jnp.maximum(m, jnp.max(s, axis=-1, keepdims=True))
            pr = jnp.where(ok, jnp.exp(s - m_new), 0.0)
            alpha = jnp.exp(m - m_new)
            l = alpha * l + jnp.sum(pr, axis=-1, keepdims=True)
            acc = alpha * acc + _dot(pr.astype(BF16), v)
            return m_new, l, acc

        init = (jnp.full((rows, 1), NEG, F32), jnp.zeros((rows, 1), F32), jnp.zeros((rows, HEAD_DIM), F32))
        _, l, acc = lax.fori_loop(lo, qi + 1, step, init)
        return acc / l

    o_sel = sweep(ks_ref, vs_ref, 0, True)
    o_win = sweep(kw_ref, vw_ref, jnp.maximum(qi - WINDOW // ATT_BLK, 0), False)

    gates = jax.nn.sigmoid(gates_ref[...])
    outs = []
    for h in range(hpg):
        head = g * hpg + h
        sl = slice(h * tq, (h + 1) * tq)
        lane = lax.broadcasted_iota(jnp.int32, (tq, LANES), 1)

        def gate_col(branch):
            return jnp.sum(jnp.where(lane == branch * N_ATT_HEADS + head, gates, 0.0), axis=-1, keepdims=True)

        outs.append(gate_col(0) * o_cmp[sl] + gate_col(1) * o_sel[sl] + gate_col(2) * o_win[sl])
    o_ref[...] = jnp.concatenate(outs, axis=1).astype(o_ref.dtype)


def _nsa(proj, batch, seq, cmp_pos_k, w_ck1, w_ck2, cmp_pos_v, w_cv1, w_cv2, bias_tabs):
    n_qt = seq // ATT_BLK
    hpg = HEADS_PER_GROUP
    n_cmp_pad = seq // CMP_STRIDE
    n_sb = seq // SEL_BLOCK
    assert n_cmp_pad == LANES and n_sb <= LANES and CMP_BLOCK == 2 * CMP_STRIDE

    cs = np.arange(n_cmp_pad)[:, None] * CMP_STRIDE
    ss = np.arange(LANES)[None, :] * SEL_BLOCK
    overlap = ((cs < ss + SEL_BLOCK) & (cs + CMP_BLOCK > ss) & (np.arange(LANES)[None, :] < n_sb)
               & (np.arange(n_cmp_pad)[:, None] < n_cmp_pad - 1))
    expand = (np.arange(LANES)[:, None] == np.arange(seq)[None, :] // SEL_BLOCK)
    overlap = jnp.asarray(overlap, BF16)
    expand = jnp.asarray(expand, BF16)

    kv_col0 = COL_KV // HEAD_DIM

    def kv_spec(idx):
        return pl.BlockSpec((seq, HEAD_DIM), lambda b, g, qi, idx=idx: (b, kv_col0 + idx * N_KV_GROUPS + g))

    def whole(arr):
        return pl.BlockSpec(arr.shape, lambda b, g, qi, nd=arr.ndim: (0,) * nd)

    in_specs = [
        pl.BlockSpec((ATT_BLK, hpg * HEAD_DIM), lambda b, g, qi: (b * n_qt + qi, g)),
        kv_spec(0), kv_spec(1), kv_spec(2), kv_spec(3), kv_spec(4), kv_spec(5),
        pl.BlockSpec((ATT_BLK, LANES), lambda b, g, qi: (b * n_qt + qi, COL_GATES // LANES)),
        whole(cmp_pos_k), whole(w_ck1), whole(w_ck2), whole(cmp_pos_v), whole(w_cv1), whole(w_cv2),
        pl.BlockSpec((hpg,) + bias_tabs.shape[1:], lambda b, g, qi: (g, 0, 0, 0)),
        whole(overlap), whole(expand),
    ]
    return pl.pallas_call(
        _nsa_body,
        grid=(batch, N_KV_GROUPS, n_qt),
        in_specs=in_specs,
        out_specs=pl.BlockSpec((ATT_BLK, hpg * HEAD_DIM), lambda b, g, qi: (b * n_qt + qi, g)),
        out_shape=jax.ShapeDtypeStruct((batch * seq, D_ATT), BF16),
        scratch_shapes=[pltpu.VMEM((n_cmp_pad, HEAD_DIM), BF16), pltpu.VMEM((n_cmp_pad, HEAD_DIM), BF16),
                        pltpu.VMEM((ATT_BLK, seq), F32)],
        compiler_params=pltpu.CompilerParams(
            dimension_semantics=("parallel", "parallel", "arbitrary"), vmem_limit_bytes=VMEM_LIMIT_BYTES),
        name="nsa",
    )(proj, proj, proj, proj, proj, proj, proj, proj,
      cmp_pos_k, w_ck1.astype(BF16), w_ck2.astype(BF16), cmp_pos_v, w_cv1.astype(BF16), w_cv2.astype(BF16),
      bias_tabs, overlap, expand)


def _pool_body(u_ref, w_ref, scale_ref, o_ref):
    gi = pl.program_id(1)
    u = u_ref[...]
    seq = u.shape[0]
    t = lax.broadcasted_iota(jnp.int32, (seq, 1), 0)

    def shifted(x, k):
        return jnp.where(t >= k, pltpu.roll(x, k, 0), 0.0)

    sums = [u]
    for n in range(int(math.log2(POOL_WINDOWS[-1]))):
        sums.append(sums[-1] + shifted(sums[-1], 2 ** n))
    wsum = sums[int(math.log2(POOL_WINDOWS[0]))]
    for idx in range(1, len(POOL_WINDOWS)):
        wsum = jnp.where(gi >= idx, sums[int(math.log2(POOL_WINDOWS[idx]))], wsum)
    window = POOL_WINDOWS[0]
    for idx in range(1, len(POOL_WINDOWS)):
        window = jnp.where(gi >= idx, POOL_WINDOWS[idx], window)
    cnt = jnp.minimum(t + 1, window).astype(F32)
    d = (wsum / cnt - u).astype(BF16)
    o_ref[...] = (_dot(d, w_ref[0]) * scale_ref[...]).astype(o_ref.dtype)


def _pool(proj, batch, seq, w_pool, pool_scale):
    n_g, dg, _ = w_pool.shape
    assert all(w == 2 ** int(math.log2(w)) for w in POOL_WINDOWS) and list(POOL_WINDOWS) == sorted(POOL_WINDOWS)
    col0 = COL_U // dg
    return pl.pallas_call(
        _pool_body,
        grid=(batch, n_g),
        in_specs=[
            pl.BlockSpec((seq, dg), lambda b, gi: (b, col0 + gi)),
            pl.BlockSpec((1, dg, dg), lambda b, gi: (gi, 0, 0)),
            pl.BlockSpec((1, dg), lambda b, gi: (0, gi)),
        ],
        out_specs=pl.BlockSpec((seq, dg), lambda b, gi: (b, gi)),
        out_shape=jax.ShapeDtypeStruct((batch * seq, n_g * dg), BF16),
        compiler_params=pltpu.CompilerParams(
            dimension_semantics=("parallel", "parallel"), vmem_limit_bytes=VMEM_LIMIT_BYTES),
        name="pool",
    )(proj, w_pool.astype(BF16), pool_scale.reshape(1, -1))


def _outproj_body(x_ref, ya_ref, yp_ref, wa_ref, wp_ref, o_ref):
    o_ref[...] = x_ref[...] + _dot(ya_ref[...], wa_ref[...]) + _dot(yp_ref[...], wp_ref[...])


def _outproj(x, y_att, y_pool, w_out):
    n, d = x.shape
    da = y_att.shape[1]
    dp = y_pool.shape[1]
    w = w_out.astype(BF16)
    return pl.pallas_call(
        _outproj_body,
        grid=(d // OUT_TN, n // OUT_TM),
        in_specs=[
            pl.BlockSpec((OUT_TM, OUT_TN), lambda j, i: (i, j)),
            pl.BlockSpec((OUT_TM, da), lambda j, i: (i, 0)),
            pl.BlockSpec((OUT_TM, dp), lambda j, i: (i, 0)),
            pl.BlockSpec((da, OUT_TN), lambda j, i: (0, j)),
            pl.BlockSpec((dp, OUT_TN), lambda j, i: (da // dp, j)),
        ],
        out_specs=pl.BlockSpec((OUT_TM, OUT_TN), lambda j, i: (i, j)),
        out_shape=jax.ShapeDtypeStruct((n, d), F32),
        compiler_params=pltpu.CompilerParams(
            dimension_semantics=("parallel", "parallel"), vmem_limit_bytes=VMEM_LIMIT_BYTES),
        name="out_proj",
    )(x, y_att, y_pool, w, w)


def kernel(x, norm_ffn1, w_ffn1_gate, w_ffn1_up, w_ffn1_down, norm_mix, w_in, cmp_pos_k, w_cmp_k1, w_cmp_k2,
           cmp_pos_v, w_cmp_v1, w_cmp_v2, w_pool, pool_scale, w_out, rel_table, norm_ffn2, w_ffn2_gate,
           w_ffn2_up, w_ffn2_down, norm_final):
    batch, seq, d = x.shape
    depth = norm_ffn1.shape[0]
    xf = x.reshape(batch * seq, d)
    bias_tabs = _bias_tables(rel_table)
    gf = norm_final.reshape(1, d)
    for l in range(depth):
        xf = _ffn(xf, norm_ffn1[l].reshape(1, d), w_ffn1_gate[l].astype(BF16), w_ffn1_up[l].astype(BF16),
                  w_ffn1_down[l].astype(BF16), gf, final_norm=False)
        proj = _inproj(xf, norm_mix[l].reshape(1, d), w_in[l])
        y_att = _nsa(proj, batch, seq, cmp_pos_k[l], w_cmp_k1[l], w_cmp_k2[l],
                     cmp_pos_v[l], w_cmp_v1[l], w_cmp_v2[l], bias_tabs)
        y_pool = _pool(proj, batch, seq, w_pool[l], pool_scale[l])
        xf = _outproj(xf, y_att, y_pool, w_out[l])
        xf = _ffn(xf, norm_ffn2[l].reshape(1, d), w_ffn2_gate[l].astype(BF16), w_ffn2_up[l].astype(BF16),
                  w_ffn2_down[l].astype(BF16), gf, final_norm=(l == depth - 1))
    if depth == 0:
        raise ValueError("depth must be positive")
    return xf.reshape(batch, seq, d)
```

```python
import functools
import math

import jax
import jax.numpy as jnp
import numpy as np
from jax import lax
from jax.experimental import pallas as pl
from jax.experimental.pallas import tpu as pltpu

HEAD_DIM = 128
N_ATT_HEADS = 16
N_KV_GROUPS = 4
HEADS_PER_GROUP = N_ATT_HEADS // N_KV_GROUPS
D_ATT = N_ATT_HEADS * HEAD_DIM
D_KV = N_KV_GROUPS * HEAD_DIM
N_POOL_GROUPS = 4
POOL_WINDOWS = (2, 4, 8, 16)
CMP_BLOCK = 32
CMP_STRIDE = 16
SEL_BLOCK = 64
SEL_TOP = 8
WINDOW = 512
REL_BUCKETS = 32
REL_MAX_DIST = 128
EPS = 1e-6
NEG = -1e30
FORCE_BONUS = 1e4
FUTURE_SCORE = -1e9

LANES = 128
MXU_DIM = 256
VMEM_LIMIT_BYTES = 56 * 2 ** 20

FFN_TM = 512
FFN_TF = MXU_DIM
FFN_TN = 1024
NORM_ROWS = 128
PROJ_TM = 512
PROJ_TN = MXU_DIM
ATT_BLK = 256
OUT_TM = 512
OUT_TN = 1024

BF16 = jnp.bfloat16
F32 = jnp.float32


def _dot(a, b):
    return jnp.dot(a, b, preferred_element_type=F32)


def _rms(x, g):
    return x * lax.rsqrt(jnp.mean(x * x, axis=-1, keepdims=True) + EPS) * g


def _ffn_body(x_ref, g_ref, wg_ref, wu_ref, wd_ref, gf_ref, o_ref, h_ref, *, final_norm):
    j = pl.program_id(1)

    @pl.when(j == 0)
    def _():
        for r in range(0, x_ref.shape[0], NORM_ROWS):
            x = x_ref[r:r + NORM_ROWS, :]
            h_ref[r:r + NORM_ROWS, :] = _rms(x, g_ref[...]).astype(BF16)
            o_ref[r:r + NORM_ROWS, :] = x

    h = h_ref[...]
    gate = _dot(h, wg_ref[...])
    up = _dot(h, wu_ref[...])
    act = (0.5 * (gate * jax.nn.sigmoid(gate)) * up).astype(BF16)
    for c in range(0, o_ref.shape[1], FFN_TN):
        o_ref[:, c:c + FFN_TN] += _dot(act, wd_ref[:, c:c + FFN_TN])

    if final_norm:
        @pl.when(j == pl.num_programs(1) - 1)
        def _():
            for r in range(0, o_ref.shape[0], NORM_ROWS):
                o_ref[r:r + NORM_ROWS, :] = _rms(o_ref[r:r + NORM_ROWS, :], gf_ref[...])


def _ffn(x, g, w_gate, w_up, w_down, gf, *, final_norm):
    n, d = x.shape
    d_ff = w_gate.shape[1]
    return pl.pallas_call(
        functools.partial(_ffn_body, final_norm=final_norm),
        grid=(n // FFN_TM, d_ff // FFN_TF),
        in_specs=[
            pl.BlockSpec((FFN_TM, d), lambda i, j: (i, 0)),
            pl.BlockSpec((1, d), lambda i, j: (0, 0)),
            pl.BlockSpec((d, FFN_TF), lambda i, j: (0, j)),
            pl.BlockSpec((d, FFN_TF), lambda i, j: (0, j)),
            pl.BlockSpec((FFN_TF, d), lambda i, j: (j, 0)),
            pl.BlockSpec((1, d), lambda i, j: (0, 0)),
        ],
        out_specs=pl.BlockSpec((FFN_TM, d), lambda i, j: (i, 0)),
        out_shape=jax.ShapeDtypeStruct((n, d), F32),
        scratch_shapes=[pltpu.VMEM((FFN_TM, d), BF16)],
        compiler_params=pltpu.CompilerParams(
            dimension_semantics=("parallel", "arbitrary"), vmem_limit_bytes=VMEM_LIMIT_BYTES),
        name="ffn_final" if final_norm else "ffn",
    )(x, g, w_gate, w_up, w_down, gf)


COL_Q = 0
COL_KV = D_ATT
COL_U = D_ATT + 6 * D_KV
D_POOL_COLS = 2048
COL_GATES = COL_U + D_POOL_COLS
N_GATES = 3 * N_ATT_HEADS


def _inproj_body(x_ref, g_ref, wa_ref, wu_ref, wg_ref, o_ref, h_ref, *, n_a, n_u):
    j = pl.program_id(1)

    @pl.when(j == 0)
    def _():
        for r in range(0, x_ref.shape[0], NORM_ROWS):
            h_ref[r:r + NORM_ROWS, :] = _rms(x_ref[r:r + NORM_ROWS, :], g_ref[...]).astype(BF16)

    @pl.when(j < n_a)
    def _():
        o_ref[...] = _dot(h_ref[...], wa_ref[...])

    @pl.when((j >= n_a) & (j < n_a + n_u))
    def _():
        o_ref[...] = _dot(h_ref[...], wu_ref[...])

    @pl.when(j >= n_a + n_u)
    def _():
        o_ref[...] = _dot(h_ref[...], wg_ref[...])


def _inproj(x, g, w_in):
    n, d = x.shape
    d_in = w_in.shape[1]
    assert d_in - COL_U - N_GATES == D_POOL_COLS and N_GATES <= PROJ_TN
    wa = w_in[:, :COL_U].astype(BF16)
    wu = w_in[:, COL_U + N_GATES:].astype(BF16)
    wg = jnp.pad(w_in[:, COL_U:COL_U + N_GATES], ((0, 0), (0, PROJ_TN - N_GATES))).astype(BF16)
    n_a = COL_U // PROJ_TN
    n_u = D_POOL_COLS // PROJ_TN
    n_t = n_a + n_u + 1
    return pl.pallas_call(
        functools.partial(_inproj_body, n_a=n_a, n_u=n_u),
        grid=(n // PROJ_TM, n_t),
        in_specs=[
            pl.BlockSpec((PROJ_TM, d), lambda i, j: (i, 0)),
            pl.BlockSpec((1, d), lambda i, j: (0, 0)),
            pl.BlockSpec((d, PROJ_TN), lambda i, j: (0, jnp.minimum(j, n_a - 1))),
            pl.BlockSpec((d, PROJ_TN), lambda i, j: (0, jnp.clip(j - n_a, 0, n_u - 1))),
            pl.BlockSpec((d, PROJ_TN), lambda i, j: (0, 0)),
        ],
        out_specs=pl.BlockSpec((PROJ_TM, PROJ_TN), lambda i, j: (i, j)),
        out_shape=jax.ShapeDtypeStruct((n, n_t * PROJ_TN), F32),
        scratch_shapes=[pltpu.VMEM((PROJ_TM, d), BF16)],
        compiler_params=pltpu.CompilerParams(
            dimension_semantics=("parallel", "arbitrary"), vmem_limit_bytes=VMEM_LIMIT_BYTES),
        name="in_proj",
    )(x, g, wa, wu, wg)


def _rel_bucket_np(n):
    max_exact = REL_BUCKETS // 2
    n = np.maximum(n, 0)
    nf = np.maximum(n, 1).astype(np.float32)
    large = max_exact + (np.log(nf / max_exact) / math.log(REL_MAX_DIST / max_exact)
                         * (REL_BUCKETS - max_exact)).astype(np.int32)
    large = np.minimum(large, REL_BUCKETS - 1)
    return np.where(n < max_exact, n, large)


FAR_DIST = int(np.max(np.nonzero(_rel_bucket_np(np.arange(4 * REL_MAX_DIST)) < REL_BUCKETS - 1)[0])) + 1
CMP_ROW_OFF = -(-(FAR_DIST + CMP_BLOCK - 1) // CMP_STRIDE) - 1
TAB_DIAG, TAB_PREV, TAB_FAR, TAB_EDGE = 0, 1, 2, 3


def _pick(rel_table, idx):
    onehot = jnp.asarray(np.arange(REL_BUCKETS)[:, None] == idx[None, :], F32)
    return jnp.einsum("kh,kn->hn", rel_table.astype(F32), onehot, precision=lax.Precision.HIGHEST)


def _group_lanes(t):
    t = t.reshape((N_KV_GROUPS, HEADS_PER_GROUP) + t.shape[1:])
    t = jnp.moveaxis(t, 1, -2)
    return t.reshape(t.shape[:-2] + (HEADS_PER_GROUP * t.shape[-1],))


def _bias_tables(rel_table, n_cmp):
    blk = ATT_BLK
    n_heads = rel_table.shape[1]
    far = REL_BUCKETS - 1
    assert blk >= FAR_DIST and WINDOW % blk == 0
    d = np.arange(2 * blk) - (blk - 1)
    idx = np.stack([_rel_bucket_np(d), _rel_bucket_np(blk + d), np.full_like(d, far), np.full_like(d, far)])
    mask = np.stack([np.where(d < 0, NEG, 0.0), np.zeros_like(d, np.float64), np.zeros_like(d, np.float64),
                     np.where(d >= 0, NEG, 0.0)]).astype(np.float32)
    v = _pick(rel_table, idx.reshape(-1)).reshape(n_heads, 4, 2 * blk) + mask[None]
    t = jnp.tile(v, (1, 1, blk))[..., :blk * (2 * blk - 1)].reshape(n_heads, 4, blk, 2 * blk - 1)[..., blk - 1:]
    sweep = _group_lanes(t)

    rows = np.arange(n_cmp - CMP_ROW_OFF, n_cmp + (blk - CMP_BLOCK) // CMP_STRIDE + 1)
    dist = np.arange(blk)[None, :] - ((rows[:, None] - n_cmp) * CMP_STRIDE + CMP_BLOCK - 1)
    band = _pick(rel_table, _rel_bucket_np(dist).reshape(-1)).reshape(n_heads, len(rows), blk)
    band = band + np.where(dist < 0, NEG, 0.0).astype(np.float32)[None]
    before = jnp.broadcast_to(rel_table[far].astype(F32)[:, None, None], (n_heads, int(rows[0]), blk))
    after = jnp.full((n_heads, 2 * n_cmp - int(rows[-1]) - 1, blk), NEG, F32)
    cmp_tab = _group_lanes(jnp.concatenate([before, band, after], axis=1))
    return sweep, cmp_tab


def _compress(kv_ref, pos_ref, w1_ref, w2_ref):
    seq, dk = kv_ref.shape
    n_half = seq // CMP_STRIDE
    first = jnp.zeros((n_half, w1_ref.shape[1]), F32)
    second = jnp.zeros((n_half, w1_ref.shape[1]), F32)
    for l in range(CMP_STRIDE):
        rows = kv_ref[pl.ds(l, n_half, stride=CMP_STRIDE), :]
        a = (rows + pos_ref[l:l + 1, :]).astype(BF16)
        b = (rows + pos_ref[CMP_STRIDE + l:CMP_STRIDE + l + 1, :]).astype(BF16)
        first += _dot(a, w1_ref[l * dk:(l + 1) * dk, :])
        second += _dot(b, w1_ref[(CMP_STRIDE + l) * dk:(CMP_STRIDE + l + 1) * dk, :])
    pre = first + pltpu.roll(second, n_half - 1, 0)
    hid = (pre * jax.nn.sigmoid(pre)).astype(BF16)
    return _dot(hid, w2_ref[...])


def _nsa_body(q_ref, kc_ref, vc_ref, ks_ref, vs_ref, kw_ref, vw_ref, gates_ref,
              posk_ref, wk1_ref, wk2_ref, posv_ref, wv1_ref, wv2_ref,
              tab_ref, tabc_ref, overlap_ref, expand_ref,
              o_ref, kcmp_ref, vcmp_t_ref, ksel_ref, vsel_t_ref, kwin_ref, vwin_t_ref, gates_t_ref):
    g = pl.program_id(1)
    qi = pl.program_id(2)
    tq = ATT_BLK
    hpg = HEADS_PER_GROUP
    lanes = hpg * tq
    n_cmp = kcmp_ref.shape[0]
    n_sb = overlap_ref.shape[0]
    seq = ks_ref.shape[0]
    assert tq & (tq - 1) == 0 and SEL_BLOCK & (SEL_BLOCK - 1) == 0

    @pl.when(qi == 0)
    def _():
        kcmp_ref[...] = _compress(kc_ref, posk_ref, wk1_ref, wk2_ref).astype(BF16)
        vcmp_t_ref[...] = _compress(vc_ref, posv_ref, wv1_ref, wv2_ref).astype(BF16).T
        for c in range(0, seq, LANES):
            ksel_ref[c:c + LANES, :] = ks_ref[c:c + LANES, :].astype(BF16)
            kwin_ref[c:c + LANES, :] = kw_ref[c:c + LANES, :].astype(BF16)
            vsel_t_ref[:, c:c + LANES] = vs_ref[c:c + LANES, :].T.astype(BF16)
            vwin_t_ref[:, c:c + LANES] = vw_ref[c:c + LANES, :].T.astype(BF16)

    q = q_ref[...] * (HEAD_DIM ** -0.5)
    q_t = jnp.concatenate([q[:, h * HEAD_DIM:(h + 1) * HEAD_DIM].T for h in range(hpg)],
                          axis=1).astype(BF16)
    t_lane = qi * tq + (lax.broadcasted_iota(jnp.int32, (1, lanes), 1) & (tq - 1))

    first_row = pl.multiple_of(n_cmp - qi * (tq // CMP_STRIDE), 8)
    logit = _dot(kcmp_ref[...], q_t) + tabc_ref[pl.ds(first_row, n_cmp), :]
    e = jnp.exp(logit - jnp.max(logit, axis=0, keepdims=True))
    any_valid = (t_lane >= CMP_BLOCK - 1).astype(F32)
    p = e / jnp.sum(e, axis=0, keepdims=True) * any_valid
    o_cmp = _dot(vcmp_t_ref[...], p.astype(BF16))

    psum = p[:, 0:tq]
    for h in range(1, hpg):
        psum = psum + p[:, h * tq:(h + 1) * tq]
    p_hi = psum.astype(BF16)
    r1 = psum - p_hi.astype(F32)
    p_mid = r1.astype(BF16)
    p_lo = (r1 - p_mid.astype(F32)).astype(BF16)
    ov = overlap_ref[...]
    imp = _dot(ov, p_hi) + _dot(ov, p_mid) + _dot(ov, p_lo)
    cur = lax.shift_right_logical(t_lane[:, 0:tq], int(math.log2(SEL_BLOCK)))
    jb = lax.broadcasted_iota(jnp.int32, (n_sb, tq), 0)
    forced = (jb == 0) | (jb == cur) | (jb == cur - 1)
    score = jnp.where(jb > cur, FUTURE_SCORE, imp + jnp.where(forced, FORCE_BONUS, 0.0))
    rank = jnp.zeros((n_sb, tq), jnp.int32)
    for j2 in range(n_sb):
        row = score[j2:j2 + 1, :]
        rank += ((row > score) | ((row == score) & (j2 < jb))).astype(jnp.int32)
    sel = (rank < SEL_TOP).astype(BF16)
    n_aug = expand_ref.shape[1]
    sel_aug = jnp.concatenate([sel, jnp.ones((16, tq), BF16), jnp.zeros((n_aug - n_sb - 16, tq), BF16)], axis=0)

    def sweep(k_ref, v_t_ref, lo, selected):
        def step(kb, carry):
            m, l, acc = carry
            start = pl.multiple_of(kb * ATT_BLK, ATT_BLK)
            dblk = qi - kb
            tab = jnp.minimum(dblk, TAB_FAR)
            if not selected:
                tab = jnp.where(dblk == WINDOW // ATT_BLK, TAB_EDGE, tab)
            s = _dot(k_ref[pl.ds(start, ATT_BLK), :], q_t) + tab_ref[tab]
            if selected:
                madd = _dot(expand_ref[pl.ds(start, ATT_BLK), :], sel_aug)
                s = s + jnp.concatenate([madd] * hpg, axis=1)
            m_new = jnp.maximum(m, jnp.max(s, axis=0, keepdims=True))
            pr = jnp.exp(s - m_new)
            alpha = jnp.exp(m - m_new)
            l = alpha * l + jnp.sum(pr, axis=0, keepdims=True)
            acc = alpha * acc + _dot(v_t_ref[:, pl.ds(start, ATT_BLK)], pr.astype(BF16))
            return m_new, l, acc

        init = (jnp.full((1, lanes), NEG, F32), jnp.zeros((1, lanes), F32), jnp.zeros((HEAD_DIM, lanes), F32))
        _, l, acc = lax.fori_loop(lo, qi + 1, step, init)
        return acc / l

    o_sel = sweep(ksel_ref, vsel_t_ref, 0, True)
    o_win = sweep(kwin_ref, vwin_t_ref, jnp.maximum(qi - WINDOW // ATT_BLK, 0), False)

    for c in range(0, tq, LANES):
        gates_t_ref[:, c:c + LANES] = jax.nn.sigmoid(gates_ref[c:c + LANES, :]).T
    for h in range(hpg):
        head = g * hpg + h
        sl = slice(h * tq, (h + 1) * tq)
        y = (gates_t_ref[pl.ds(head, 1), :] * o_cmp[:, sl]
             + gates_t_ref[pl.ds(N_ATT_HEADS + head, 1), :] * o_sel[:, sl]
             + gates_t_ref[pl.ds(2 * N_ATT_HEADS + head, 1), :] * o_win[:, sl])
        for c in range(0, tq, LANES):
            o_ref[c:c + LANES, h * HEAD_DIM:(h + 1) * HEAD_DIM] = y[:, c:c + LANES].T.astype(o_ref.dtype)


def _nsa(proj, batch, seq, cmp_pos_k, w_ck1, w_ck2, cmp_pos_v, w_cv1, w_cv2, rel_table):
    n_qt = seq // ATT_BLK
    hpg = HEADS_PER_GROUP
    n_cmp = seq // CMP_STRIDE
    n_sb = seq // SEL_BLOCK
    n_aug = LANES
    assert CMP_BLOCK == 2 * CMP_STRIDE and n_cmp % 8 == 0 and n_sb % 16 == 0 and n_sb + 16 <= n_aug
    assert (ATT_BLK // CMP_STRIDE) % 8 == 0 and n_qt * (ATT_BLK // CMP_STRIDE) <= n_cmp

    sweep_tab, cmp_tab = _bias_tables(rel_table, n_cmp)
    cs = np.arange(n_cmp)[None, :] * CMP_STRIDE
    ss = np.arange(n_sb)[:, None] * SEL_BLOCK
    overlap = (cs < ss + SEL_BLOCK) & (cs + CMP_BLOCK > ss) & (np.arange(n_cmp)[None, :] < n_cmp - 1)
    big = float(np.asarray(-NEG, dtype=BF16))
    blk_of_key = np.arange(seq)[:, None] // SEL_BLOCK
    col = np.arange(n_aug)[None, :]
    expand = np.where(col == blk_of_key, big, 0.0) + np.where(col == n_sb, -big, 0.0)
    overlap = jnp.asarray(overlap, BF16)
    expand = jnp.asarray(expand, BF16)

    kv_col0 = COL_KV // HEAD_DIM

    def kv_spec(idx):
        return pl.BlockSpec((seq, HEAD_DIM), lambda b, g, qi, idx=idx: (b, kv_col0 + idx * N_KV_GROUPS + g))

    def whole(arr):
        return pl.BlockSpec(arr.shape, lambda b, g, qi, nd=arr.ndim: (0,) * nd)

    in_specs = [
        pl.BlockSpec((ATT_BLK, hpg * HEAD_DIM), lambda b, g, qi: (b * n_qt + qi, g)),
        kv_spec(0), kv_spec(1), kv_spec(2), kv_spec(3), kv_spec(4), kv_spec(5),
        pl.BlockSpec((ATT_BLK, LANES), lambda b, g, qi: (b * n_qt + qi, COL_GATES // LANES)),
        whole(cmp_pos_k), whole(w_ck1), whole(w_ck2), whole(cmp_pos_v), whole(w_cv1), whole(w_cv2),
        pl.BlockSpec((None,) + sweep_tab.shape[1:], lambda b, g, qi: (g, 0, 0, 0)),
        pl.BlockSpec((None,) + cmp_tab.shape[1:], lambda b, g, qi: (g, 0, 0)),
        whole(overlap), whole(expand),
    ]
    return pl.pallas_call(
        _nsa_body,
        grid=(batch, N_KV_GROUPS, n_qt),
        in_specs=in_specs,
        out_specs=pl.BlockSpec((ATT_BLK, hpg * HEAD_DIM), lambda b, g, qi: (b * n_qt + qi, g)),
        out_shape=jax.ShapeDtypeStruct((batch * seq, D_ATT), BF16),
        scratch_shapes=[pltpu.VMEM((n_cmp, HEAD_DIM), BF16), pltpu.VMEM((HEAD_DIM, n_cmp), BF16),
                        pltpu.VMEM((seq, HEAD_DIM), BF16), pltpu.VMEM((HEAD_DIM, seq), BF16),
                        pltpu.VMEM((seq, HEAD_DIM), BF16), pltpu.VMEM((HEAD_DIM, seq), BF16),
                        pltpu.VMEM((LANES, ATT_BLK), F32)],
        compiler_params=pltpu.CompilerParams(
            dimension_semantics=("parallel", "parallel", "arbitrary"), vmem_limit_bytes=VMEM_LIMIT_BYTES),
        name="nsa",
    )(proj, proj, proj, proj, proj, proj, proj, proj,
      cmp_pos_k, w_ck1.astype(BF16), w_ck2.astype(BF16), cmp_pos_v, w_cv1.astype(BF16), w_cv2.astype(BF16),
      sweep_tab, cmp_tab, overlap, expand)


def _pool_body(u_ref, w_ref, scale_ref, o_ref):
    gi = pl.program_id(1)
    u = u_ref[...]
    seq = u.shape[0]
    t = lax.broadcasted_iota(jnp.int32, (seq, 1), 0)

    def shifted(x, k):
        return jnp.where(t >= k, pltpu.roll(x, k, 0), 0.0)

    sums = [u]
    for n in range(int(math.log2(POOL_WINDOWS[-1]))):
        sums.append(sums[-1] + shifted(sums[-1], 2 ** n))
    wsum = sums[int(math.log2(POOL_WINDOWS[0]))]
    for idx in range(1, len(POOL_WINDOWS)):
        wsum = jnp.where(gi >= idx, sums[int(math.log2(POOL_WINDOWS[idx]))], wsum)
    window = POOL_WINDOWS[0]
    for idx in range(1, len(POOL_WINDOWS)):
        window = jnp.where(gi >= idx, POOL_WINDOWS[idx], window)
    cnt = jnp.minimum(t + 1, window).astype(F32)
    d = (wsum / cnt - u).astype(BF16)
    o_ref[...] = (_dot(d, w_ref[0]) * scale_ref[...]).astype(o_ref.dtype)


def _pool(proj, batch, seq, w_pool, pool_scale):
    n_g, dg, _ = w_pool.shape
    assert all(w == 2 ** int(math.log2(w)) for w in POOL_WINDOWS) and list(POOL_WINDOWS) == sorted(POOL_WINDOWS)
    col0 = COL_U // dg
    return pl.pallas_call(
        _pool_body,
        grid=(batch, n_g),
        in_specs=[
            pl.BlockSpec((seq, dg), lambda b, gi: (b, col0 + gi)),
            pl.BlockSpec((1, dg, dg), lambda b, gi: (gi, 0, 0)),
            pl.BlockSpec((1, dg), lambda b, gi: (0, gi)),
        ],
        out_specs=pl.BlockSpec((seq, dg), lambda b, gi: (b, gi)),
        out_shape=jax.ShapeDtypeStruct((batch * seq, n_g * dg), BF16),
        compiler_params=pltpu.CompilerParams(
            dimension_semantics=("parallel", "parallel"), vmem_limit_bytes=VMEM_LIMIT_BYTES),
        name="pool",
    )(proj, w_pool.astype(BF16), pool_scale.reshape(1, -1))


def _outproj_body(x_ref, ya_ref, yp_ref, wa_ref, wp_ref, o_ref):
    o_ref[...] = x_ref[...] + _dot(ya_ref[...], wa_ref[...]) + _dot(yp_ref[...], wp_ref[...])


def _outproj(x, y_att, y_pool, w_out):
    n, d = x.shape
    da = y_att.shape[1]
    dp = y_pool.shape[1]
    w = w_out.astype(BF16)
    return pl.pallas_call(
        _outproj_body,
        grid=(d // OUT_TN, n // OUT_TM),
        in_specs=[
            pl.BlockSpec((OUT_TM, OUT_TN), lambda j, i: (i, j)),
            pl.BlockSpec((OUT_TM, da), lambda j, i: (i, 0)),
            pl.BlockSpec((OUT_TM, dp), lambda j, i: (i, 0)),
            pl.BlockSpec((da, OUT_TN), lambda j, i: (0, j)),
            pl.BlockSpec((dp, OUT_TN), lambda j, i: (da // dp, j)),
        ],
        out_specs=pl.BlockSpec((OUT_TM, OUT_TN), lambda j, i: (i, j)),
        out_shape=jax.ShapeDtypeStruct((n, d), F32),
        compiler_params=pltpu.CompilerParams(
            dimension_semantics=("parallel", "parallel"), vmem_limit_bytes=VMEM_LIMIT_BYTES),
        name="out_proj",
    )(x, y_att, y_pool, w, w)


def kernel(x, norm_ffn1, w_ffn1_gate, w_ffn1_up, w_ffn1_down, norm_mix, w_in, cmp_pos_k, w_cmp_k1, w_cmp_k2,
           cmp_pos_v, w_cmp_v1, w_cmp_v2, w_pool, pool_scale, w_out, rel_table, norm_ffn2, w_ffn2_gate,
           w_ffn2_up, w_ffn2_down, norm_final):
    batch, seq, d = x.shape
    depth = norm_ffn1.shape[0]
    xf = x.reshape(batch * seq, d)
    gf = norm_final.reshape(1, d)
    for l in range(depth):
        xf = _ffn(xf, norm_ffn1[l].reshape(1, d), w_ffn1_gate[l].astype(BF16), w_ffn1_up[l].astype(BF16),
                  w_ffn1_down[l].astype(BF16), gf, final_norm=False)
        proj = _inproj(xf, norm_mix[l].reshape(1, d), w_in[l])
        y_att = _nsa(proj, batch, seq, cmp_pos_k[l], w_cmp_k1[l], w_cmp_k2[l],
                     cmp_pos_v[l], w_cmp_v1[l], w_cmp_v2[l], rel_table)
        y_pool = _pool(proj, batch, seq, w_pool[l], pool_scale[l])
        xf = _outproj(xf, y_att, y_pool, w_out[l])
        xf = _ffn(xf, norm_ffn2[l].reshape(1, d), w_ffn2_gate[l].astype(BF16), w_ffn2_up[l].astype(BF16),
                  w_ffn2_down[l].astype(BF16), gf, final_norm=(l == depth - 1))
    if depth == 0:
        raise ValueError("depth must be positive")
    return xf.reshape(batch, seq, d)
```

```python
import functools
import math

import jax
import jax.numpy as jnp
import numpy as np
from jax import lax
from jax.experimental import pallas as pl
from jax.experimental.pallas import tpu as pltpu

HEAD_DIM = 128
N_ATT_HEADS = 16
N_KV_GROUPS = 4
HEADS_PER_GROUP = N_ATT_HEADS // N_KV_GROUPS
D_ATT = N_ATT_HEADS * HEAD_DIM
D_KV = N_KV_GROUPS * HEAD_DIM
N_POOL_GROUPS = 4
POOL_WINDOWS = (2, 4, 8, 16)
CMP_BLOCK = 32
CMP_STRIDE = 16
SEL_BLOCK = 64
SEL_TOP = 8
WINDOW = 512
REL_BUCKETS = 32
REL_MAX_DIST = 128
EPS = 1e-6
NEG = -1e30
FORCE_BONUS = 1e4
FUTURE_SCORE = -1e9

LANES = 128
MXU_DIM = 256
VMEM_LIMIT_BYTES = 56 * 2 ** 20

FFN_TM = 512
FFN_TF = MXU_DIM
FFN_TN = 1024
NORM_ROWS = 128
PROJ_TM = 512
PROJ_TN = 512
PROJ_TG = MXU_DIM
ATT_BLK = 256
OUT_TM = 512
OUT_TN = 1024

BF16 = jnp.bfloat16
F32 = jnp.float32


def _dot(a, b):
    return jnp.dot(a, b, preferred_element_type=F32)


def _rms(x, g):
    return x * lax.rsqrt(jnp.mean(x * x, axis=-1, keepdims=True) + EPS) * g


def _ffn_body(x_ref, g_ref, wg_ref, wu_ref, wd_ref, gf_ref, o_ref, h_ref, *, final_norm):
    j = pl.program_id(1)

    @pl.when(j == 0)
    def _():
        for r in range(0, x_ref.shape[0], NORM_ROWS):
            x = x_ref[r:r + NORM_ROWS, :]
            h_ref[r:r + NORM_ROWS, :] = _rms(x, g_ref[...]).astype(BF16)
            o_ref[r:r + NORM_ROWS, :] = x

    h = h_ref[...]
    gate = _dot(h, wg_ref[...])
    up = _dot(h, wu_ref[...])
    act = (0.5 * (gate * jax.nn.sigmoid(gate)) * up).astype(BF16)
    for c in range(0, o_ref.shape[1], FFN_TN):
        o_ref[:, c:c + FFN_TN] += _dot(act, wd_ref[:, c:c + FFN_TN])

    if final_norm:
        @pl.when(j == pl.num_programs(1) - 1)
        def _():
            for r in range(0, o_ref.shape[0], NORM_ROWS):
                o_ref[r:r + NORM_ROWS, :] = _rms(o_ref[r:r + NORM_ROWS, :], gf_ref[...])


def _ffn(x, g, w_gate, w_up, w_down, gf, *, final_norm):
    n, d = x.shape
    d_ff = w_gate.shape[1]
    return pl.pallas_call(
        functools.partial(_ffn_body, final_norm=final_norm),
        grid=(n // FFN_TM, d_ff // FFN_TF),
        in_specs=[
            pl.BlockSpec((FFN_TM, d), lambda i, j: (i, 0)),
            pl.BlockSpec((1, d), lambda i, j: (0, 0)),
            pl.BlockSpec((d, FFN_TF), lambda i, j: (0, j)),
            pl.BlockSpec((d, FFN_TF), lambda i, j: (0, j)),
            pl.BlockSpec((FFN_TF, d), lambda i, j: (j, 0)),
            pl.BlockSpec((1, d), lambda i, j: (0, 0)),
        ],
        out_specs=pl.BlockSpec((FFN_TM, d), lambda i, j: (i, 0)),
        out_shape=jax.ShapeDtypeStruct((n, d), F32),
        scratch_shapes=[pltpu.VMEM((FFN_TM, d), BF16)],
        compiler_params=pltpu.CompilerParams(
            dimension_semantics=("parallel", "arbitrary"), vmem_limit_bytes=VMEM_LIMIT_BYTES),
        name="ffn_final" if final_norm else "ffn",
    )(x, g, w_gate, w_up, w_down, gf)


COL_Q = 0
COL_KV = D_ATT
COL_U = D_ATT + 6 * D_KV
D_POOL_COLS = 2048
COL_GATES = COL_U + D_POOL_COLS
N_GATES = 3 * N_ATT_HEADS


def _inproj_body(x_ref, g_ref, wa_ref, wu_ref, wg_ref, o_ref, og_ref, h_ref, *, n_a):
    j = pl.program_id(1)

    @pl.when(j == 0)
    def _():
        for r in range(0, x_ref.shape[0], NORM_ROWS):
            h_ref[r:r + NORM_ROWS, :] = _rms(x_ref[r:r + NORM_ROWS, :], g_ref[...]).astype(BF16)

    @pl.when(j < n_a)
    def _():
        o_ref[...] = _dot(h_ref[...], wa_ref[...])

    @pl.when(j >= n_a)
    def _():
        o_ref[...] = _dot(h_ref[...], wu_ref[...])

    @pl.when(j == pl.num_programs(1) - 1)
    def _():
        og_ref[...] = _dot(h_ref[...], wg_ref[...])


def _inproj(x, g, w_in):
    n, d = x.shape
    d_in = w_in.shape[1]
    assert d_in - COL_U - N_GATES == D_POOL_COLS and N_GATES <= PROJ_TG
    assert COL_U % PROJ_TN == 0 and D_POOL_COLS % PROJ_TN == 0
    wa = w_in[:, :COL_U].astype(BF16)
    wu = w_in[:, COL_U + N_GATES:].astype(BF16)
    wg = jnp.pad(w_in[:, COL_U:COL_U + N_GATES], ((0, 0), (0, PROJ_TG - N_GATES))).astype(BF16)
    n_a = COL_U // PROJ_TN
    n_u = D_POOL_COLS // PROJ_TN
    return pl.pallas_call(
        functools.partial(_inproj_body, n_a=n_a),
        grid=(n // PROJ_TM, n_a + n_u),
        in_specs=[
            pl.BlockSpec((PROJ_TM, d), lambda i, j: (i, 0)),
            pl.BlockSpec((1, d), lambda i, j: (0, 0)),
            pl.BlockSpec((d, PROJ_TN), lambda i, j: (0, jnp.minimum(j, n_a - 1))),
            pl.BlockSpec((d, PROJ_TN), lambda i, j: (0, jnp.maximum(j - n_a, 0))),
            pl.BlockSpec((d, PROJ_TG), lambda i, j: (0, 0)),
        ],
        out_specs=[pl.BlockSpec((PROJ_TM, PROJ_TN), lambda i, j: (i, j)),
                   pl.BlockSpec((PROJ_TM, PROJ_TG), lambda i, j: (i, 0))],
        out_shape=[jax.ShapeDtypeStruct((n, COL_GATES), F32), jax.ShapeDtypeStruct((n, PROJ_TG), F32)],
        scratch_shapes=[pltpu.VMEM((PROJ_TM, d), BF16)],
        compiler_params=pltpu.CompilerParams(
            dimension_semantics=("parallel", "arbitrary"), vmem_limit_bytes=VMEM_LIMIT_BYTES),
        name="in_proj",
    )(x, g, wa, wu, wg)


def _rel_bucket_np(n):
    max_exact = REL_BUCKETS // 2
    n = np.maximum(n, 0)
    nf = np.maximum(n, 1).astype(np.float32)
    large = max_exact + (np.log(nf / max_exact) / math.log(REL_MAX_DIST / max_exact)
                         * (REL_BUCKETS - max_exact)).astype(np.int32)
    large = np.minimum(large, REL_BUCKETS - 1)
    return np.where(n < max_exact, n, large)


FAR_DIST = int(np.max(np.nonzero(_rel_bucket_np(np.arange(4 * REL_MAX_DIST)) < REL_BUCKETS - 1)[0])) + 1
CMP_ROW_OFF = -(-(FAR_DIST + CMP_BLOCK - 1) // CMP_STRIDE) - 1
TAB_DIAG, TAB_PREV, TAB_FAR, TAB_EDGE = 0, 1, 2, 3


def _pick(rel_table, idx):
    onehot = jnp.asarray(np.arange(REL_BUCKETS)[:, None] == idx[None, :], F32)
    return jnp.einsum("kh,kn->hn", rel_table.astype(F32), onehot, precision=lax.Precision.HIGHEST)


def _group_lanes(t):
    t = t.reshape((N_KV_GROUPS, HEADS_PER_GROUP) + t.shape[1:])
    t = jnp.moveaxis(t, 1, -2)
    return t.reshape(t.shape[:-2] + (HEADS_PER_GROUP * t.shape[-1],))


def _bias_tables(rel_table, n_cmp):
    blk = ATT_BLK
    n_heads = rel_table.shape[1]
    far = REL_BUCKETS - 1
    assert blk >= FAR_DIST and WINDOW % blk == 0
    d = np.arange(2 * blk) - (blk - 1)
    idx = np.stack([_rel_bucket_np(d), _rel_bucket_np(blk + d), np.full_like(d, far), np.full_like(d, far)])
    mask = np.stack([np.where(d < 0, NEG, 0.0), np.zeros_like(d, np.float64), np.zeros_like(d, np.float64),
                     np.where(d >= 0, NEG, 0.0)]).astype(np.float32)
    v = _pick(rel_table, idx.reshape(-1)).reshape(n_heads, 4, 2 * blk) + mask[None]
    t = jnp.tile(v, (1, 1, blk))[..., :blk * (2 * blk - 1)].reshape(n_heads, 4, blk, 2 * blk - 1)[..., blk - 1:]
    sweep = _group_lanes(t)

    rows = np.arange(n_cmp - CMP_ROW_OFF, n_cmp + (blk - CMP_BLOCK) // CMP_STRIDE + 1)
    dist = np.arange(blk)[None, :] - ((rows[:, None] - n_cmp) * CMP_STRIDE + CMP_BLOCK - 1)
    band = _pick(rel_table, _rel_bucket_np(dist).reshape(-1)).reshape(n_heads, len(rows), blk)
    band = band + np.where(dist < 0, NEG, 0.0).astype(np.float32)[None]
    before = jnp.broadcast_to(rel_table[far].astype(F32)[:, None, None], (n_heads, int(rows[0]), blk))
    after = jnp.full((n_heads, 2 * n_cmp - int(rows[-1]) - 1, blk), NEG, F32)
    cmp_tab = _group_lanes(jnp.concatenate([before, band, after], axis=1))
    return sweep, cmp_tab


def _compress(kv_ref, pos_ref, w1_ref, w2_ref):
    seq, dk = kv_ref.shape
    n_half = seq // CMP_STRIDE
    first = jnp.zeros((n_half, w1_ref.shape[1]), F32)
    second = jnp.zeros((n_half, w1_ref.shape[1]), F32)
    for l in range(CMP_STRIDE):
        rows = kv_ref[pl.ds(l, n_half, stride=CMP_STRIDE), :]
        a = (rows + pos_ref[l:l + 1, :]).astype(BF16)
        b = (rows + pos_ref[CMP_STRIDE + l:CMP_STRIDE + l + 1, :]).astype(BF16)
        first += _dot(a, w1_ref[l * dk:(l + 1) * dk, :])
        second += _dot(b, w1_ref[(CMP_STRIDE + l) * dk:(CMP_STRIDE + l + 1) * dk, :])
    pre = first + pltpu.roll(second, n_half - 1, 0)
    hid = (pre * jax.nn.sigmoid(pre)).astype(BF16)
    return _dot(hid, w2_ref[...])


def _nsa_body(q_ref, kc_ref, vc_ref, ks_ref, vs_ref, kw_ref, vw_ref, gates_ref,
              posk_ref, wk1_ref, wk2_ref, posv_ref, wv1_ref, wv2_ref,
              tab_ref, tabc_ref, overlap_ref, expand_ref,
              o_ref, kcmp_ref, vcmp_t_ref, ksel_ref, vsel_t_ref, kwin_ref, vwin_t_ref, gates_t_ref, s_ref):
    g = pl.program_id(1)
    qi = pl.program_id(2)
    tq = ATT_BLK
    hpg = HEADS_PER_GROUP
    lanes = hpg * tq
    n_cmp = kcmp_ref.shape[0]
    n_sb = overlap_ref.shape[0]
    seq = ks_ref.shape[0]
    assert tq & (tq - 1) == 0 and SEL_BLOCK & (SEL_BLOCK - 1) == 0

    @pl.when(qi == 0)
    def _():
        kcmp_ref[...] = _compress(kc_ref, posk_ref, wk1_ref, wk2_ref).astype(BF16)
        vcmp_t_ref[...] = _compress(vc_ref, posv_ref, wv1_ref, wv2_ref).astype(BF16).T
        for c in range(0, seq, LANES):
            ksel_ref[c:c + LANES, 0:HEAD_DIM] = ks_ref[c:c + LANES, :].astype(BF16)
            ksel_ref[c:c + LANES, HEAD_DIM:] = expand_ref[c:c + LANES, :]
            kwin_ref[c:c + LANES, :] = kw_ref[c:c + LANES, :].astype(BF16)
            vsel_t_ref[:, c:c + LANES] = vs_ref[c:c + LANES, :].T.astype(BF16)
            vwin_t_ref[:, c:c + LANES] = vw_ref[c:c + LANES, :].T.astype(BF16)

    q = q_ref[...] * (HEAD_DIM ** -0.5)
    q_t = jnp.concatenate([q[:, h * HEAD_DIM:(h + 1) * HEAD_DIM].T for h in range(hpg)],
                          axis=1).astype(BF16)
    t_lane = qi * tq + (lax.broadcasted_iota(jnp.int32, (1, lanes), 1) & (tq - 1))

    first_row = pl.multiple_of(n_cmp - qi * (tq // CMP_STRIDE), 8)
    logit = _dot(kcmp_ref[...], q_t) + tabc_ref[pl.ds(first_row, n_cmp), :]
    e = jnp.exp(logit - jnp.max(logit, axis=0, keepdims=True))
    any_valid = (t_lane >= CMP_BLOCK - 1).astype(F32)
    p = e / jnp.sum(e, axis=0, keepdims=True) * any_valid
    o_cmp = _dot(vcmp_t_ref[...], p.astype(BF16))

    psum = p[:, 0:tq]
    for h in range(1, hpg):
        psum = psum + p[:, h * tq:(h + 1) * tq]
    p_hi = psum.astype(BF16)
    r1 = psum - p_hi.astype(F32)
    p_mid = r1.astype(BF16)
    p_lo = (r1 - p_mid.astype(F32)).astype(BF16)
    ov = overlap_ref[...]
    imp = _dot(ov, p_hi) + _dot(ov, p_mid) + _dot(ov, p_lo)
    cur = lax.shift_right_logical(t_lane[:, 0:tq], int(math.log2(SEL_BLOCK)))
    jb = lax.broadcasted_iota(jnp.int32, (n_sb, tq), 0)
    forced = (jb == 0) | (jb == cur) | (jb == cur - 1)
    score = jnp.where(jb > cur, FUTURE_SCORE, imp + jnp.where(forced, FORCE_BONUS, 0.0))
    rank = jnp.zeros((n_sb, tq), jnp.int32)
    for j2 in range(n_sb):
        row = score[j2:j2 + 1, :]
        rank += ((row > score) | ((row == score) & (j2 < jb))).astype(jnp.int32)
    sel_neg = jnp.where(rank < SEL_TOP, 0.0, NEG).astype(BF16)
    n_aug = ksel_ref.shape[1] - HEAD_DIM
    q_sel = jnp.concatenate([q_t, jnp.concatenate([sel_neg] * hpg, axis=1),
                             jnp.zeros((n_aug - n_sb, lanes), BF16)], axis=0)

    def sweep(k_ref, v_t_ref, q_rhs, lo, edge):
        def rows(kb):
            return pl.ds(pl.multiple_of(kb * ATT_BLK, ATT_BLK), ATT_BLK)

        def scores(kb, m):
            dblk = qi - kb
            tab = jnp.minimum(dblk, TAB_FAR)
            if edge is not None:
                tab = jnp.where(dblk == edge, TAB_EDGE, tab)
            s = _dot(k_ref[rows(kb), :], q_rhs) + tab_ref[tab]
            s_ref[rows(kb), :] = s
            return jnp.maximum(m, jnp.max(s, axis=0, keepdims=True))

        def over_chunks(body, init):
            n = qi + 1 - lo
            out = lax.fori_loop(0, n // 2, lambda i, c: body(lo + 2 * i + 1, body(lo + 2 * i, c)), init)
            return lax.cond(n % 2 == 1, lambda c: body(qi, c), lambda c: c, out)

        m = over_chunks(scores, jnp.full((1, lanes), NEG, F32))

        def accumulate(kb, carry):
            l, acc = carry
            pr = jnp.exp(s_ref[rows(kb), :] - m)
            return (l + jnp.sum(pr, axis=0, keepdims=True),
                    acc + _dot(v_t_ref[:, rows(kb)], pr.astype(BF16)))

        l, acc = over_chunks(accumulate, (jnp.zeros((1, lanes), F32), jnp.zeros((HEAD_DIM, lanes), F32)))
        return acc / l

    o_sel = sweep(ksel_ref, vsel_t_ref, q_sel, 0, None)
    n_back = WINDOW // ATT_BLK
    o_win = sweep(kwin_ref, vwin_t_ref, q_t, jnp.maximum(qi - n_back, 0), n_back)

    for c in range(0, tq, LANES):
        gates_t_ref[:, c:c + LANES] = jax.nn.sigmoid(gates_ref[c:c + LANES, :]).T
    for h in range(hpg):
        head = g * hpg + h
        sl = slice(h * tq, (h + 1) * tq)
        y = (gates_t_ref[pl.ds(head, 1), :] * o_cmp[:, sl]
             + gates_t_ref[pl.ds(N_ATT_HEADS + head, 1), :] * o_sel[:, sl]
             + gates_t_ref[pl.ds(2 * N_ATT_HEADS + head, 1), :] * o_win[:, sl])
        for c in range(0, tq, LANES):
            o_ref[c:c + LANES, h * HEAD_DIM:(h + 1) * HEAD_DIM] = y[:, c:c + LANES].T.astype(o_ref.dtype)


def _nsa(proj, gates, batch, seq, cmp_pos_k, w_ck1, w_ck2, cmp_pos_v, w_cv1, w_cv2, rel_table):
    n_qt = seq // ATT_BLK
    hpg = HEADS_PER_GROUP
    n_cmp = seq // CMP_STRIDE
    n_sb = seq // SEL_BLOCK
    n_aug = LANES
    assert CMP_BLOCK == 2 * CMP_STRIDE and n_cmp % 8 == 0 and n_sb % 16 == 0 and n_sb <= n_aug
    assert (ATT_BLK // CMP_STRIDE) % 8 == 0 and n_qt * (ATT_BLK // CMP_STRIDE) <= n_cmp

    sweep_tab, cmp_tab = _bias_tables(rel_table, n_cmp)
    cs = np.arange(n_cmp)[None, :] * CMP_STRIDE
    ss = np.arange(n_sb)[:, None] * SEL_BLOCK
    overlap = (cs < ss + SEL_BLOCK) & (cs + CMP_BLOCK > ss) & (np.arange(n_cmp)[None, :] < n_cmp - 1)
    expand = np.arange(seq)[:, None] // SEL_BLOCK == np.arange(n_aug)[None, :]
    overlap = jnp.asarray(overlap, BF16)
    expand = jnp.asarray(expand, BF16)

    kv_col0 = COL_KV // HEAD_DIM

    def kv_spec(idx):
        return pl.BlockSpec((seq, HEAD_DIM), lambda b, g, qi, idx=idx: (b, kv_col0 + idx * N_KV_GROUPS + g))

    def whole(arr):
        return pl.BlockSpec(arr.shape, lambda b, g, qi, nd=arr.ndim: (0,) * nd)

    in_specs = [
        pl.BlockSpec((ATT_BLK, hpg * HEAD_DIM), lambda b, g, qi: (b * n_qt + qi, g)),
        kv_spec(0), kv_spec(1), kv_spec(2), kv_spec(3), kv_spec(4), kv_spec(5),
        pl.BlockSpec((ATT_BLK, LANES), lambda b, g, qi: (b * n_qt + qi, 0)),
        whole(cmp_pos_k), whole(w_ck1), whole(w_ck2), whole(cmp_pos_v), whole(w_cv1), whole(w_cv2),
        pl.BlockSpec((None,) + sweep_tab.shape[1:], lambda b, g, qi: (g, 0, 0, 0)),
        pl.BlockSpec((None,) + cmp_tab.shape[1:], lambda b, g, qi: (g, 0, 0)),
        whole(overlap), whole(expand),
    ]
    return pl.pallas_call(
        _nsa_body,
        grid=(batch, N_KV_GROUPS, n_qt),
        in_specs=in_specs,
        out_specs=pl.BlockSpec((ATT_BLK, hpg * HEAD_DIM), lambda b, g, qi: (b * n_qt + qi, g)),
        out_shape=jax.ShapeDtypeStruct((batch * seq, D_ATT), BF16),
        scratch_shapes=[pltpu.VMEM((n_cmp, HEAD_DIM), BF16), pltpu.VMEM((HEAD_DIM, n_cmp), BF16),
                        pltpu.VMEM((seq, HEAD_DIM + n_aug), BF16), pltpu.VMEM((HEAD_DIM, seq), BF16),
                        pltpu.VMEM((seq, HEAD_DIM), BF16), pltpu.VMEM((HEAD_DIM, seq), BF16),
                        pltpu.VMEM((LANES, ATT_BLK), F32), pltpu.VMEM((seq, hpg * ATT_BLK), F32)],
        compiler_params=pltpu.CompilerParams(
            dimension_semantics=("parallel", "parallel", "arbitrary"), vmem_limit_bytes=VMEM_LIMIT_BYTES),
        name="nsa",
    )(proj, proj, proj, proj, proj, proj, proj, gates,
      cmp_pos_k, w_ck1.astype(BF16), w_ck2.astype(BF16), cmp_pos_v, w_cv1.astype(BF16), w_cv2.astype(BF16),
      sweep_tab, cmp_tab, overlap, expand)


def _pool_body(u_ref, w_ref, scale_ref, o_ref):
    gi = pl.program_id(1)
    u = u_ref[...]
    seq = u.shape[0]
    t = lax.broadcasted_iota(jnp.int32, (seq, 1), 0)

    def shifted(x, k):
        return jnp.where(t >= k, pltpu.roll(x, k, 0), 0.0)

    sums = [u]
    for n in range(int(math.log2(POOL_WINDOWS[-1]))):
        sums.append(sums[-1] + shifted(sums[-1], 2 ** n))
    wsum = sums[int(math.log2(POOL_WINDOWS[0]))]
    for idx in range(1, len(POOL_WINDOWS)):
        wsum = jnp.where(gi >= idx, sums[int(math.log2(POOL_WINDOWS[idx]))], wsum)
    window = POOL_WINDOWS[0]
    for idx in range(1, len(POOL_WINDOWS)):
        window = jnp.where(gi >= idx, POOL_WINDOWS[idx], window)
    cnt = jnp.minimum(t + 1, window).astype(F32)
    d = (wsum / cnt - u).astype(BF16)
    o_ref[...] = (_dot(d, w_ref[0]) * scale_ref[...]).astype(o_ref.dtype)


def _pool(proj, batch, seq, w_pool, pool_scale):
    n_g, dg, _ = w_pool.shape
    assert all(w == 2 ** int(math.log2(w)) for w in POOL_WINDOWS) and list(POOL_WINDOWS) == sorted(POOL_WINDOWS)
    col0 = COL_U // dg
    return pl.pallas_call(
        _pool_body,
        grid=(batch, n_g),
        in_specs=[
            pl.BlockSpec((seq, dg), lambda b, gi: (b, col0 + gi)),
            pl.BlockSpec((1, dg, dg), lambda b, gi: (gi, 0, 0)),
            pl.BlockSpec((1, dg), lambda b, gi: (0, gi)),
        ],
        out_specs=pl.BlockSpec((seq, dg), lambda b, gi: (b, gi)),
        out_shape=jax.ShapeDtypeStruct((batch * seq, n_g * dg), BF16),
        compiler_params=pltpu.CompilerParams(
            dimension_semantics=("parallel", "parallel"), vmem_limit_bytes=VMEM_LIMIT_BYTES),
        name="pool",
    )(proj, w_pool.astype(BF16), pool_scale.reshape(1, -1))


def _outproj_body(x_ref, ya_ref, yp_ref, wa_ref, wp_ref, o_ref):
    o_ref[...] = x_ref[...] + _dot(ya_ref[...], wa_ref[...]) + _dot(yp_ref[...], wp_ref[...])


def _outproj(x, y_att, y_pool, w_out):
    n, d = x.shape
    da = y_att.shape[1]
    dp = y_pool.shape[1]
    w = w_out.astype(BF16)
    return pl.pallas_call(
        _outproj_body,
        grid=(d // OUT_TN, n // OUT_TM),
        in_specs=[
            pl.BlockSpec((OUT_TM, OUT_TN), lambda j, i: (i, j)),
            pl.BlockSpec((OUT_TM, da), lambda j, i: (i, 0)),
            pl.BlockSpec((OUT_TM, dp), lambda j, i: (i, 0)),
            pl.BlockSpec((da, OUT_TN), lambda j, i: (0, j)),
            pl.BlockSpec((dp, OUT_TN), lambda j, i: (da // dp, j)),
        ],
        out_specs=pl.BlockSpec((OUT_TM, OUT_TN), lambda j, i: (i, j)),
        out_shape=jax.ShapeDtypeStruct((n, d), F32),
        compiler_params=pltpu.CompilerParams(
            dimension_semantics=("parallel", "parallel"), vmem_limit_bytes=VMEM_LIMIT_BYTES),
        name="out_proj",
    )(x, y_att, y_pool, w, w)


def kernel(x, norm_ffn1, w_ffn1_gate, w_ffn1_up, w_ffn1_down, norm_mix, w_in, cmp_pos_k, w_cmp_k1, w_cmp_k2,
           cmp_pos_v, w_cmp_v1, w_cmp_v2, w_pool, pool_scale, w_out, rel_table, norm_ffn2, w_ffn2_gate,
           w_ffn2_up, w_ffn2_down, norm_final):
    batch, seq, d = x.shape
    depth = norm_ffn1.shape[0]
    xf = x.reshape(batch * seq, d)
    gf = norm_final.reshape(1, d)
    for l in range(depth):
        xf = _ffn(xf, norm_ffn1[l].reshape(1, d), w_ffn1_gate[l].astype(BF16), w_ffn1_up[l].astype(BF16),
                  w_ffn1_down[l].astype(BF16), gf, final_norm=False)
        proj, gates = _inproj(xf, norm_mix[l].reshape(1, d), w_in[l])
        y_att = _nsa(proj, gates, batch, seq, cmp_pos_k[l], w_cmp_k1[l], w_cmp_k2[l],
                     cmp_pos_v[l], w_cmp_v1[l], w_cmp_v2[l], rel_table)
        y_pool = _pool(proj, batch, seq, w_pool[l], pool_scale[l])
        xf = _outproj(xf, y_att, y_pool, w_out[l])
        xf = _ffn(xf, norm_ffn2[l].reshape(1, d), w_ffn2_gate[l].astype(BF16), w_ffn2_up[l].astype(BF16),
                  w_ffn2_down[l].astype(BF16), gf, final_norm=(l == depth - 1))
    if depth == 0:
        raise ValueError("depth must be positive")
    return xf.reshape(batch, seq, d)
```

```python
import functools
import math

import jax
import jax.numpy as jnp
import numpy as np
from jax import lax
from jax.experimental import pallas as pl
from jax.experimental.pallas import tpu as pltpu

HEAD_DIM = 128
N_ATT_HEADS = 16
N_KV_GROUPS = 4
HEADS_PER_GROUP = N_ATT_HEADS // N_KV_GROUPS
D_ATT = N_ATT_HEADS * HEAD_DIM
D_KV = N_KV_GROUPS * HEAD_DIM
N_POOL_GROUPS = 4
POOL_WINDOWS = (2, 4, 8, 16)
CMP_BLOCK = 32
CMP_STRIDE = 16
SEL_BLOCK = 64
SEL_TOP = 8
WINDOW = 512
REL_BUCKETS = 32
REL_MAX_DIST = 128
EPS = 1e-6
NEG = -1e30
FORCE_BONUS = 1e4
FUTURE_SCORE = -1e9

LANES = 128
MXU_DIM = 256
VMEM_LIMIT_BYTES = 56 * 2 ** 20

FFN_TM = 512
FFN_TF = MXU_DIM
FFN_TN = 1024
NORM_ROWS = 128
CAST_ROWS = 16
PROJ_TM = 512
PROJ_TN = 512
PROJ_TG = MXU_DIM
ATT_BLK = 256
OUT_TM = 512
OUT_TN = 1024

BF16 = jnp.bfloat16
F32 = jnp.float32


def _dot(a, b):
    return jnp.dot(a, b, preferred_element_type=F32)


def _rms(x, g):
    return x * lax.rsqrt(jnp.mean(x * x, axis=-1, keepdims=True) + EPS) * g


def _ffn_body(*refs, final_norm, n_cast):
    x_ref, g_ref, wg_ref, wu_ref, wd_ref, gf_ref = refs[:6]
    cast_in = refs[6:6 + n_cast]
    o_ref = refs[6 + n_cast]
    cast_out = refs[7 + n_cast:7 + 2 * n_cast]
    h_ref = refs[7 + 2 * n_cast]
    j = pl.program_id(1)

    @pl.when(j == 0)
    def _():
        for r in range(0, x_ref.shape[0], NORM_ROWS):
            x = x_ref[r:r + NORM_ROWS, :]
            h_ref[r:r + NORM_ROWS, :] = _rms(x, g_ref[...]).astype(BF16)
            o_ref[r:r + NORM_ROWS, :] = x

    h = h_ref[...]
    gate = _dot(h, wg_ref[...])
    up = _dot(h, wu_ref[...])
    act = (0.5 * (gate * jax.nn.sigmoid(gate)) * up).astype(BF16)
    for c in range(0, o_ref.shape[1], FFN_TN):
        o_ref[:, c:c + FFN_TN] += _dot(act, wd_ref[:, c:c + FFN_TN])

    if final_norm:
        @pl.when(j == pl.num_programs(1) - 1)
        def _():
            for r in range(0, o_ref.shape[0], NORM_ROWS):
                o_ref[r:r + NORM_ROWS, :] = _rms(o_ref[r:r + NORM_ROWS, :], gf_ref[...])

    for src, dst in zip(cast_in, cast_out):
        dst[...] = src[...].astype(BF16)


def _cast_spec(a, n_i, n_j):
    r, c = a.shape
    if r % n_i == 0 and c % n_j == 0 and (r // n_i) % CAST_ROWS == 0 and (c // n_j) % LANES == 0:
        return pl.BlockSpec((r // n_i, c // n_j), lambda i, j: (i, j))
    if r % n_j == 0 and c % n_i == 0 and (r // n_j) % CAST_ROWS == 0 and (c // n_i) % LANES == 0:
        return pl.BlockSpec((r // n_j, c // n_i), lambda i, j: (j, i))
    per_i, rem = divmod(r, n_i * CAST_ROWS)
    assert rem == 0 and per_i <= n_j
    return pl.BlockSpec((CAST_ROWS, c), lambda i, j: (i * per_i + jnp.minimum(j, per_i - 1), 0))


def _ffn(x, g, w_gate, w_up, w_down, gf, *, final_norm, to_cast=()):
    n, d = x.shape
    d_ff = w_gate.shape[1]
    n_i, n_j = n // FFN_TM, d_ff // FFN_TF
    cast_specs = [_cast_spec(a, n_i, n_j) for a in to_cast]
    outs = pl.pallas_call(
        functools.partial(_ffn_body, final_norm=final_norm, n_cast=len(to_cast)),
        grid=(n_i, n_j),
        in_specs=[
            pl.BlockSpec((FFN_TM, d), lambda i, j: (i, 0)),
            pl.BlockSpec((1, d), lambda i, j: (0, 0)),
            pl.BlockSpec((d, FFN_TF), lambda i, j: (0, j)),
            pl.BlockSpec((d, FFN_TF), lambda i, j: (0, j)),
            pl.BlockSpec((FFN_TF, d), lambda i, j: (j, 0)),
            pl.BlockSpec((1, d), lambda i, j: (0, 0)),
        ] + cast_specs,
        out_specs=[pl.BlockSpec((FFN_TM, d), lambda i, j: (i, 0))] + cast_specs,
        out_shape=[jax.ShapeDtypeStruct((n, d), F32)] + [jax.ShapeDtypeStruct(a.shape, BF16) for a in to_cast],
        scratch_shapes=[pltpu.VMEM((FFN_TM, d), BF16)],
        compiler_params=pltpu.CompilerParams(
            dimension_semantics=("parallel", "arbitrary"), vmem_limit_bytes=VMEM_LIMIT_BYTES),
        name="ffn_final" if final_norm else "ffn",
    )(x, g, w_gate, w_up, w_down, gf, *to_cast)
    return outs[0], outs[1:]


COL_Q = 0
COL_KV = D_ATT
COL_U = D_ATT + 6 * D_KV
D_POOL_COLS = 2048
COL_GATES = COL_U + D_POOL_COLS
N_GATES = 3 * N_ATT_HEADS


def _inproj_body(x_ref, g_ref, wa_ref, wu_ref, wg_ref, o_ref, og_ref, h_ref, *, n_a):
    j = pl.program_id(1)

    @pl.when(j == 0)
    def _():
        for r in range(0, x_ref.shape[0], NORM_ROWS):
            h_ref[r:r + NORM_ROWS, :] = _rms(x_ref[r:r + NORM_ROWS, :], g_ref[...]).astype(BF16)

    @pl.when(j < n_a)
    def _():
        o_ref[...] = _dot(h_ref[...], wa_ref[...])

    @pl.when(j >= n_a)
    def _():
        o_ref[...] = _dot(h_ref[...], wu_ref[...])

    @pl.when(j == pl.num_programs(1) - 1)
    def _():
        og_ref[...] = _dot(h_ref[...], wg_ref[...])


def _inproj(x, g, w_in):
    n, d = x.shape
    d_in = w_in.shape[1]
    assert d_in - COL_U - N_GATES == D_POOL_COLS and N_GATES <= PROJ_TG
    assert COL_U % PROJ_TN == 0 and D_POOL_COLS % PROJ_TN == 0
    wu = w_in[:, COL_U + N_GATES:]
    wg = jnp.pad(w_in[:, COL_U:COL_U + N_GATES], ((0, 0), (0, PROJ_TG - N_GATES)))
    n_a = COL_U // PROJ_TN
    n_u = D_POOL_COLS // PROJ_TN
    return pl.pallas_call(
        functools.partial(_inproj_body, n_a=n_a),
        grid=(n // PROJ_TM, n_a + n_u),
        in_specs=[
            pl.BlockSpec((PROJ_TM, d), lambda i, j: (i, 0)),
            pl.BlockSpec((1, d), lambda i, j: (0, 0)),
            pl.BlockSpec((d, PROJ_TN), lambda i, j: (0, jnp.minimum(j, n_a - 1))),
            pl.BlockSpec((d, PROJ_TN), lambda i, j: (0, jnp.maximum(j - n_a, 0))),
            pl.BlockSpec((d, PROJ_TG), lambda i, j: (0, 0)),
        ],
        out_specs=[pl.BlockSpec((PROJ_TM, PROJ_TN), lambda i, j: (i, j)),
                   pl.BlockSpec((PROJ_TM, PROJ_TG), lambda i, j: (i, 0))],
        out_shape=[jax.ShapeDtypeStruct((n, COL_GATES), F32), jax.ShapeDtypeStruct((n, PROJ_TG), F32)],
        scratch_shapes=[pltpu.VMEM((PROJ_TM, d), BF16)],
        compiler_params=pltpu.CompilerParams(
            dimension_semantics=("parallel", "arbitrary"), vmem_limit_bytes=VMEM_LIMIT_BYTES),
        name="in_proj",
    )(x, g, w_in, wu, wg)


def _rel_bucket_np(n):
    max_exact = REL_BUCKETS // 2
    n = np.maximum(n, 0)
    nf = np.maximum(n, 1).astype(np.float32)
    large = max_exact + (np.log(nf / max_exact) / math.log(REL_MAX_DIST / max_exact)
                         * (REL_BUCKETS - max_exact)).astype(np.int32)
    large = np.minimum(large, REL_BUCKETS - 1)
    return np.where(n < max_exact, n, large)


FAR_DIST = int(np.max(np.nonzero(_rel_bucket_np(np.arange(4 * REL_MAX_DIST)) < REL_BUCKETS - 1)[0])) + 1
CMP_ROW_OFF = -(-(FAR_DIST + CMP_BLOCK - 1) // CMP_STRIDE) - 1
TAB_DIAG, TAB_PREV, TAB_FAR, TAB_EDGE = 0, 1, 2, 3


def _pick(rel_table, idx):
    onehot = jnp.asarray(np.arange(REL_BUCKETS)[:, None] == idx[None, :], F32)
    return jnp.einsum("kh,kn->hn", rel_table.astype(F32), onehot, precision=lax.Precision.HIGHEST)


def _group_lanes(t):
    t = t.reshape((N_KV_GROUPS, HEADS_PER_GROUP) + t.shape[1:])
    t = jnp.moveaxis(t, 1, -2)
    return t.reshape(t.shape[:-2] + (HEADS_PER_GROUP * t.shape[-1],))


def _toeplitz_body(p_ref, o_ref):
    blk = o_ref.shape[-1]
    row = lax.broadcasted_iota(jnp.int32, (blk, 2 * blk), 0)
    for t in range(o_ref.shape[0]):
        y = jnp.broadcast_to(p_ref[t:t + 1, :], (blk, 2 * blk))
        for b in range(int(math.log2(blk))):
            y = jnp.where(((row >> b) & 1) == 1, pltpu.roll(y, 2 ** b, 1), y)
        o_ref[t] = y[:, :blk]


def _bias_tables(rel_table, n_cmp):
    blk = ATT_BLK
    n_heads = rel_table.shape[1]
    far = REL_BUCKETS - 1
    assert blk >= FAR_DIST and WINDOW % blk == 0
    m = np.arange(2 * blk)
    d = np.where(m < blk, m, m - 2 * blk)
    idx = np.stack([_rel_bucket_np(d), _rel_bucket_np(blk + d), np.full_like(d, far), np.full_like(d, far)])
    mask = np.stack([np.where(d < 0, NEG, 0.0), np.zeros_like(d, np.float64), np.zeros_like(d, np.float64),
                     np.where(d >= 0, NEG, 0.0)]).astype(np.float32)
    profiles = _pick(rel_table, idx.reshape(-1)).reshape(n_heads, 4, 2 * blk) + mask[None]
    sweep = pl.pallas_call(
        _toeplitz_body,
        grid=(N_KV_GROUPS, HEADS_PER_GROUP),
        in_specs=[pl.BlockSpec((None, 4, 2 * blk), lambda g, h: (g * HEADS_PER_GROUP + h, 0, 0))],
        out_specs=pl.BlockSpec((None, 4, blk, blk), lambda g, h: (g, 0, 0, h)),
        out_shape=jax.ShapeDtypeStruct((N_KV_GROUPS, 4, blk, HEADS_PER_GROUP * blk), F32),
        name="bias_tables",
    )(profiles)

    rows = np.arange(n_cmp - CMP_ROW_OFF, n_cmp + (blk - CMP_BLOCK) // CMP_STRIDE + 1)
    dist = np.arange(blk)[None, :] - ((rows[:, None] - n_cmp) * CMP_STRIDE + CMP_BLOCK - 1)
    band = _pick(rel_table, _rel_bucket_np(dist).reshape(-1)).reshape(n_heads, len(rows), blk)
    band = band + np.where(dist < 0, NEG, 0.0).astype(np.float32)[None]
    before = jnp.broadcast_to(rel_table[far].astype(F32)[:, None, None], (n_heads, int(rows[0]), blk))
    after = jnp.full((n_heads, 2 * n_cmp - int(rows[-1]) - 1, blk), NEG, F32)
    cmp_tab = _group_lanes(jnp.concatenate([before, band, after], axis=1))
    return sweep, cmp_tab


def _compress(kv_ref, pos_ref, w1_ref, w2_ref):
    seq, dk = kv_ref.shape
    n_half = seq // CMP_STRIDE
    first = jnp.zeros((n_half, w1_ref.shape[1]), F32)
    second = jnp.zeros((n_half, w1_ref.shape[1]), F32)
    for l in range(CMP_STRIDE):
        rows = kv_ref[pl.ds(l, n_half, stride=CMP_STRIDE), :]
        a = (rows + pos_ref[l:l + 1, :]).astype(BF16)
        b = (rows + pos_ref[CMP_STRIDE + l:CMP_STRIDE + l + 1, :]).astype(BF16)
        first += _dot(a, w1_ref[l * dk:(l + 1) * dk, :])
        second += _dot(b, w1_ref[(CMP_STRIDE + l) * dk:(CMP_STRIDE + l + 1) * dk, :])
    pre = first + pltpu.roll(second, n_half - 1, 0)
    hid = (pre * jax.nn.sigmoid(pre)).astype(BF16)
    return _dot(hid, w2_ref[...])


def _nsa_body(q_ref, kc_ref, vc_ref, ks_ref, vs_ref, kw_ref, vw_ref, gates_ref,
              posk_ref, wk1_ref, wk2_ref, posv_ref, wv1_ref, wv2_ref,
              tab_ref, tabc_ref, overlap_ref, expand_ref,
              o_ref, kcmp_ref, vcmp_t_ref, ksel_ref, vsel_t_ref, kwin_ref, vwin_t_ref, gates_t_ref, s_ref):
    g = pl.program_id(1)
    qi = pl.program_id(2)
    tq = ATT_BLK
    hpg = HEADS_PER_GROUP
    lanes = hpg * tq
    n_cmp = kcmp_ref.shape[0]
    n_sb = overlap_ref.shape[0]
    seq = ks_ref.shape[0]
    assert tq & (tq - 1) == 0 and SEL_BLOCK & (SEL_BLOCK - 1) == 0

    @pl.when(qi == 0)
    def _():
        kcmp_ref[...] = _compress(kc_ref, posk_ref, wk1_ref, wk2_ref).astype(BF16)
        vcmp_t_ref[...] = _compress(vc_ref, posv_ref, wv1_ref, wv2_ref).astype(BF16).T
        for c in range(0, seq, LANES):
            ksel_ref[c:c + LANES, 0:HEAD_DIM] = ks_ref[c:c + LANES, :].astype(BF16)
            ksel_ref[c:c + LANES, HEAD_DIM:] = expand_ref[c:c + LANES, :]
            kwin_ref[c:c + LANES, :] = kw_ref[c:c + LANES, :].astype(BF16)
            vsel_t_ref[:, c:c + LANES] = vs_ref[c:c + LANES, :].T.astype(BF16)
            vwin_t_ref[:, c:c + LANES] = vw_ref[c:c + LANES, :].T.astype(BF16)

    q = q_ref[...] * (HEAD_DIM ** -0.5)
    q_t = jnp.concatenate([q[:, h * HEAD_DIM:(h + 1) * HEAD_DIM].T for h in range(hpg)],
                          axis=1).astype(BF16)
    t_lane = qi * tq + (lax.broadcasted_iota(jnp.int32, (1, lanes), 1) & (tq - 1))

    first_row = pl.multiple_of(n_cmp - qi * (tq // CMP_STRIDE), 8)
    logit = _dot(kcmp_ref[...], q_t) + tabc_ref[pl.ds(first_row, n_cmp), :]
    e = jnp.exp(logit - jnp.max(logit, axis=0, keepdims=True))
    any_valid = (t_lane >= CMP_BLOCK - 1).astype(F32)
    p = e / jnp.sum(e, axis=0, keepdims=True) * any_valid
    o_cmp = _dot(vcmp_t_ref[...], p.astype(BF16))

    psum = p[:, 0:tq]
    for h in range(1, hpg):
        psum = psum + p[:, h * tq:(h + 1) * tq]
    p_hi = psum.astype(BF16)
    r1 = psum - p_hi.astype(F32)
    p_mid = r1.astype(BF16)
    p_lo = (r1 - p_mid.astype(F32)).astype(BF16)
    ov = overlap_ref[...]
    imp = _dot(ov, p_hi) + _dot(ov, p_mid) + _dot(ov, p_lo)
    cur = lax.shift_right_logical(t_lane[:, 0:tq], int(math.log2(SEL_BLOCK)))
    jb = lax.broadcasted_iota(jnp.int32, (n_sb, tq), 0)
    forced = (jb == 0) | (jb == cur) | (jb == cur - 1)
    score = jnp.where(jb > cur, FUTURE_SCORE, imp + jnp.where(forced, FORCE_BONUS, 0.0))
    rank = jnp.zeros((n_sb, tq), jnp.int32)
    for j2 in range(n_sb):
        row = score[j2:j2 + 1, :]
        rank += ((row > score) | ((row == score) & (j2 < jb))).astype(jnp.int32)
    sel_neg = jnp.where(rank < SEL_TOP, 0.0, NEG).astype(BF16)
    n_aug = ksel_ref.shape[1] - HEAD_DIM
    q_sel = jnp.concatenate([q_t, jnp.concatenate([sel_neg] * hpg, axis=1),
                             jnp.zeros((n_aug - n_sb, lanes), BF16)], axis=0)

    def sweep(k_ref, v_t_ref, q_rhs, lo, edge):
        def rows(kb):
            return pl.ds(pl.multiple_of(kb * ATT_BLK, ATT_BLK), ATT_BLK)

        def scores(kb, m):
            dblk = qi - kb
            tab = jnp.minimum(dblk, TAB_FAR)
            if edge is not None:
                tab = jnp.where(dblk == edge, TAB_EDGE, tab)
            s = _dot(k_ref[rows(kb), :], q_rhs) + tab_ref[tab]
            s_ref[rows(kb), :] = s
            return jnp.maximum(m, jnp.max(s, axis=0, keepdims=True))

        def over_chunks(body, init):
            n = qi + 1 - lo
            out = lax.fori_loop(0, n // 2, lambda i, c: body(lo + 2 * i + 1, body(lo + 2 * i, c)), init)
            return lax.cond(n % 2 == 1, lambda c: body(qi, c), lambda c: c, out)

        m = over_chunks(scores, jnp.full((1, lanes), NEG, F32))

        def accumulate(kb, carry):
            l, acc = carry
            pr = jnp.exp(s_ref[rows(kb), :] - m)
            return (l + jnp.sum(pr, axis=0, keepdims=True),
                    acc + _dot(v_t_ref[:, rows(kb)], pr.astype(BF16)))

        l, acc = over_chunks(accumulate, (jnp.zeros((1, lanes), F32), jnp.zeros((HEAD_DIM, lanes), F32)))
        return acc / l

    o_sel = sweep(ksel_ref, vsel_t_ref, q_sel, 0, None)
    n_back = WINDOW // ATT_BLK
    o_win = sweep(kwin_ref, vwin_t_ref, q_t, jnp.maximum(qi - n_back, 0), n_back)

    for c in range(0, tq, LANES):
        gates_t_ref[:, c:c + LANES] = jax.nn.sigmoid(gates_ref[c:c + LANES, :]).T
    for h in range(hpg):
        head = g * hpg + h
        sl = slice(h * tq, (h + 1) * tq)
        y = (gates_t_ref[pl.ds(head, 1), :] * o_cmp[:, sl]
             + gates_t_ref[pl.ds(N_ATT_HEADS + head, 1), :] * o_sel[:, sl]
             + gates_t_ref[pl.ds(2 * N_ATT_HEADS + head, 1), :] * o_win[:, sl])
        for c in range(0, tq, LANES):
            o_ref[c:c + LANES, h * HEAD_DIM:(h + 1) * HEAD_DIM] = y[:, c:c + LANES].T.astype(o_ref.dtype)


def _nsa(proj, gates, batch, seq, cmp_pos_k, w_ck1, w_ck2, cmp_pos_v, w_cv1, w_cv2, rel_table):
    n_qt = seq // ATT_BLK
    hpg = HEADS_PER_GROUP
    n_cmp = seq // CMP_STRIDE
    n_sb = seq // SEL_BLOCK
    n_aug = LANES
    assert CMP_BLOCK == 2 * CMP_STRIDE and n_cmp % 8 == 0 and n_sb % 16 == 0 and n_sb <= n_aug
    assert (ATT_BLK // CMP_STRIDE) % 8 == 0 and n_qt * (ATT_BLK // CMP_STRIDE) <= n_cmp

    sweep_tab, cmp_tab = _bias_tables(rel_table, n_cmp)
    cs = np.arange(n_cmp)[None, :] * CMP_STRIDE
    ss = np.arange(n_sb)[:, None] * SEL_BLOCK
    overlap = (cs < ss + SEL_BLOCK) & (cs + CMP_BLOCK > ss) & (np.arange(n_cmp)[None, :] < n_cmp - 1)
    expand = np.arange(seq)[:, None] // SEL_BLOCK == np.arange(n_aug)[None, :]
    overlap = jnp.asarray(overlap, BF16)
    expand = jnp.asarray(expand, BF16)

    kv_col0 = COL_KV // HEAD_DIM

    def kv_spec(idx):
        return pl.BlockSpec((seq, HEAD_DIM), lambda b, g, qi, idx=idx: (b, kv_col0 + idx * N_KV_GROUPS + g))

    def whole(arr):
        return pl.BlockSpec(arr.shape, lambda b, g, qi, nd=arr.ndim: (0,) * nd)

    in_specs = [
        pl.BlockSpec((ATT_BLK, hpg * HEAD_DIM), lambda b, g, qi: (b * n_qt + qi, g)),
        kv_spec(0), kv_spec(1), kv_spec(2), kv_spec(3), kv_spec(4), kv_spec(5),
        pl.BlockSpec((ATT_BLK, LANES), lambda b, g, qi: (b * n_qt + qi, 0)),
        whole(cmp_pos_k), whole(w_ck1), whole(w_ck2), whole(cmp_pos_v), whole(w_cv1), whole(w_cv2),
        pl.BlockSpec((None,) + sweep_tab.shape[1:], lambda b, g, qi: (g, 0, 0, 0)),
        pl.BlockSpec((None,) + cmp_tab.shape[1:], lambda b, g, qi: (g, 0, 0)),
        whole(overlap), whole(expand),
    ]
    return pl.pallas_call(
        _nsa_body,
        grid=(batch, N_KV_GROUPS, n_qt),
        in_specs=in_specs,
        out_specs=pl.BlockSpec((ATT_BLK, hpg * HEAD_DIM), lambda b, g, qi: (b * n_qt + qi, g)),
        out_shape=jax.ShapeDtypeStruct((batch * seq, D_ATT), BF16),
        scratch_shapes=[pltpu.VMEM((n_cmp, HEAD_DIM), BF16), pltpu.VMEM((HEAD_DIM, n_cmp), BF16),
                        pltpu.VMEM((seq, HEAD_DIM + n_aug), BF16), pltpu.VMEM((HEAD_DIM, seq), BF16),
                        pltpu.VMEM((seq, HEAD_DIM), BF16), pltpu.VMEM((HEAD_DIM, seq), BF16),
                        pltpu.VMEM((LANES, ATT_BLK), F32), pltpu.VMEM((seq, hpg * ATT_BLK), F32)],
        compiler_params=pltpu.CompilerParams(
            dimension_semantics=("parallel", "parallel", "arbitrary"), vmem_limit_bytes=VMEM_LIMIT_BYTES),
        name="nsa",
    )(proj, proj, proj, proj, proj, proj, proj, gates,
      cmp_pos_k, w_ck1.astype(BF16), w_ck2.astype(BF16), cmp_pos_v, w_cv1.astype(BF16), w_cv2.astype(BF16),
      sweep_tab, cmp_tab, overlap, expand)


def _pool_body(u_ref, w_ref, scale_ref, o_ref):
    gi = pl.program_id(1)
    u = u_ref[...]
    seq = u.shape[0]
    t = lax.broadcasted_iota(jnp.int32, (seq, 1), 0)

    def shifted(x, k):
        return jnp.where(t >= k, pltpu.roll(x, k, 0), 0.0)

    sums = [u]
    for n in range(int(math.log2(POOL_WINDOWS[-1]))):
        sums.append(sums[-1] + shifted(sums[-1], 2 ** n))
    wsum = sums[int(math.log2(POOL_WINDOWS[0]))]
    for idx in range(1, len(POOL_WINDOWS)):
        wsum = jnp.where(gi >= idx, sums[int(math.log2(POOL_WINDOWS[idx]))], wsum)
    window = POOL_WINDOWS[0]
    for idx in range(1, len(POOL_WINDOWS)):
        window = jnp.where(gi >= idx, POOL_WINDOWS[idx], window)
    cnt = jnp.minimum(t + 1, window).astype(F32)
    d = (wsum / cnt - u).astype(BF16)
    o_ref[...] = (_dot(d, w_ref[0]) * scale_ref[...]).astype(o_ref.dtype)


def _pool(proj, batch, seq, w_pool, pool_scale):
    n_g, dg, _ = w_pool.shape
    assert all(w == 2 ** int(math.log2(w)) for w in POOL_WINDOWS) and list(POOL_WINDOWS) == sorted(POOL_WINDOWS)
    col0 = COL_U // dg
    return pl.pallas_call(
        _pool_body,
        grid=(batch, n_g),
        in_specs=[
            pl.BlockSpec((seq, dg), lambda b, gi: (b, col0 + gi)),
            pl.BlockSpec((1, dg, dg), lambda b, gi: (gi, 0, 0)),
            pl.BlockSpec((1, dg), lambda b, gi: (0, gi)),
        ],
        out_specs=pl.BlockSpec((seq, dg), lambda b, gi: (b, gi)),
        out_shape=jax.ShapeDtypeStruct((batch * seq, n_g * dg), BF16),
        compiler_params=pltpu.CompilerParams(
            dimension_semantics=("parallel", "parallel"), vmem_limit_bytes=VMEM_LIMIT_BYTES),
        name="pool",
    )(proj, w_pool.astype(BF16), pool_scale.reshape(1, -1))


def _outproj_body(x_ref, ya_ref, yp_ref, wa_ref, wp_ref, o_ref):
    o_ref[...] = x_ref[...] + _dot(ya_ref[...], wa_ref[...]) + _dot(yp_ref[...], wp_ref[...])


def _outproj(x, y_att, y_pool, w_out):
    n, d = x.shape
    da = y_att.shape[1]
    dp = y_pool.shape[1]
    w = w_out
    return pl.pallas_call(
        _outproj_body,
        grid=(d // OUT_TN, n // OUT_TM),
        in_specs=[
            pl.BlockSpec((OUT_TM, OUT_TN), lambda j, i: (i, j)),
            pl.BlockSpec((OUT_TM, da), lambda j, i: (i, 0)),
            pl.BlockSpec((OUT_TM, dp), lambda j, i: (i, 0)),
            pl.BlockSpec((da, OUT_TN), lambda j, i: (0, j)),
            pl.BlockSpec((dp, OUT_TN), lambda j, i: (da // dp, j)),
        ],
        out_specs=pl.BlockSpec((OUT_TM, OUT_TN), lambda j, i: (i, j)),
        out_shape=jax.ShapeDtypeStruct((n, d), F32),
        compiler_params=pltpu.CompilerParams(
            dimension_semantics=("parallel", "parallel"), vmem_limit_bytes=VMEM_LIMIT_BYTES),
        name="out_proj",
    )(x, y_att, y_pool, w, w)


def kernel(x, norm_ffn1, w_ffn1_gate, w_ffn1_up, w_ffn1_down, norm_mix, w_in, cmp_pos_k, w_cmp_k1, w_cmp_k2,
           cmp_pos_v, w_cmp_v1, w_cmp_v2, w_pool, pool_scale, w_out, rel_table, norm_ffn2, w_ffn2_gate,
           w_ffn2_up, w_ffn2_down, norm_final):
    batch, seq, d = x.shape
    depth = norm_ffn1.shape[0]
    xf = x.reshape(batch * seq, d)
    gf = norm_final.reshape(1, d)
    for l in range(depth):
        later = (w_ffn2_gate[l], w_ffn2_up[l], w_ffn2_down[l], w_in[l], w_out[l])
        xf, (w2_gate, w2_up, w2_down, w_in_b, w_out_b) = _ffn(
            xf, norm_ffn1[l].reshape(1, d), w_ffn1_gate[l].astype(BF16), w_ffn1_up[l].astype(BF16),
            w_ffn1_down[l].astype(BF16), gf, final_norm=False, to_cast=later)
        proj, gates = _inproj(xf, norm_mix[l].reshape(1, d), w_in_b)
        y_att = _nsa(proj, gates, batch, seq, cmp_pos_k[l], w_cmp_k1[l], w_cmp_k2[l],
                     cmp_pos_v[l], w_cmp_v1[l], w_cmp_v2[l], rel_table)
        y_pool = _pool(proj, batch, seq, w_pool[l], pool_scale[l])
        xf = _outproj(xf, y_att, y_pool, w_out_b)
        xf, _ = _ffn(xf, norm_ffn2[l].reshape(1, d), w2_gate, w2_up, w2_down, gf, final_norm=(l == depth - 1))
    if depth == 0:
        raise ValueError("depth must be positive")
    return xf.reshape(batch, seq, d)
```

```python
import functools
import math

import jax
import jax.numpy as jnp
import numpy as np
from jax import lax
from jax.experimental import pallas as pl
from jax.experimental.pallas import tpu as pltpu

HEAD_DIM = 128
N_ATT_HEADS = 16
N_KV_GROUPS = 4
HEADS_PER_GROUP = N_ATT_HEADS // N_KV_GROUPS
D_ATT = N_ATT_HEADS * HEAD_DIM
D_KV = N_KV_GROUPS * HEAD_DIM
N_POOL_GROUPS = 4
POOL_WINDOWS = (2, 4, 8, 16)
CMP_BLOCK = 32
CMP_STRIDE = 16
SEL_BLOCK = 64
SEL_TOP = 8
WINDOW = 512
REL_BUCKETS = 32
REL_MAX_DIST = 128
EPS = 1e-6
NEG = -1e30
FORCE_BONUS = 1e4
FUTURE_SCORE = -1e9

LANES = 128
MXU_DIM = 256
VMEM_LIMIT_BYTES = 56 * 2 ** 20

FFN_TM = 512
FFN_TF = MXU_DIM
FFN_TN = 1024
NORM_ROWS = 128
CAST_ROWS = 16
PROJ_TM = 512
PROJ_TN = 512
PROJ_TG = MXU_DIM
ATT_BLK = 256
OUT_TM = 512
OUT_TN = 1024

BF16 = jnp.bfloat16
F32 = jnp.float32


def _dot(a, b):
    return jnp.dot(a, b, preferred_element_type=F32)


def _dot_nt(a, b):
    return lax.dot_general(a, b, (((1,), (1,)), ((), ())), preferred_element_type=F32)


def _rms(x, g):
    return x * lax.rsqrt(jnp.mean(x * x, axis=-1, keepdims=True) + EPS) * g


def _ffn_body(*refs, final_norm, n_cast):
    x_ref, g_ref, wg_ref, wu_ref, wd_ref, gf_ref = refs[:6]
    cast_in = refs[6:6 + n_cast]
    o_ref = refs[6 + n_cast]
    cast_out = refs[7 + n_cast:7 + 2 * n_cast]
    h_ref = refs[7 + 2 * n_cast]
    j = pl.program_id(1)

    @pl.when(j == 0)
    def _():
        for r in range(0, x_ref.shape[0], NORM_ROWS):
            x = x_ref[r:r + NORM_ROWS, :]
            h_ref[r:r + NORM_ROWS, :] = _rms(x, g_ref[...]).astype(BF16)
            o_ref[r:r + NORM_ROWS, :] = x

    h = h_ref[...]
    gate = _dot(h, wg_ref[...])
    up = _dot(h, wu_ref[...])
    act = (0.5 * (gate * jax.nn.sigmoid(gate)) * up).astype(BF16)
    for c in range(0, o_ref.shape[1], FFN_TN):
        o_ref[:, c:c + FFN_TN] += _dot(act, wd_ref[:, c:c + FFN_TN])

    if final_norm:
        @pl.when(j == pl.num_programs(1) - 1)
        def _():
            for r in range(0, o_ref.shape[0], NORM_ROWS):
                o_ref[r:r + NORM_ROWS, :] = _rms(o_ref[r:r + NORM_ROWS, :], gf_ref[...])

    for src, dst in zip(cast_in, cast_out):
        dst[...] = src[...].astype(BF16)


def _cast_spec(a, n_i, n_j):
    r, c = a.shape
    if r % n_i == 0 and c % n_j == 0 and (r // n_i) % CAST_ROWS == 0 and (c // n_j) % LANES == 0:
        return pl.BlockSpec((r // n_i, c // n_j), lambda i, j: (i, j))
    if r % n_j == 0 and c % n_i == 0 and (r // n_j) % CAST_ROWS == 0 and (c // n_i) % LANES == 0:
        return pl.BlockSpec((r // n_j, c // n_i), lambda i, j: (j, i))
    n_blocks, rem = divmod(r, CAST_ROWS)
    assert rem == 0 and n_blocks <= n_i * n_j
    return pl.BlockSpec((CAST_ROWS, c), lambda i, j: (jnp.minimum(i * n_j + j, n_blocks - 1), 0))


def _ffn(x, g, w_gate, w_up, w_down, gf, *, final_norm, to_cast=()):
    n, d = x.shape
    d_ff = w_gate.shape[1]
    n_i, n_j = n // FFN_TM, d_ff // FFN_TF
    cast_specs = [_cast_spec(a, n_i, n_j) for a in to_cast]
    outs = pl.pallas_call(
        functools.partial(_ffn_body, final_norm=final_norm, n_cast=len(to_cast)),
        grid=(n_i, n_j),
        in_specs=[
            pl.BlockSpec((FFN_TM, d), lambda i, j: (i, 0)),
            pl.BlockSpec((1, d), lambda i, j: (0, 0)),
            pl.BlockSpec((d, FFN_TF), lambda i, j: (0, j)),
            pl.BlockSpec((d, FFN_TF), lambda i, j: (0, j)),
            pl.BlockSpec((FFN_TF, d), lambda i, j: (j, 0)),
            pl.BlockSpec((1, d), lambda i, j: (0, 0)),
        ] + cast_specs,
        out_specs=[pl.BlockSpec((FFN_TM, d), lambda i, j: (i, 0))] + cast_specs,
        out_shape=[jax.ShapeDtypeStruct((n, d), F32)] + [jax.ShapeDtypeStruct(a.shape, BF16) for a in to_cast],
        scratch_shapes=[pltpu.VMEM((FFN_TM, d), BF16)],
        compiler_params=pltpu.CompilerParams(
            dimension_semantics=("parallel", "arbitrary"), vmem_limit_bytes=VMEM_LIMIT_BYTES),
        name="ffn_final" if final_norm else "ffn",
    )(x, g, w_gate, w_up, w_down, gf, *to_cast)
    return outs[0], outs[1:]


COL_Q = 0
COL_KV = D_ATT
COL_U = D_ATT + 6 * D_KV
D_POOL_COLS = 2048
COL_GATES = COL_U + D_POOL_COLS
N_GATES = 3 * N_ATT_HEADS


def _inproj_body(x_ref, g_ref, wa_ref, wu_ref, wg_ref, o_ref, og_ref, h_ref, *, n_a):
    j = pl.program_id(1)

    @pl.when(j == 0)
    def _():
        for r in range(0, x_ref.shape[0], NORM_ROWS):
            h_ref[r:r + NORM_ROWS, :] = _rms(x_ref[r:r + NORM_ROWS, :], g_ref[...]).astype(BF16)

    @pl.when(j < n_a)
    def _():
        o_ref[...] = _dot_nt(h_ref[...], wa_ref[...])

    @pl.when(j >= n_a)
    def _():
        o_ref[...] = _dot_nt(h_ref[...], wu_ref[...])

    @pl.when(j == pl.num_programs(1) - 1)
    def _():
        og_ref[...] = _dot_nt(h_ref[...], wg_ref[...])


def _inproj(x, g, w_in_t):
    n, d = x.shape
    d_in = w_in_t.shape[0]
    assert d_in - COL_U - N_GATES == D_POOL_COLS and N_GATES <= PROJ_TG
    assert COL_U % PROJ_TN == 0 and D_POOL_COLS % PROJ_TN == 0
    wu = w_in_t[COL_U + N_GATES:]
    wg = jnp.pad(w_in_t[COL_U:COL_U + N_GATES], ((0, PROJ_TG - N_GATES), (0, 0)))
    n_a = COL_U // PROJ_TN
    n_u = D_POOL_COLS // PROJ_TN
    return pl.pallas_call(
        functools.partial(_inproj_body, n_a=n_a),
        grid=(n // PROJ_TM, n_a + n_u),
        in_specs=[
            pl.BlockSpec((PROJ_TM, d), lambda i, j: (i, 0)),
            pl.BlockSpec((1, d), lambda i, j: (0, 0)),
            pl.BlockSpec((PROJ_TN, d), lambda i, j: (jnp.minimum(j, n_a - 1), 0)),
            pl.BlockSpec((PROJ_TN, d), lambda i, j: (jnp.maximum(j - n_a, 0), 0)),
            pl.BlockSpec((PROJ_TG, d), lambda i, j: (0, 0)),
        ],
        out_specs=[pl.BlockSpec((PROJ_TM, PROJ_TN), lambda i, j: (i, j)),
                   pl.BlockSpec((PROJ_TM, PROJ_TG), lambda i, j: (i, 0))],
        out_shape=[jax.ShapeDtypeStruct((n, COL_GATES), F32), jax.ShapeDtypeStruct((n, PROJ_TG), F32)],
        scratch_shapes=[pltpu.VMEM((PROJ_TM, d), BF16)],
        compiler_params=pltpu.CompilerParams(
            dimension_semantics=("parallel", "arbitrary"), vmem_limit_bytes=VMEM_LIMIT_BYTES),
        name="in_proj",
    )(x, g, w_in_t, wu, wg)


def _rel_bucket_np(n):
    max_exact = REL_BUCKETS // 2
    n = np.maximum(n, 0)
    nf = np.maximum(n, 1).astype(np.float32)
    large = max_exact + (np.log(nf / max_exact) / math.log(REL_MAX_DIST / max_exact)
                         * (REL_BUCKETS - max_exact)).astype(np.int32)
    large = np.minimum(large, REL_BUCKETS - 1)
    return np.where(n < max_exact, n, large)


FAR_DIST = int(np.max(np.nonzero(_rel_bucket_np(np.arange(4 * REL_MAX_DIST)) < REL_BUCKETS - 1)[0])) + 1
CMP_ROW_OFF = -(-(FAR_DIST + CMP_BLOCK - 1) // CMP_STRIDE) - 1
TAB_DIAG, TAB_PREV, TAB_FAR, TAB_EDGE = 0, 1, 2, 3


def _pick(rel_table, idx):
    onehot = jnp.asarray(np.arange(REL_BUCKETS)[:, None] == idx[None, :], F32)
    return jnp.einsum("kh,kn->hn", rel_table.astype(F32), onehot, precision=lax.Precision.HIGHEST)


def _group_lanes(t):
    t = t.reshape((N_KV_GROUPS, HEADS_PER_GROUP) + t.shape[1:])
    t = jnp.moveaxis(t, 1, -2)
    return t.reshape(t.shape[:-2] + (HEADS_PER_GROUP * t.shape[-1],))


def _toeplitz_body(p_ref, o_ref):
    blk = o_ref.shape[-1]
    for t in range(o_ref.shape[0]):
        y = pltpu.roll(jnp.broadcast_to(p_ref[t:t + 1, :], (blk, 2 * blk)), 0, 1, stride=1, stride_axis=0)
        o_ref[t] = y[:, :blk]


def _bias_tables(rel_table, n_cmp):
    blk = ATT_BLK
    n_heads = rel_table.shape[1]
    far = REL_BUCKETS - 1
    assert blk >= FAR_DIST and WINDOW % blk == 0
    m = np.arange(2 * blk)
    d = np.where(m < blk, m, m - 2 * blk)
    idx = np.stack([_rel_bucket_np(d), _rel_bucket_np(blk + d), np.full_like(d, far), np.full_like(d, far)])
    mask = np.stack([np.where(d < 0, NEG, 0.0), np.zeros_like(d, np.float64), np.zeros_like(d, np.float64),
                     np.where(d >= 0, NEG, 0.0)]).astype(np.float32)
    profiles = _pick(rel_table, idx.reshape(-1)).reshape(n_heads, 4, 2 * blk) + mask[None]
    sweep = pl.pallas_call(
        _toeplitz_body,
        grid=(N_KV_GROUPS, HEADS_PER_GROUP),
        in_specs=[pl.BlockSpec((None, 4, 2 * blk), lambda g, h: (g * HEADS_PER_GROUP + h, 0, 0))],
        out_specs=pl.BlockSpec((None, 4, blk, blk), lambda g, h: (g, 0, 0, h)),
        out_shape=jax.ShapeDtypeStruct((N_KV_GROUPS, 4, blk, HEADS_PER_GROUP * blk), F32),
        name="bias_tables",
    )(profiles)

    rows = np.arange(n_cmp - CMP_ROW_OFF, n_cmp + (blk - CMP_BLOCK) // CMP_STRIDE + 1)
    dist = np.arange(blk)[None, :] - ((rows[:, None] - n_cmp) * CMP_STRIDE + CMP_BLOCK - 1)
    band = _pick(rel_table, _rel_bucket_np(dist).reshape(-1)).reshape(n_heads, len(rows), blk)
    band = band + np.where(dist < 0, NEG, 0.0).astype(np.float32)[None]
    before = jnp.broadcast_to(rel_table[far].astype(F32)[:, None, None], (n_heads, int(rows[0]), blk))
    after = jnp.full((n_heads, 2 * n_cmp - int(rows[-1]) - 1, blk), NEG, F32)
    cmp_tab = _group_lanes(jnp.concatenate([before, band, after], axis=1))
    return sweep, cmp_tab


def _compress(kv_ref, pos_ref, w1_ref, w2_ref):
    seq, dk = kv_ref.shape
    n_half = seq // CMP_STRIDE
    first = jnp.zeros((n_half, w1_ref.shape[1]), F32)
    second = jnp.zeros((n_half, w1_ref.shape[1]), F32)
    for l in range(CMP_STRIDE):
        rows = kv_ref[pl.ds(l, n_half, stride=CMP_STRIDE), :]
        a = (rows + pos_ref[l:l + 1, :]).astype(BF16)
        b = (rows + pos_ref[CMP_STRIDE + l:CMP_STRIDE + l + 1, :]).astype(BF16)
        first += _dot(a, w1_ref[l * dk:(l + 1) * dk, :])
        second += _dot(b, w1_ref[(CMP_STRIDE + l) * dk:(CMP_STRIDE + l + 1) * dk, :])
    pre = first + pltpu.roll(second, n_half - 1, 0)
    hid = (pre * jax.nn.sigmoid(pre)).astype(BF16)
    return _dot(hid, w2_ref[...])


def _nsa_body(q_ref, kc_ref, vc_ref, ks_ref, vs_ref, kw_ref, vw_ref, gates_ref,
              posk_ref, wk1_ref, wk2_ref, posv_ref, wv1_ref, wv2_ref,
              tab_ref, tabc_ref, overlap_ref, expand_ref,
              o_ref, kcmp_ref, vcmp_t_ref, ksel_ref, vsel_t_ref, kwin_ref, vwin_t_ref, gates_t_ref, s_ref):
    g = pl.program_id(1)
    qi = pl.program_id(2)
    tq = ATT_BLK
    hpg = HEADS_PER_GROUP
    lanes = hpg * tq
    n_cmp = kcmp_ref.shape[0]
    n_sb = overlap_ref.shape[0]
    seq = ks_ref.shape[0]
    assert tq & (tq - 1) == 0 and SEL_BLOCK & (SEL_BLOCK - 1) == 0

    @pl.when(qi == 0)
    def _():
        kcmp_ref[...] = _compress(kc_ref, posk_ref, wk1_ref, wk2_ref).astype(BF16)
        vcmp_t_ref[...] = _compress(vc_ref, posv_ref, wv1_ref, wv2_ref).astype(BF16).T
        for c in range(0, seq, LANES):
            ksel_ref[c:c + LANES, 0:HEAD_DIM] = ks_ref[c:c + LANES, :].astype(BF16)
            ksel_ref[c:c + LANES, HEAD_DIM:] = expand_ref[c:c + LANES, :]
            kwin_ref[c:c + LANES, :] = kw_ref[c:c + LANES, :].astype(BF16)
            vsel_t_ref[:, c:c + LANES] = vs_ref[c:c + LANES, :].T.astype(BF16)
            vwin_t_ref[:, c:c + LANES] = vw_ref[c:c + LANES, :].T.astype(BF16)

    q = q_ref[...] * (HEAD_DIM ** -0.5)
    q_t = jnp.concatenate([q[:, h * HEAD_DIM:(h + 1) * HEAD_DIM].T for h in range(hpg)],
                          axis=1).astype(BF16)
    t_lane = qi * tq + (lax.broadcasted_iota(jnp.int32, (1, lanes), 1) & (tq - 1))

    first_row = pl.multiple_of(n_cmp - qi * (tq // CMP_STRIDE), 8)
    logit = _dot(kcmp_ref[...], q_t) + tabc_ref[pl.ds(first_row, n_cmp), :]
    e = jnp.exp(logit - jnp.max(logit, axis=0, keepdims=True))
    any_valid = (t_lane >= CMP_BLOCK - 1).astype(F32)
    p = e / jnp.sum(e, axis=0, keepdims=True) * any_valid
    o_cmp = _dot(vcmp_t_ref[...], p.astype(BF16))

    psum = p[:, 0:tq]
    for h in range(1, hpg):
        psum = psum + p[:, h * tq:(h + 1) * tq]
    p_hi = psum.astype(BF16)
    r1 = psum - p_hi.astype(F32)
    p_mid = r1.astype(BF16)
    p_lo = (r1 - p_mid.astype(F32)).astype(BF16)
    ov = overlap_ref[...]
    imp = _dot(ov, p_hi) + _dot(ov, p_mid) + _dot(ov, p_lo)
    cur = lax.shift_right_logical(t_lane[:, 0:tq], int(math.log2(SEL_BLOCK)))
    jb = lax.broadcasted_iota(jnp.int32, (n_sb, tq), 0)
    forced = (jb == 0) | (jb == cur) | (jb == cur - 1)
    score = jnp.where(jb > cur, FUTURE_SCORE, imp + jnp.where(forced, FORCE_BONUS, 0.0))
    rank = jnp.zeros((n_sb, tq), jnp.int32)
    for j2 in range(n_sb):
        row = score[j2:j2 + 1, :]
        rank += ((row > score) | ((row == score) & (j2 < jb))).astype(jnp.int32)
    sel_neg = jnp.where(rank < SEL_TOP, 0.0, NEG).astype(BF16)
    n_aug = ksel_ref.shape[1] - HEAD_DIM
    q_sel = jnp.concatenate([q_t, jnp.concatenate([sel_neg] * hpg, axis=1),
                             jnp.zeros((n_aug - n_sb, lanes), BF16)], axis=0)

    def sweep(k_ref, v_t_ref, q_rhs, lo, edge):
        def rows(kb):
            return pl.ds(pl.multiple_of(kb * ATT_BLK, ATT_BLK), ATT_BLK)

        def scores(kb, m):
            dblk = qi - kb
            tab = jnp.minimum(dblk, TAB_FAR)
            if edge is not None:
                tab = jnp.where(dblk == edge, TAB_EDGE, tab)
            s = _dot(k_ref[rows(kb), :], q_rhs) + tab_ref[tab]
            s_ref[rows(kb), :] = s
            return jnp.maximum(m, jnp.max(s, axis=0, keepdims=True))

        def over_chunks(body, init):
            n = qi + 1 - lo
            out = lax.fori_loop(0, n // 2, lambda i, c: body(lo + 2 * i + 1, body(lo + 2 * i, c)), init)
            return lax.cond(n % 2 == 1, lambda c: body(qi, c), lambda c: c, out)

        m = over_chunks(scores, jnp.full((1, lanes), NEG, F32))

        def accumulate(kb, carry):
            l, acc = carry
            pr = jnp.exp(s_ref[rows(kb), :] - m)
            return (l + jnp.sum(pr, axis=0, keepdims=True),
                    acc + _dot(v_t_ref[:, rows(kb)], pr.astype(BF16)))

        l, acc = over_chunks(accumulate, (jnp.zeros((1, lanes), F32), jnp.zeros((HEAD_DIM, lanes), F32)))
        return acc / l

    o_sel = sweep(ksel_ref, vsel_t_ref, q_sel, 0, None)
    n_back = WINDOW // ATT_BLK
    o_win = sweep(kwin_ref, vwin_t_ref, q_t, jnp.maximum(qi - n_back, 0), n_back)

    for c in range(0, tq, LANES):
        gates_t_ref[:, c:c + LANES] = jax.nn.sigmoid(gates_ref[c:c + LANES, :]).T
    for h in range(hpg):
        head = g * hpg + h
        sl = slice(h * tq, (h + 1) * tq)
        y = (gates_t_ref[pl.ds(head, 1), :] * o_cmp[:, sl]
             + gates_t_ref[pl.ds(N_ATT_HEADS + head, 1), :] * o_sel[:, sl]
             + gates_t_ref[pl.ds(2 * N_ATT_HEADS + head, 1), :] * o_win[:, sl])
        for c in range(0, tq, LANES):
            o_ref[c:c + LANES, h * HEAD_DIM:(h + 1) * HEAD_DIM] = y[:, c:c + LANES].T.astype(o_ref.dtype)


def _nsa(proj, gates, batch, seq, cmp_pos_k, w_ck1, w_ck2, cmp_pos_v, w_cv1, w_cv2, rel_table):
    n_qt = seq // ATT_BLK
    hpg = HEADS_PER_GROUP
    n_cmp = seq // CMP_STRIDE
    n_sb = seq // SEL_BLOCK
    n_aug = LANES
    assert CMP_BLOCK == 2 * CMP_STRIDE and n_cmp % 8 == 0 and n_sb % 16 == 0 and n_sb <= n_aug
    assert (ATT_BLK // CMP_STRIDE) % 8 == 0 and n_qt * (ATT_BLK // CMP_STRIDE) <= n_cmp

    sweep_tab, cmp_tab = _bias_tables(rel_table, n_cmp)
    cs = np.arange(n_cmp)[None, :] * CMP_STRIDE
    ss = np.arange(n_sb)[:, None] * SEL_BLOCK
    overlap = (cs < ss + SEL_BLOCK) & (cs + CMP_BLOCK > ss) & (np.arange(n_cmp)[None, :] < n_cmp - 1)
    expand = np.arange(seq)[:, None] // SEL_BLOCK == np.arange(n_aug)[None, :]
    overlap = jnp.asarray(overlap, BF16)
    expand = jnp.asarray(expand, BF16)

    kv_col0 = COL_KV // HEAD_DIM

    def kv_spec(idx):
        return pl.BlockSpec((seq, HEAD_DIM), lambda b, g, qi, idx=idx: (b, kv_col0 + idx * N_KV_GROUPS + g))

    def whole(arr):
        return pl.BlockSpec(arr.shape, lambda b, g, qi, nd=arr.ndim: (0,) * nd)

    in_specs = [
        pl.BlockSpec((ATT_BLK, hpg * HEAD_DIM), lambda b, g, qi: (b * n_qt + qi, g)),
        kv_spec(0), kv_spec(1), kv_spec(2), kv_spec(3), kv_spec(4), kv_spec(5),
        pl.BlockSpec((ATT_BLK, LANES), lambda b, g, qi: (b * n_qt + qi, 0)),
        whole(cmp_pos_k), whole(w_ck1), whole(w_ck2), whole(cmp_pos_v), whole(w_cv1), whole(w_cv2),
        pl.BlockSpec((None,) + sweep_tab.shape[1:], lambda b, g, qi: (g, 0, 0, 0)),
        pl.BlockSpec((None,) + cmp_tab.shape[1:], lambda b, g, qi: (g, 0, 0)),
        whole(overlap), whole(expand),
    ]
    return pl.pallas_call(
        _nsa_body,
        grid=(batch, N_KV_GROUPS, n_qt),
        in_specs=in_specs,
        out_specs=pl.BlockSpec((ATT_BLK, hpg * HEAD_DIM), lambda b, g, qi: (b * n_qt + qi, g)),
        out_shape=jax.ShapeDtypeStruct((batch * seq, D_ATT), BF16),
        scratch_shapes=[pltpu.VMEM((n_cmp, HEAD_DIM), BF16), pltpu.VMEM((HEAD_DIM, n_cmp), BF16),
                        pltpu.VMEM((seq, HEAD_DIM + n_aug), BF16), pltpu.VMEM((HEAD_DIM, seq), BF16),
                        pltpu.VMEM((seq, HEAD_DIM), BF16), pltpu.VMEM((HEAD_DIM, seq), BF16),
                        pltpu.VMEM((LANES, ATT_BLK), F32), pltpu.VMEM((seq, hpg * ATT_BLK), F32)],
        compiler_params=pltpu.CompilerParams(
            dimension_semantics=("parallel", "parallel", "arbitrary"), vmem_limit_bytes=VMEM_LIMIT_BYTES),
        name="nsa",
    )(proj, proj, proj, proj, proj, proj, proj, gates,
      cmp_pos_k, w_ck1.astype(BF16), w_ck2.astype(BF16), cmp_pos_v, w_cv1.astype(BF16), w_cv2.astype(BF16),
      sweep_tab, cmp_tab, overlap, expand)


def _pool_body(u_ref, w_ref, scale_ref, o_ref):
    gi = pl.program_id(1)
    u = u_ref[...]
    seq = u.shape[0]
    t = lax.broadcasted_iota(jnp.int32, (seq, 1), 0)

    def shifted(x, k):
        return jnp.where(t >= k, pltpu.roll(x, k, 0), 0.0)

    sums = [u]
    for n in range(int(math.log2(POOL_WINDOWS[-1]))):
        sums.append(sums[-1] + shifted(sums[-1], 2 ** n))
    wsum = sums[int(math.log2(POOL_WINDOWS[0]))]
    for idx in range(1, len(POOL_WINDOWS)):
        wsum = jnp.where(gi >= idx, sums[int(math.log2(POOL_WINDOWS[idx]))], wsum)
    window = POOL_WINDOWS[0]
    for idx in range(1, len(POOL_WINDOWS)):
        window = jnp.where(gi >= idx, POOL_WINDOWS[idx], window)
    cnt = jnp.minimum(t + 1, window).astype(F32)
    d = (wsum / cnt - u).astype(BF16)
    o_ref[...] = (_dot(d, w_ref[0]) * scale_ref[...]).astype(o_ref.dtype)


def _pool(proj, batch, seq, w_pool, pool_scale):
    n_g, dg, _ = w_pool.shape
    assert all(w == 2 ** int(math.log2(w)) for w in POOL_WINDOWS) and list(POOL_WINDOWS) == sorted(POOL_WINDOWS)
    col0 = COL_U // dg
    return pl.pallas_call(
        _pool_body,
        grid=(batch, n_g),
        in_specs=[
            pl.BlockSpec((seq, dg), lambda b, gi: (b, col0 + gi)),
            pl.BlockSpec((1, dg, dg), lambda b, gi: (gi, 0, 0)),
            pl.BlockSpec((1, dg), lambda b, gi: (0, gi)),
        ],
        out_specs=pl.BlockSpec((seq, dg), lambda b, gi: (b, gi)),
        out_shape=jax.ShapeDtypeStruct((batch * seq, n_g * dg), BF16),
        compiler_params=pltpu.CompilerParams(
            dimension_semantics=("parallel", "parallel"), vmem_limit_bytes=VMEM_LIMIT_BYTES),
        name="pool",
    )(proj, w_pool.astype(BF16), pool_scale.reshape(1, -1))


def _outproj_body(x_ref, ya_ref, yp_ref, wa_ref, wp_ref, o_ref):
    o_ref[...] = x_ref[...] + _dot(ya_ref[...], wa_ref[...]) + _dot(yp_ref[...], wp_ref[...])


def _outproj(x, y_att, y_pool, w_out):
    n, d = x.shape
    da = y_att.shape[1]
    dp = y_pool.shape[1]
    w = w_out
    return pl.pallas_call(
        _outproj_body,
        grid=(d // OUT_TN, n // OUT_TM),
        in_specs=[
            pl.BlockSpec((OUT_TM, OUT_TN), lambda j, i: (i, j)),
            pl.BlockSpec((OUT_TM, da), lambda j, i: (i, 0)),
            pl.BlockSpec((OUT_TM, dp), lambda j, i: (i, 0)),
            pl.BlockSpec((da, OUT_TN), lambda j, i: (0, j)),
            pl.BlockSpec((dp, OUT_TN), lambda j, i: (da // dp, j)),
        ],
        out_specs=pl.BlockSpec((OUT_TM, OUT_TN), lambda j, i: (i, j)),
        out_shape=jax.ShapeDtypeStruct((n, d), F32),
        compiler_params=pltpu.CompilerParams(
            dimension_semantics=("parallel", "parallel"), vmem_limit_bytes=VMEM_LIMIT_BYTES),
        name="out_proj",
    )(x, y_att, y_pool, w, w)


def kernel(x, norm_ffn1, w_ffn1_gate, w_ffn1_up, w_ffn1_down, norm_mix, w_in, cmp_pos_k, w_cmp_k1, w_cmp_k2,
           cmp_pos_v, w_cmp_v1, w_cmp_v2, w_pool, pool_scale, w_out, rel_table, norm_ffn2, w_ffn2_gate,
           w_ffn2_up, w_ffn2_down, norm_final):
    batch, seq, d = x.shape
    depth = norm_ffn1.shape[0]
    xf = x.reshape(batch * seq, d)
    gf = norm_final.reshape(1, d)
    for l in range(depth):
        later = (w_ffn2_gate[l], w_ffn2_up[l], w_ffn2_down[l], w_in[l].T, w_out[l])
        xf, (w2_gate, w2_up, w2_down, w_in_b, w_out_b) = _ffn(
            xf, norm_ffn1[l].reshape(1, d), w_ffn1_gate[l].astype(BF16), w_ffn1_up[l].astype(BF16),
            w_ffn1_down[l].astype(BF16), gf, final_norm=False, to_cast=later)
        proj, gates = _inproj(xf, norm_mix[l].reshape(1, d), w_in_b)
        y_att = _nsa(proj, gates, batch, seq, cmp_pos_k[l], w_cmp_k1[l], w_cmp_k2[l],
                     cmp_pos_v[l], w_cmp_v1[l], w_cmp_v2[l], rel_table)
        y_pool = _pool(proj, batch, seq, w_pool[l], pool_scale[l])
        xf = _outproj(xf, y_att, y_pool, w_out_b)
        xf, _ = _ffn(xf, norm_ffn2[l].reshape(1, d), w2_gate, w2_up, w2_down, gf, final_norm=(l == depth - 1))
    if depth == 0:
        raise ValueError("depth must be positive")
    return xf.reshape(batch, seq, d)
```

```python
import functools
import math

import jax
import jax.numpy as jnp
import numpy as np
from jax import lax
from jax.experimental import pallas as pl
from jax.experimental.pallas import tpu as pltpu

HEAD_DIM = 128
N_ATT_HEADS = 16
N_KV_GROUPS = 4
HEADS_PER_GROUP = N_ATT_HEADS // N_KV_GROUPS
D_ATT = N_ATT_HEADS * HEAD_DIM
D_KV = N_KV_GROUPS * HEAD_DIM
N_POOL_GROUPS = 4
POOL_WINDOWS = (2, 4, 8, 16)
CMP_BLOCK = 32
CMP_STRIDE = 16
SEL_BLOCK = 64
SEL_TOP = 8
WINDOW = 512
REL_BUCKETS = 32
REL_MAX_DIST = 128
EPS = 1e-6
NEG = -1e30
FORCE_BONUS = 1e4
FUTURE_SCORE = -1e9

LANES = 128
MXU_DIM = 256
VMEM_LIMIT_BYTES = 58 * 2 ** 20

FFN_TM = 512
FFN_TF = 2 * MXU_DIM
FFN_TN = 1024
NORM_ROWS = 128
CAST_ROWS = 16
PROJ_TM = 512
PROJ_TN = 512
PROJ_TG = MXU_DIM
ATT_BLK = 256
OUT_TM = 512
OUT_TN = 1024

BF16 = jnp.bfloat16
F32 = jnp.float32


def _dot(a, b):
    return jnp.dot(a, b, preferred_element_type=F32)


def _dot_nt(a, b):
    return lax.dot_general(a, b, (((1,), (1,)), ((), ())), preferred_element_type=F32)


def _rms(x, g):
    return x * lax.rsqrt(jnp.mean(x * x, axis=-1, keepdims=True) + EPS) * g


def _ffn_body(*refs, final_norm, n_cast, tail):
    x_hbm, g_ref, wg_ref, wu_ref, wd_ref, gf_ref = refs[:6]
    cast_in = refs[6:6 + n_cast]
    o_ref = refs[6 + n_cast]
    cast_out = refs[7 + n_cast:7 + 2 * n_cast]
    h_ref = refs[7 + 2 * n_cast]
    i = pl.program_id(0)
    j = pl.program_id(1)
    last = pl.num_programs(1) - 1
    tm = o_ref.shape[0]

    @pl.when(j == 0)
    def _():
        pltpu.sync_copy(x_hbm.at[pl.ds(pl.multiple_of(i * tm, tm), tm), :], o_ref)
        for r in range(0, tm, NORM_ROWS):
            h_ref[r:r + NORM_ROWS, :] = _rms(o_ref[r:r + NORM_ROWS, :], g_ref[...]).astype(BF16)

    def accumulate(width):
        h = h_ref[...]
        gate = _dot(h, wg_ref[:, :width])
        up = _dot(h, wu_ref[:, :width])
        act = (0.5 * (gate * jax.nn.sigmoid(gate)) * up).astype(BF16)
        for c in range(0, o_ref.shape[1], FFN_TN):
            o_ref[:, c:c + FFN_TN] += _dot(act, wd_ref[:width, c:c + FFN_TN])

    if tail == wg_ref.shape[1]:
        accumulate(tail)
    else:
        pl.when(j < last)(functools.partial(accumulate, wg_ref.shape[1]))
        pl.when(j == last)(functools.partial(accumulate, tail))

    if final_norm:
        @pl.when(j == last)
        def _():
            for r in range(0, tm, NORM_ROWS):
                o_ref[r:r + NORM_ROWS, :] = _rms(o_ref[r:r + NORM_ROWS, :], gf_ref[...])

    for src, dst in zip(cast_in, cast_out):
        dst[...] = src[...].astype(BF16)


def _cast_spec(a, n_steps_j, n_steps):
    r, c = a.shape
    rows = CAST_ROWS
    while pl.cdiv(r, rows) > n_steps:
        rows *= 2
    n_blocks = pl.cdiv(r, rows)
    return pl.BlockSpec((rows, c), lambda i, j: (jnp.minimum(i * n_steps_j + j, n_blocks - 1), 0))


def _ffn(x, g, w_gate, w_up, w_down, gf, *, final_norm, to_cast=()):
    n, d = x.shape
    d_ff = w_gate.shape[1]
    n_i, n_j = n // FFN_TM, pl.cdiv(d_ff, FFN_TF)
    tail = d_ff - (n_j - 1) * FFN_TF
    assert tail % MXU_DIM == 0
    cast_specs = [_cast_spec(a, n_j, n_i * n_j) for a in to_cast]
    outs = pl.pallas_call(
        functools.partial(_ffn_body, final_norm=final_norm, n_cast=len(to_cast), tail=tail),
        grid=(n_i, n_j),
        in_specs=[
            pl.BlockSpec(memory_space=pl.ANY),
            pl.BlockSpec((1, d), lambda i, j: (0, 0)),
            pl.BlockSpec((d, FFN_TF), lambda i, j: (0, j)),
            pl.BlockSpec((d, FFN_TF), lambda i, j: (0, j)),
            pl.BlockSpec((FFN_TF, d), lambda i, j: (j, 0)),
            pl.BlockSpec((1, d), lambda i, j: (0, 0)),
        ] + cast_specs,
        out_specs=[pl.BlockSpec((FFN_TM, d), lambda i, j: (i, 0))] + cast_specs,
        out_shape=[jax.ShapeDtypeStruct((n, d), F32)] + [jax.ShapeDtypeStruct(a.shape, BF16) for a in to_cast],
        scratch_shapes=[pltpu.VMEM((FFN_TM, d), BF16)],
        compiler_params=pltpu.CompilerParams(
            dimension_semantics=("parallel", "arbitrary"), vmem_limit_bytes=VMEM_LIMIT_BYTES),
        name="ffn_final" if final_norm else "ffn",
    )(x, g, w_gate, w_up, w_down, gf, *to_cast)
    return outs[0], outs[1:]


COL_Q = 0
COL_KV = D_ATT
COL_U = D_ATT + 6 * D_KV
D_POOL_COLS = 2048
COL_GATES = COL_U + D_POOL_COLS
N_GATES = 3 * N_ATT_HEADS


def _inproj_body(x_ref, g_ref, wa_ref, wu_ref, wg_ref, o_ref, og_ref, h_ref, *, n_a):
    j = pl.program_id(1)

    @pl.when(j == 0)
    def _():
        for r in range(0, x_ref.shape[0], NORM_ROWS):
            h_ref[r:r + NORM_ROWS, :] = _rms(x_ref[r:r + NORM_ROWS, :], g_ref[...]).astype(BF16)

    @pl.when(j < n_a)
    def _():
        o_ref[...] = _dot_nt(h_ref[...], wa_ref[...])

    @pl.when(j >= n_a)
    def _():
        o_ref[...] = _dot_nt(h_ref[...], wu_ref[...])

    @pl.when(j == pl.num_programs(1) - 1)
    def _():
        og_ref[...] = _dot_nt(h_ref[...], wg_ref[...])


def _inproj(x, g, w_in_t):
    n, d = x.shape
    d_in = w_in_t.shape[0]
    assert d_in - COL_U - N_GATES == D_POOL_COLS and N_GATES <= PROJ_TG
    assert COL_U % PROJ_TN == 0 and D_POOL_COLS % PROJ_TN == 0
    wu = w_in_t[COL_U + N_GATES:]
    wg = jnp.pad(w_in_t[COL_U:COL_U + N_GATES], ((0, PROJ_TG - N_GATES), (0, 0)))
    n_a = COL_U // PROJ_TN
    n_u = D_POOL_COLS // PROJ_TN
    return pl.pallas_call(
        functools.partial(_inproj_body, n_a=n_a),
        grid=(n // PROJ_TM, n_a + n_u),
        in_specs=[
            pl.BlockSpec((PROJ_TM, d), lambda i, j: (i, 0)),
            pl.BlockSpec((1, d), lambda i, j: (0, 0)),
            pl.BlockSpec((PROJ_TN, d), lambda i, j: (jnp.minimum(j, n_a - 1), 0)),
            pl.BlockSpec((PROJ_TN, d), lambda i, j: (jnp.maximum(j - n_a, 0), 0)),
            pl.BlockSpec((PROJ_TG, d), lambda i, j: (0, 0)),
        ],
        out_specs=[pl.BlockSpec((PROJ_TM, PROJ_TN), lambda i, j: (i, j)),
                   pl.BlockSpec((PROJ_TM, PROJ_TG), lambda i, j: (i, 0))],
        out_shape=[jax.ShapeDtypeStruct((n, COL_GATES), F32), jax.ShapeDtypeStruct((n, PROJ_TG), F32)],
        scratch_shapes=[pltpu.VMEM((PROJ_TM, d), BF16)],
        compiler_params=pltpu.CompilerParams(
            dimension_semantics=("parallel", "arbitrary"), vmem_limit_bytes=VMEM_LIMIT_BYTES),
        name="in_proj",
    )(x, g, w_in_t, wu, wg)


def _rel_bucket_np(n):
    max_exact = REL_BUCKETS // 2
    n = np.maximum(n, 0)
    nf = np.maximum(n, 1).astype(np.float32)
    large = max_exact + (np.log(nf / max_exact) / math.log(REL_MAX_DIST / max_exact)
                         * (REL_BUCKETS - max_exact)).astype(np.int32)
    large = np.minimum(large, REL_BUCKETS - 1)
    return np.where(n < max_exact, n, large)


FAR_DIST = int(np.max(np.nonzero(_rel_bucket_np(np.arange(4 * REL_MAX_DIST)) < REL_BUCKETS - 1)[0])) + 1
CMP_ROW_OFF = -(-(FAR_DIST + CMP_BLOCK - 1) // CMP_STRIDE) - 1
TAB_DIAG, TAB_PREV, TAB_FAR, TAB_EDGE = 0, 1, 2, 3


def _pick(rel_table, idx):
    onehot = jnp.asarray(np.arange(REL_BUCKETS)[:, None] == idx[None, :], F32)
    return jnp.einsum("kh,kn->hn", rel_table.astype(F32), onehot, precision=lax.Precision.HIGHEST)


def _group_lanes(t):
    t = t.reshape((N_KV_GROUPS, HEADS_PER_GROUP) + t.shape[1:])
    t = jnp.moveaxis(t, 1, -2)
    return t.reshape(t.shape[:-2] + (HEADS_PER_GROUP * t.shape[-1],))


def _toeplitz_body(p_ref, o_ref):
    blk = o_ref.shape[-1]
    for t in range(o_ref.shape[0]):
        y = pltpu.roll(jnp.broadcast_to(p_ref[t:t + 1, :], (blk, 2 * blk)), 0, 1, stride=1, stride_axis=0)
        o_ref[t] = y[:, :blk]


def _bias_tables(rel_table, n_cmp):
    blk = ATT_BLK
    n_heads = rel_table.shape[1]
    far = REL_BUCKETS - 1
    assert blk >= FAR_DIST and WINDOW % blk == 0
    m = np.arange(2 * blk)
    d = np.where(m < blk, m, m - 2 * blk)
    idx = np.stack([_rel_bucket_np(d), _rel_bucket_np(blk + d), np.full_like(d, far), np.full_like(d, far)])
    mask = np.stack([np.where(d < 0, NEG, 0.0), np.zeros_like(d, np.float64), np.zeros_like(d, np.float64),
                     np.where(d >= 0, NEG, 0.0)]).astype(np.float32)
    profiles = _pick(rel_table, idx.reshape(-1)).reshape(n_heads, 4, 2 * blk) + mask[None]
    sweep = pl.pallas_call(
        _toeplitz_body,
        grid=(N_KV_GROUPS, HEADS_PER_GROUP),
        in_specs=[pl.BlockSpec((None, 4, 2 * blk), lambda g, h: (g * HEADS_PER_GROUP + h, 0, 0))],
        out_specs=pl.BlockSpec((None, 4, blk, blk), lambda g, h: (g, 0, 0, h)),
        out_shape=jax.ShapeDtypeStruct((N_KV_GROUPS, 4, blk, HEADS_PER_GROUP * blk), F32),
        name="bias_tables",
    )(profiles)

    rows = np.arange(n_cmp - CMP_ROW_OFF, n_cmp + (blk - CMP_BLOCK) // CMP_STRIDE + 1)
    dist = np.arange(blk)[None, :] - ((rows[:, None] - n_cmp) * CMP_STRIDE + CMP_BLOCK - 1)
    band = _pick(rel_table, _rel_bucket_np(dist).reshape(-1)).reshape(n_heads, len(rows), blk)
    band = band + np.where(dist < 0, NEG, 0.0).astype(np.float32)[None]
    before = jnp.broadcast_to(rel_table[far].astype(F32)[:, None, None], (n_heads, int(rows[0]), blk))
    after = jnp.full((n_heads, 2 * n_cmp - int(rows[-1]) - 1, blk), NEG, F32)
    cmp_tab = _group_lanes(jnp.concatenate([before, band, after], axis=1))
    return sweep, cmp_tab


def _compress(kv_ref, pos_ref, w1_ref, w2_ref):
    seq, dk = kv_ref.shape
    n_half = seq // CMP_STRIDE
    first = jnp.zeros((n_half, w1_ref.shape[1]), F32)
    second = jnp.zeros((n_half, w1_ref.shape[1]), F32)
    for l in range(CMP_STRIDE):
        rows = kv_ref[pl.ds(l, n_half, stride=CMP_STRIDE), :]
        a = (rows + pos_ref[l:l + 1, :]).astype(BF16)
        b = (rows + pos_ref[CMP_STRIDE + l:CMP_STRIDE + l + 1, :]).astype(BF16)
        first += _dot(a, w1_ref[l * dk:(l + 1) * dk, :])
        second += _dot(b, w1_ref[(CMP_STRIDE + l) * dk:(CMP_STRIDE + l + 1) * dk, :])
    pre = first + pltpu.roll(second, n_half - 1, 0)
    hid = (pre * jax.nn.sigmoid(pre)).astype(BF16)
    return _dot(hid, w2_ref[...])


def _nsa_body(q_ref, kc_ref, vc_ref, ks_ref, vs_ref, kw_ref, vw_ref, gates_ref,
              posk_ref, wk1_ref, wk2_ref, posv_ref, wv1_ref, wv2_ref,
              tab_ref, tabc_ref, overlap_ref, expand_ref,
              o_ref, kcmp_ref, vcmp_t_ref, ksel_ref, vsel_t_ref, kwin_ref, vwin_t_ref, gates_t_ref, s_ref):
    g = pl.program_id(1)
    qi = pl.program_id(2)
    tq = ATT_BLK
    hpg = HEADS_PER_GROUP
    lanes = hpg * tq
    n_cmp = kcmp_ref.shape[0]
    n_sb = overlap_ref.shape[0]
    seq = ks_ref.shape[0]
    assert tq & (tq - 1) == 0 and SEL_BLOCK & (SEL_BLOCK - 1) == 0

    @pl.when(qi == 0)
    def _():
        kcmp_ref[...] = _compress(kc_ref, posk_ref, wk1_ref, wk2_ref).astype(BF16)
        vcmp_t_ref[...] = _compress(vc_ref, posv_ref, wv1_ref, wv2_ref).astype(BF16).T
        for c in range(0, seq, LANES):
            ksel_ref[c:c + LANES, 0:HEAD_DIM] = ks_ref[c:c + LANES, :].astype(BF16)
            ksel_ref[c:c + LANES, HEAD_DIM:] = expand_ref[c:c + LANES, :]
            kwin_ref[c:c + LANES, :] = kw_ref[c:c + LANES, :].astype(BF16)
            vsel_t_ref[:, c:c + LANES] = vs_ref[c:c + LANES, :].T.astype(BF16)
            vwin_t_ref[:, c:c + LANES] = vw_ref[c:c + LANES, :].T.astype(BF16)

    q = q_ref[...] * (HEAD_DIM ** -0.5)
    q_t = jnp.concatenate([q[:, h * HEAD_DIM:(h + 1) * HEAD_DIM].T for h in range(hpg)],
                          axis=1).astype(BF16)
    t_lane = qi * tq + (lax.broadcasted_iota(jnp.int32, (1, lanes), 1) & (tq - 1))

    first_row = pl.multiple_of(n_cmp - qi * (tq // CMP_STRIDE), 8)
    logit = _dot(kcmp_ref[...], q_t) + tabc_ref[pl.ds(first_row, n_cmp), :]
    e = jnp.exp(logit - jnp.max(logit, axis=0, keepdims=True))
    any_valid = (t_lane >= CMP_BLOCK - 1).astype(F32)
    p = e / jnp.sum(e, axis=0, keepdims=True) * any_valid
    o_cmp = _dot(vcmp_t_ref[...], p.astype(BF16))

    psum = p[:, 0:tq]
    for h in range(1, hpg):
        psum = psum + p[:, h * tq:(h + 1) * tq]
    p_hi = psum.astype(BF16)
    r1 = psum - p_hi.astype(F32)
    p_mid = r1.astype(BF16)
    p_lo = (r1 - p_mid.astype(F32)).astype(BF16)
    ov = overlap_ref[...]
    imp = _dot(ov, p_hi) + _dot(ov, p_mid) + _dot(ov, p_lo)
    cur = lax.shift_right_logical(t_lane[:, 0:tq], int(math.log2(SEL_BLOCK)))
    jb = lax.broadcasted_iota(jnp.int32, (n_sb, tq), 0)
    forced = (jb == 0) | (jb == cur) | (jb == cur - 1)
    score = jnp.where(jb > cur, FUTURE_SCORE, imp + jnp.where(forced, FORCE_BONUS, 0.0))
    rank = jnp.zeros((n_sb, tq), jnp.int32)
    for j2 in range(n_sb):
        row = score[j2:j2 + 1, :]
        rank += ((row > score) | ((row == score) & (j2 < jb))).astype(jnp.int32)
    sel_neg = jnp.where(rank < SEL_TOP, 0.0, NEG).astype(BF16)
    n_aug = ksel_ref.shape[1] - HEAD_DIM
    q_sel = jnp.concatenate([q_t, jnp.concatenate([sel_neg] * hpg, axis=1),
                             jnp.zeros((n_aug - n_sb, lanes), BF16)], axis=0)

    def sweep(k_ref, v_t_ref, q_rhs, lo, edge):
        def rows(kb):
            return pl.ds(pl.multiple_of(kb * ATT_BLK, ATT_BLK), ATT_BLK)

        def scores(kb, m):
            dblk = qi - kb
            tab = jnp.minimum(dblk, TAB_FAR)
            if edge is not None:
                tab = jnp.where(dblk == edge, TAB_EDGE, tab)
            s = _dot(k_ref[rows(kb), :], q_rhs) + tab_ref[tab]
            s_ref[rows(kb), :] = s
            return jnp.maximum(m, jnp.max(s, axis=0, keepdims=True))

        def over_chunks(body, init):
            n = qi + 1 - lo
            out = lax.fori_loop(0, n // 2, lambda i, c: body(lo + 2 * i + 1, body(lo + 2 * i, c)), init)
            return lax.cond(n % 2 == 1, lambda c: body(qi, c), lambda c: c, out)

        m = over_chunks(scores, jnp.full((1, lanes), NEG, F32))

        def accumulate(kb, carry):
            l, acc = carry
            pr = jnp.exp(s_ref[rows(kb), :] - m)
            return (l + jnp.sum(pr, axis=0, keepdims=True),
                    acc + _dot(v_t_ref[:, rows(kb)], pr.astype(BF16)))

        l, acc = over_chunks(accumulate, (jnp.zeros((1, lanes), F32), jnp.zeros((HEAD_DIM, lanes), F32)))
        return acc / l

    o_sel = sweep(ksel_ref, vsel_t_ref, q_sel, 0, None)
    n_back = WINDOW // ATT_BLK
    o_win = sweep(kwin_ref, vwin_t_ref, q_t, jnp.maximum(qi - n_back, 0), n_back)

    for c in range(0, tq, LANES):
        gates_t_ref[:, c:c + LANES] = jax.nn.sigmoid(gates_ref[c:c + LANES, :]).T
    for h in range(hpg):
        head = g * hpg + h
        sl = slice(h * tq, (h + 1) * tq)
        y = (gates_t_ref[pl.ds(head, 1), :] * o_cmp[:, sl]
             + gates_t_ref[pl.ds(N_ATT_HEADS + head, 1), :] * o_sel[:, sl]
             + gates_t_ref[pl.ds(2 * N_ATT_HEADS + head, 1), :] * o_win[:, sl])
        for c in range(0, tq, LANES):
            o_ref[c:c + LANES, h * HEAD_DIM:(h + 1) * HEAD_DIM] = y[:, c:c + LANES].T.astype(o_ref.dtype)


def _nsa(proj, gates, batch, seq, cmp_pos_k, w_ck1, w_ck2, cmp_pos_v, w_cv1, w_cv2, rel_table):
    n_qt = seq // ATT_BLK
    hpg = HEADS_PER_GROUP
    n_cmp = seq // CMP_STRIDE
    n_sb = seq // SEL_BLOCK
    n_aug = LANES
    assert CMP_BLOCK == 2 * CMP_STRIDE and n_cmp % 8 == 0 and n_sb % 16 == 0 and n_sb <= n_aug
    assert (ATT_BLK // CMP_STRIDE) % 8 == 0 and n_qt * (ATT_BLK // CMP_STRIDE) <= n_cmp

    sweep_tab, cmp_tab = _bias_tables(rel_table, n_cmp)
    cs = np.arange(n_cmp)[None, :] * CMP_STRIDE
    ss = np.arange(n_sb)[:, None] * SEL_BLOCK
    overlap = (cs < ss + SEL_BLOCK) & (cs + CMP_BLOCK > ss) & (np.arange(n_cmp)[None, :] < n_cmp - 1)
    expand = np.arange(seq)[:, None] // SEL_BLOCK == np.arange(n_aug)[None, :]
    overlap = jnp.asarray(overlap, BF16)
    expand = jnp.asarray(expand, BF16)

    kv_col0 = COL_KV // HEAD_DIM

    def kv_spec(idx):
        return pl.BlockSpec((seq, HEAD_DIM), lambda b, g, qi, idx=idx: (b, kv_col0 + idx * N_KV_GROUPS + g))

    def whole(arr):
        return pl.BlockSpec(arr.shape, lambda b, g, qi, nd=arr.ndim: (0,) * nd)

    in_specs = [
        pl.BlockSpec((ATT_BLK, hpg * HEAD_DIM), lambda b, g, qi: (b * n_qt + qi, g)),
        kv_spec(0), kv_spec(1), kv_spec(2), kv_spec(3), kv_spec(4), kv_spec(5),
        pl.BlockSpec((ATT_BLK, LANES), lambda b, g, qi: (b * n_qt + qi, 0)),
        whole(cmp_pos_k), whole(w_ck1), whole(w_ck2), whole(cmp_pos_v), whole(w_cv1), whole(w_cv2),
        pl.BlockSpec((None,) + sweep_tab.shape[1:], lambda b, g, qi: (g, 0, 0, 0)),
        pl.BlockSpec((None,) + cmp_tab.shape[1:], lambda b, g, qi: (g, 0, 0)),
        whole(overlap), whole(expand),
    ]
    return pl.pallas_call(
        _nsa_body,
        grid=(batch, N_KV_GROUPS, n_qt),
        in_specs=in_specs,
        out_specs=pl.BlockSpec((ATT_BLK, hpg * HEAD_DIM), lambda b, g, qi: (b * n_qt + qi, g)),
        out_shape=jax.ShapeDtypeStruct((batch * seq, D_ATT), BF16),
        scratch_shapes=[pltpu.VMEM((n_cmp, HEAD_DIM), BF16), pltpu.VMEM((HEAD_DIM, n_cmp), BF16),
                        pltpu.VMEM((seq, HEAD_DIM + n_aug), BF16), pltpu.VMEM((HEAD_DIM, seq), BF16),
                        pltpu.VMEM((seq, HEAD_DIM), BF16), pltpu.VMEM((HEAD_DIM, seq), BF16),
                        pltpu.VMEM((LANES, ATT_BLK), F32), pltpu.VMEM((seq, hpg * ATT_BLK), F32)],
        compiler_params=pltpu.CompilerParams(
            dimension_semantics=("parallel", "parallel", "arbitrary"), vmem_limit_bytes=VMEM_LIMIT_BYTES),
        name="nsa",
    )(proj, proj, proj, proj, proj, proj, proj, gates,
      cmp_pos_k, w_ck1.astype(BF16), w_ck2.astype(BF16), cmp_pos_v, w_cv1.astype(BF16), w_cv2.astype(BF16),
      sweep_tab, cmp_tab, overlap, expand)


def _pool_body(u_ref, w_ref, scale_ref, o_ref):
    gi = pl.program_id(1)
    u = u_ref[...]
    seq = u.shape[0]
    t = lax.broadcasted_iota(jnp.int32, (seq, 1), 0)

    def shifted(x, k):
        return jnp.where(t >= k, pltpu.roll(x, k, 0), 0.0)

    sums = [u]
    for n in range(int(math.log2(POOL_WINDOWS[-1]))):
        sums.append(sums[-1] + shifted(sums[-1], 2 ** n))
    wsum = sums[int(math.log2(POOL_WINDOWS[0]))]
    for idx in range(1, len(POOL_WINDOWS)):
        wsum = jnp.where(gi >= idx, sums[int(math.log2(POOL_WINDOWS[idx]))], wsum)
    window = POOL_WINDOWS[0]
    for idx in range(1, len(POOL_WINDOWS)):
        window = jnp.where(gi >= idx, POOL_WINDOWS[idx], window)
    cnt = jnp.minimum(t + 1, window).astype(F32)
    d = (wsum / cnt - u).astype(BF16)
    o_ref[...] = (_dot(d, w_ref[0]) * scale_ref[...]).astype(o_ref.dtype)


def _pool(proj, batch, seq, w_pool, pool_scale):
    n_g, dg, _ = w_pool.shape
    assert all(w == 2 ** int(math.log2(w)) for w in POOL_WINDOWS) and list(POOL_WINDOWS) == sorted(POOL_WINDOWS)
    col0 = COL_U // dg
    return pl.pallas_call(
        _pool_body,
        grid=(batch, n_g),
        in_specs=[
            pl.BlockSpec((seq, dg), lambda b, gi: (b, col0 + gi)),
            pl.BlockSpec((1, dg, dg), lambda b, gi: (gi, 0, 0)),
            pl.BlockSpec((1, dg), lambda b, gi: (0, gi)),
        ],
        out_specs=pl.BlockSpec((seq, dg), lambda b, gi: (b, gi)),
        out_shape=jax.ShapeDtypeStruct((batch * seq, n_g * dg), BF16),
        compiler_params=pltpu.CompilerParams(
            dimension_semantics=("parallel", "parallel"), vmem_limit_bytes=VMEM_LIMIT_BYTES),
        name="pool",
    )(proj, w_pool.astype(BF16), pool_scale.reshape(1, -1))


def _outproj_body(x_ref, ya_ref, yp_ref, wa_ref, wp_ref, o_ref):
    o_ref[...] = x_ref[...] + _dot(ya_ref[...], wa_ref[...]) + _dot(yp_ref[...], wp_ref[...])


def _outproj(x, y_att, y_pool, w_out):
    n, d = x.shape
    da = y_att.shape[1]
    dp = y_pool.shape[1]
    w = w_out
    return pl.pallas_call(
        _outproj_body,
        grid=(d // OUT_TN, n // OUT_TM),
        in_specs=[
            pl.BlockSpec((OUT_TM, OUT_TN), lambda j, i: (i, j)),
            pl.BlockSpec((OUT_TM, da), lambda j, i: (i, 0)),
            pl.BlockSpec((OUT_TM, dp), lambda j, i: (i, 0)),
            pl.BlockSpec((da, OUT_TN), lambda j, i: (0, j)),
            pl.BlockSpec((dp, OUT_TN), lambda j, i: (da // dp, j)),
        ],
        out_specs=pl.BlockSpec((OUT_TM, OUT_TN), lambda j, i: (i, j)),
        out_shape=jax.ShapeDtypeStruct((n, d), F32),
        compiler_params=pltpu.CompilerParams(
            dimension_semantics=("parallel", "parallel"), vmem_limit_bytes=VMEM_LIMIT_BYTES),
        name="out_proj",
    )(x, y_att, y_pool, w, w)


def kernel(x, norm_ffn1, w_ffn1_gate, w_ffn1_up, w_ffn1_down, norm_mix, w_in, cmp_pos_k, w_cmp_k1, w_cmp_k2,
           cmp_pos_v, w_cmp_v1, w_cmp_v2, w_pool, pool_scale, w_out, rel_table, norm_ffn2, w_ffn2_gate,
           w_ffn2_up, w_ffn2_down, norm_final):
    batch, seq, d = x.shape
    depth = norm_ffn1.shape[0]
    xf = x.reshape(batch * seq, d)
    gf = norm_final.reshape(1, d)
    for l in range(depth):
        later = (w_ffn2_gate[l], w_ffn2_up[l], w_ffn2_down[l], w_in[l].T, w_out[l])
        xf, (w2_gate, w2_up, w2_down, w_in_b, w_out_b) = _ffn(
            xf, norm_ffn1[l].reshape(1, d), w_ffn1_gate[l].astype(BF16), w_ffn1_up[l].astype(BF16),
            w_ffn1_down[l].astype(BF16), gf, final_norm=False, to_cast=later)
        proj, gates = _inproj(xf, norm_mix[l].reshape(1, d), w_in_b)
        y_att = _nsa(proj, gates, batch, seq, cmp_pos_k[l], w_cmp_k1[l], w_cmp_k2[l],
                     cmp_pos_v[l], w_cmp_v1[l], w_cmp_v2[l], rel_table)
        y_pool = _pool(proj, batch, seq, w_pool[l], pool_scale[l])
        xf = _outproj(xf, y_att, y_pool, w_out_b)
        xf, _ = _ffn(xf, norm_ffn2[l].reshape(1, d), w2_gate, w2_up, w2_down, gf, final_norm=(l == depth - 1))
    if depth == 0:
        raise ValueError("depth must be positive")
    return xf.reshape(batch, seq, d)
```

```python
import functools
import math

import jax
import jax.numpy as jnp
import numpy as np
from jax import lax
from jax.experimental import pallas as pl
from jax.experimental.pallas import tpu as pltpu

HEAD_DIM = 128
N_ATT_HEADS = 16
N_KV_GROUPS = 4
HEADS_PER_GROUP = N_ATT_HEADS // N_KV_GROUPS
D_ATT = N_ATT_HEADS * HEAD_DIM
D_KV = N_KV_GROUPS * HEAD_DIM
N_POOL_GROUPS = 4
POOL_WINDOWS = (2, 4, 8, 16)
CMP_BLOCK = 32
CMP_STRIDE = 16
SEL_BLOCK = 64
SEL_TOP = 8
WINDOW = 512
REL_BUCKETS = 32
REL_MAX_DIST = 128
EPS = 1e-6
NEG = -1e30
FORCE_BONUS = 1e4
FUTURE_SCORE = -1e9

LANES = 128
MXU_DIM = 256
VMEM_LIMIT_BYTES = 56 * 2 ** 20

FFN_TM = 512
FFN_TF = MXU_DIM
FFN_TN = 1024
NORM_ROWS = 128
CAST_ROWS = 16
PROJ_TM = 512
PROJ_TN = 1024
PROJ_TG = MXU_DIM
ATT_BLK = 256
OUT_TM = 1024
OUT_TN = 1024

BF16 = jnp.bfloat16
F32 = jnp.float32


def _dot(a, b):
    return jnp.dot(a, b, preferred_element_type=F32)


def _dot_nt(a, b):
    return lax.dot_general(a, b, (((1,), (1,)), ((), ())), preferred_element_type=F32)


def _rms(x, g):
    return x * lax.rsqrt(jnp.mean(x * x, axis=-1, keepdims=True) + EPS) * g


def _ffn_body(*refs, final_norm, n_cast):
    x_ref, g_ref, wg_ref, wu_ref, wd_ref, gf_ref = refs[:6]
    cast_in = refs[6:6 + n_cast]
    o_ref = refs[6 + n_cast]
    cast_out = refs[7 + n_cast:7 + 2 * n_cast]
    h_ref = refs[7 + 2 * n_cast]
    j = pl.program_id(1)

    @pl.when(j == 0)
    def _():
        for r in range(0, x_ref.shape[0], NORM_ROWS):
            x = x_ref[r:r + NORM_ROWS, :]
            h_ref[r:r + NORM_ROWS, :] = _rms(x, g_ref[...]).astype(BF16)
            o_ref[r:r + NORM_ROWS, :] = x

    h = h_ref[...]
    gate = _dot(h, wg_ref[...])
    up = _dot(h, wu_ref[...])
    act = (0.5 * (gate * jax.nn.sigmoid(gate)) * up).astype(BF16)
    for c in range(0, o_ref.shape[1], FFN_TN):
        o_ref[:, c:c + FFN_TN] += _dot(act, wd_ref[:, c:c + FFN_TN])

    if final_norm:
        @pl.when(j == pl.num_programs(1) - 1)
        def _():
            for r in range(0, o_ref.shape[0], NORM_ROWS):
                o_ref[r:r + NORM_ROWS, :] = _rms(o_ref[r:r + NORM_ROWS, :], gf_ref[...])

    for src, dst in zip(cast_in, cast_out):
        dst[...] = src[...].astype(BF16)


def _cast_specs(a, n_i, n_j, out_stripe=None):
    r, c = a.shape
    if out_stripe is None:
        if r % n_i == 0 and c % n_j == 0 and (r // n_i) % CAST_ROWS == 0 and (c // n_j) % LANES == 0:
            return (pl.BlockSpec((r // n_i, c // n_j), lambda i, j: (i, j)),) * 2
        if r % n_j == 0 and c % n_i == 0 and (r // n_j) % CAST_ROWS == 0 and (c // n_i) % LANES == 0:
            return (pl.BlockSpec((r // n_j, c // n_i), lambda i, j: (j, i)),) * 2
        out_stripe = lambda b: b
    n_blocks, rem = divmod(r, CAST_ROWS)
    assert rem == 0 and n_blocks <= n_i * n_j
    stripe = lambda i, j: jnp.minimum(i * n_j + j, n_blocks - 1)
    return (pl.BlockSpec((CAST_ROWS, c), lambda i, j: (stripe(i, j), 0)),
            pl.BlockSpec((CAST_ROWS, c), lambda i, j: (out_stripe(stripe(i, j)), 0)))


def _ffn(x, g, w_gate, w_up, w_down, gf, *, final_norm, to_cast=()):
    n, d = x.shape
    d_ff = w_gate.shape[1]
    n_i, n_j = n // FFN_TM, d_ff // FFN_TF
    to_cast = [a if isinstance(a, tuple) else (a, None) for a in to_cast]
    cast_specs = [_cast_specs(a, n_i, n_j, out_stripe) for a, out_stripe in to_cast]
    to_cast = [a for a, _ in to_cast]
    outs = pl.pallas_call(
        functools.partial(_ffn_body, final_norm=final_norm, n_cast=len(to_cast)),
        grid=(n_i, n_j),
        in_specs=[
            pl.BlockSpec((FFN_TM, d), lambda i, j: (i, 0)),
            pl.BlockSpec((1, d), lambda i, j: (0, 0)),
            pl.BlockSpec((d, FFN_TF), lambda i, j: (0, j)),
            pl.BlockSpec((d, FFN_TF), lambda i, j: (0, j)),
            pl.BlockSpec((FFN_TF, d), lambda i, j: (j, 0)),
            pl.BlockSpec((1, d), lambda i, j: (0, 0)),
        ] + [cs[0] for cs in cast_specs],
        out_specs=[pl.BlockSpec((FFN_TM, d), lambda i, j: (i, 0))] + [cs[1] for cs in cast_specs],
        out_shape=[jax.ShapeDtypeStruct((n, d), F32)] + [jax.ShapeDtypeStruct(a.shape, BF16) for a in to_cast],
        scratch_shapes=[pltpu.VMEM((FFN_TM, d), BF16)],
        compiler_params=pltpu.CompilerParams(
            dimension_semantics=("parallel", "arbitrary"), vmem_limit_bytes=VMEM_LIMIT_BYTES),
        name="ffn_final" if final_norm else "ffn",
    )(x, g, w_gate, w_up, w_down, gf, *to_cast)
    return outs[0], outs[1:]


COL_Q = 0
COL_KV = D_ATT
COL_U = D_ATT + 6 * D_KV
D_POOL_COLS = 2048
COL_GATES = COL_U + D_POOL_COLS
N_GATES = 3 * N_ATT_HEADS


def _inproj_body(x_ref, g_ref, w_ref, wg_ref, o_ref, og_ref, h_ref):
    j = pl.program_id(1)

    @pl.when(j == 0)
    def _():
        for r in range(0, x_ref.shape[0], NORM_ROWS):
            h_ref[r:r + NORM_ROWS, :] = _rms(x_ref[r:r + NORM_ROWS, :], g_ref[...]).astype(BF16)

    o_ref[...] = _dot_nt(h_ref[...], w_ref[...])

    @pl.when(j == pl.num_programs(1) - 1)
    def _():
        og_ref[...] = _dot_nt(h_ref[...], wg_ref[...])


def _inproj_stripe(b):
    n_qkv, n_g, n_u = COL_U // CAST_ROWS, N_GATES // CAST_ROWS, D_POOL_COLS // CAST_ROWS
    return jnp.where(b < n_qkv, b, jnp.where(b < n_qkv + n_g, b + n_u, b - n_g))


def _inproj(x, g, w_r):
    n, d = x.shape
    assert w_r.shape[0] == COL_GATES + N_GATES and N_GATES <= PROJ_TG
    assert COL_GATES % PROJ_TN == 0 and COL_U % CAST_ROWS == 0 and N_GATES % CAST_ROWS == 0
    wg = jnp.pad(w_r[COL_GATES:], ((0, PROJ_TG - N_GATES), (0, 0)))
    return pl.pallas_call(
        _inproj_body,
        grid=(n // PROJ_TM, COL_GATES // PROJ_TN),
        in_specs=[
            pl.BlockSpec((PROJ_TM, d), lambda i, j: (i, 0)),
            pl.BlockSpec((1, d), lambda i, j: (0, 0)),
            pl.BlockSpec((PROJ_TN, d), lambda i, j: (j, 0)),
            pl.BlockSpec((PROJ_TG, d), lambda i, j: (0, 0)),
        ],
        out_specs=[pl.BlockSpec((PROJ_TM, PROJ_TN), lambda i, j: (i, j)),
                   pl.BlockSpec((PROJ_TM, PROJ_TG), lambda i, j: (i, 0))],
        out_shape=[jax.ShapeDtypeStruct((n, COL_GATES), F32), jax.ShapeDtypeStruct((n, PROJ_TG), F32)],
        scratch_shapes=[pltpu.VMEM((PROJ_TM, d), BF16)],
        compiler_params=pltpu.CompilerParams(
            dimension_semantics=("parallel", "arbitrary"), vmem_limit_bytes=VMEM_LIMIT_BYTES),
        name="in_proj",
    )(x, g, w_r, wg)


def _rel_bucket_np(n):
    max_exact = REL_BUCKETS // 2
    n = np.maximum(n, 0)
    nf = np.maximum(n, 1).astype(np.float32)
    large = max_exact + (np.log(nf / max_exact) / math.log(REL_MAX_DIST / max_exact)
                         * (REL_BUCKETS - max_exact)).astype(np.int32)
    large = np.minimum(large, REL_BUCKETS - 1)
    return np.where(n < max_exact, n, large)


FAR_DIST = int(np.max(np.nonzero(_rel_bucket_np(np.arange(4 * REL_MAX_DIST)) < REL_BUCKETS - 1)[0])) + 1
CMP_ROW_OFF = -(-(FAR_DIST + CMP_BLOCK - 1) // CMP_STRIDE) - 1
TAB_DIAG, TAB_PREV, TAB_FAR, TAB_EDGE = 0, 1, 2, 3


def _pick(rel_table, idx):
    onehot = jnp.asarray(np.arange(REL_BUCKETS)[:, None] == idx[None, :], F32)
    return jnp.einsum("kh,kn->hn", rel_table.astype(F32), onehot, precision=lax.Precision.HIGHEST)


def _group_lanes(t):
    t = t.reshape((N_KV_GROUPS, HEADS_PER_GROUP) + t.shape[1:])
    t = jnp.moveaxis(t, 1, -2)
    return t.reshape(t.shape[:-2] + (HEADS_PER_GROUP * t.shape[-1],))


def _toeplitz_body(p_ref, o_ref):
    blk = o_ref.shape[-1]
    for t in range(o_ref.shape[0]):
        y = pltpu.roll(jnp.broadcast_to(p_ref[t:t + 1, :], (blk, 2 * blk)), 0, 1, stride=1, stride_axis=0)
        o_ref[t] = y[:, :blk]


def _bias_tables(rel_table, n_cmp):
    blk = ATT_BLK
    n_heads = rel_table.shape[1]
    far = REL_BUCKETS - 1
    assert blk >= FAR_DIST and WINDOW % blk == 0
    m = np.arange(2 * blk)
    d = np.where(m < blk, m, m - 2 * blk)
    idx = np.stack([_rel_bucket_np(d), _rel_bucket_np(blk + d), np.full_like(d, far), np.full_like(d, far)])
    mask = np.stack([np.where(d < 0, NEG, 0.0), np.zeros_like(d, np.float64), np.zeros_like(d, np.float64),
                     np.where(d >= 0, NEG, 0.0)]).astype(np.float32)
    profiles = _pick(rel_table, idx.reshape(-1)).reshape(n_heads, 4, 2 * blk) + mask[None]
    sweep = pl.pallas_call(
        _toeplitz_body,
        grid=(N_KV_GROUPS, HEADS_PER_GROUP),
        in_specs=[pl.BlockSpec((None, 4, 2 * blk), lambda g, h: (g * HEADS_PER_GROUP + h, 0, 0))],
        out_specs=pl.BlockSpec((None, 4, blk, blk), lambda g, h: (g, 0, 0, h)),
        out_shape=jax.ShapeDtypeStruct((N_KV_GROUPS, 4, blk, HEADS_PER_GROUP * blk), F32),
        name="bias_tables",
    )(profiles)

    rows = np.arange(n_cmp - CMP_ROW_OFF, n_cmp + (blk - CMP_BLOCK) // CMP_STRIDE + 1)
    dist = np.arange(blk)[None, :] - ((rows[:, None] - n_cmp) * CMP_STRIDE + CMP_BLOCK - 1)
    band = _pick(rel_table, _rel_bucket_np(dist).reshape(-1)).reshape(n_heads, len(rows), blk)
    band = band + np.where(dist < 0, NEG, 0.0).astype(np.float32)[None]
    before = jnp.broadcast_to(rel_table[far].astype(F32)[:, None, None], (n_heads, int(rows[0]), blk))
    after = jnp.full((n_heads, 2 * n_cmp - int(rows[-1]) - 1, blk), NEG, F32)
    cmp_tab = _group_lanes(jnp.concatenate([before, band, after], axis=1))
    return sweep, cmp_tab


def _compress(kv_ref, pos_ref, w1_ref, w2_ref):
    seq, dk = kv_ref.shape
    n_half = seq // CMP_STRIDE
    first = jnp.zeros((n_half, w1_ref.shape[1]), F32)
    second = jnp.zeros((n_half, w1_ref.shape[1]), F32)
    for l in range(CMP_STRIDE):
        rows = kv_ref[pl.ds(l, n_half, stride=CMP_STRIDE), :]
        a = (rows + pos_ref[l:l + 1, :]).astype(BF16)
        b = (rows + pos_ref[CMP_STRIDE + l:CMP_STRIDE + l + 1, :]).astype(BF16)
        first += _dot(a, w1_ref[l * dk:(l + 1) * dk, :])
        second += _dot(b, w1_ref[(CMP_STRIDE + l) * dk:(CMP_STRIDE + l + 1) * dk, :])
    pre = first + pltpu.roll(second, n_half - 1, 0)
    hid = (pre * jax.nn.sigmoid(pre)).astype(BF16)
    return _dot(hid, w2_ref[...])


def _nsa_body(q_ref, kc_ref, vc_ref, ks_ref, vs_ref, kw_ref, vw_ref, gates_ref,
              posk_ref, wk1_ref, wk2_ref, posv_ref, wv1_ref, wv2_ref,
              tab_ref, tabc_ref, overlap_ref, expand_ref,
              o_ref, kcmp_ref, vcmp_t_ref, ksel_ref, vsel_t_ref, kwin_ref, vwin_t_ref, gates_t_ref, s_ref):
    g = pl.program_id(1)
    qi = pl.program_id(2)
    tq = ATT_BLK
    hpg = HEADS_PER_GROUP
    lanes = hpg * tq
    n_cmp = kcmp_ref.shape[0]
    n_sb = overlap_ref.shape[0]
    seq = ks_ref.shape[0]
    assert tq & (tq - 1) == 0 and SEL_BLOCK & (SEL_BLOCK - 1) == 0

    @pl.when(qi == 0)
    def _():
        kcmp_ref[...] = _compress(kc_ref, posk_ref, wk1_ref, wk2_ref).astype(BF16)
        vcmp_t_ref[...] = _compress(vc_ref, posv_ref, wv1_ref, wv2_ref).astype(BF16).T
        for c in range(0, seq, LANES):
            ksel_ref[c:c + LANES, 0:HEAD_DIM] = ks_ref[c:c + LANES, :].astype(BF16)
            ksel_ref[c:c + LANES, HEAD_DIM:] = expand_ref[c:c + LANES, :]
            kwin_ref[c:c + LANES, :] = kw_ref[c:c + LANES, :].astype(BF16)
            vsel_t_ref[:, c:c + LANES] = vs_ref[c:c + LANES, :].T.astype(BF16)
            vwin_t_ref[:, c:c + LANES] = vw_ref[c:c + LANES, :].T.astype(BF16)

    q = q_ref[...] * (HEAD_DIM ** -0.5)
    q_t = jnp.concatenate([q[:, h * HEAD_DIM:(h + 1) * HEAD_DIM].T for h in range(hpg)],
                          axis=1).astype(BF16)
    t_lane = qi * tq + (lax.broadcasted_iota(jnp.int32, (1, lanes), 1) & (tq - 1))

    first_row = pl.multiple_of(n_cmp - qi * (tq // CMP_STRIDE), 8)
    logit = _dot(kcmp_ref[...], q_t) + tabc_ref[pl.ds(first_row, n_cmp), :]
    e = jnp.exp(logit - jnp.max(logit, axis=0, keepdims=True))
    any_valid = (t_lane >= CMP_BLOCK - 1).astype(F32)
    p = e / jnp.sum(e, axis=0, keepdims=True) * any_valid
    o_cmp = _dot(vcmp_t_ref[...], p.astype(BF16))

    psum = p[:, 0:tq]
    for h in range(1, hpg):
        psum = psum + p[:, h * tq:(h + 1) * tq]
    p_hi = psum.astype(BF16)
    r1 = psum - p_hi.astype(F32)
    p_mid = r1.astype(BF16)
    p_lo = (r1 - p_mid.astype(F32)).astype(BF16)
    ov = overlap_ref[...]
    imp = _dot(ov, p_hi) + _dot(ov, p_mid) + _dot(ov, p_lo)
    cur = lax.shift_right_logical(t_lane[:, 0:tq], int(math.log2(SEL_BLOCK)))
    jb = lax.broadcasted_iota(jnp.int32, (n_sb, tq), 0)
    forced = (jb == 0) | (jb == cur) | (jb == cur - 1)
    score = jnp.where(jb > cur, FUTURE_SCORE, imp + jnp.where(forced, FORCE_BONUS, 0.0))
    rank = jnp.zeros((n_sb, tq), jnp.int32)
    for j2 in range(n_sb):
        row = score[j2:j2 + 1, :]
        rank += ((row > score) | ((row == score) & (j2 < jb))).astype(jnp.int32)
    sel_neg = jnp.where(rank < SEL_TOP, 0.0, NEG).astype(BF16)
    n_aug = ksel_ref.shape[1] - HEAD_DIM
    q_sel = jnp.concatenate([q_t, jnp.concatenate([sel_neg] * hpg, axis=1),
                             jnp.zeros((n_aug - n_sb, lanes), BF16)], axis=0)

    def sweep(k_ref, v_t_ref, q_rhs, lo, edge):
        def rows(kb):
            return pl.ds(pl.multiple_of(kb * ATT_BLK, ATT_BLK), ATT_BLK)

        def scores(kb, m):
            dblk = qi - kb
            tab = jnp.minimum(dblk, TAB_FAR)
            if edge is not None:
                tab = jnp.where(dblk == edge, TAB_EDGE, tab)
            s = _dot(k_ref[rows(kb), :], q_rhs) + tab_ref[tab]
            s_ref[rows(kb), :] = s
            return jnp.maximum(m, jnp.max(s, axis=0, keepdims=True))

        def over_chunks(body, init):
            n = qi + 1 - lo
            out = lax.fori_loop(0, n // 2, lambda i, c: body(lo + 2 * i + 1, body(lo + 2 * i, c)), init)
            return lax.cond(n % 2 == 1, lambda c: body(qi, c), lambda c: c, out)

        m = over_chunks(scores, jnp.full((1, lanes), NEG, F32))

        def accumulate(kb, carry):
            l, acc = carry
            pr = jnp.exp(s_ref[rows(kb), :] - m)
            return (l + jnp.sum(pr, axis=0, keepdims=True),
                    acc + _dot(v_t_ref[:, rows(kb)], pr.astype(BF16)))

        l, acc = over_chunks(accumulate, (jnp.zeros((1, lanes), F32), jnp.zeros((HEAD_DIM, lanes), F32)))
        return acc / l

    o_sel = sweep(ksel_ref, vsel_t_ref, q_sel, 0, None)
    n_back = WINDOW // ATT_BLK
    o_win = sweep(kwin_ref, vwin_t_ref, q_t, jnp.maximum(qi - n_back, 0), n_back)

    for c in range(0, tq, LANES):
        gates_t_ref[:, c:c + LANES] = jax.nn.sigmoid(gates_ref[c:c + LANES, :]).T
    for h in range(hpg):
        head = g * hpg + h
        sl = slice(h * tq, (h + 1) * tq)
        y = (gates_t_ref[pl.ds(head, 1), :] * o_cmp[:, sl]
             + gates_t_ref[pl.ds(N_ATT_HEADS + head, 1), :] * o_sel[:, sl]
             + gates_t_ref[pl.ds(2 * N_ATT_HEADS + head, 1), :] * o_win[:, sl])
        for c in range(0, tq, LANES):
            o_ref[c:c + LANES, h * HEAD_DIM:(h + 1) * HEAD_DIM] = y[:, c:c + LANES].T.astype(o_ref.dtype)


def _nsa(proj, gates, batch, seq, cmp_pos_k, w_ck1, w_ck2, cmp_pos_v, w_cv1, w_cv2, rel_table):
    n_qt = seq // ATT_BLK
    hpg = HEADS_PER_GROUP
    n_cmp = seq // CMP_STRIDE
    n_sb = seq // SEL_BLOCK
    n_aug = LANES
    assert CMP_BLOCK == 2 * CMP_STRIDE and n_cmp % 8 == 0 and n_sb % 16 == 0 and n_sb <= n_aug
    assert (ATT_BLK // CMP_STRIDE) % 8 == 0 and n_qt * (ATT_BLK // CMP_STRIDE) <= n_cmp

    sweep_tab, cmp_tab = _bias_tables(rel_table, n_cmp)
    cs = np.arange(n_cmp)[None, :] * CMP_STRIDE
    ss = np.arange(n_sb)[:, None] * SEL_BLOCK
    overlap = (cs < ss + SEL_BLOCK) & (cs + CMP_BLOCK > ss) & (np.arange(n_cmp)[None, :] < n_cmp - 1)
    expand = np.arange(seq)[:, None] // SEL_BLOCK == np.arange(n_aug)[None, :]
    overlap = jnp.asarray(overlap, BF16)
    expand = jnp.asarray(expand, BF16)

    kv_col0 = COL_KV // HEAD_DIM

    def kv_spec(idx):
        return pl.BlockSpec((seq, HEAD_DIM), lambda b, g, qi, idx=idx: (b, kv_col0 + idx * N_KV_GROUPS + g))

    def whole(arr):
        return pl.BlockSpec(arr.shape, lambda b, g, qi, nd=arr.ndim: (0,) * nd)

    in_specs = [
        pl.BlockSpec((ATT_BLK, hpg * HEAD_DIM), lambda b, g, qi: (b * n_qt + qi, g)),
        kv_spec(0), kv_spec(1), kv_spec(2), kv_spec(3), kv_spec(4), kv_spec(5),
        pl.BlockSpec((ATT_BLK, LANES), lambda b, g, qi: (b * n_qt + qi, 0)),
        whole(cmp_pos_k), whole(w_ck1), whole(w_ck2), whole(cmp_pos_v), whole(w_cv1), whole(w_cv2),
        pl.BlockSpec((None,) + sweep_tab.shape[1:], lambda b, g, qi: (g, 0, 0, 0)),
        pl.BlockSpec((None,) + cmp_tab.shape[1:], lambda b, g, qi: (g, 0, 0)),
        whole(overlap), whole(expand),
    ]
    return pl.pallas_call(
        _nsa_body,
        grid=(batch, N_KV_GROUPS, n_qt),
        in_specs=in_specs,
        out_specs=pl.BlockSpec((ATT_BLK, hpg * HEAD_DIM), lambda b, g, qi: (b * n_qt + qi, g)),
        out_shape=jax.ShapeDtypeStruct((batch * seq, D_ATT), BF16),
        scratch_shapes=[pltpu.VMEM((n_cmp, HEAD_DIM), BF16), pltpu.VMEM((HEAD_DIM, n_cmp), BF16),
                        pltpu.VMEM((seq, HEAD_DIM + n_aug), BF16), pltpu.VMEM((HEAD_DIM, seq), BF16),
                        pltpu.VMEM((seq, HEAD_DIM), BF16), pltpu.VMEM((HEAD_DIM, seq), BF16),
                        pltpu.VMEM((LANES, ATT_BLK), F32), pltpu.VMEM((seq, hpg * ATT_BLK), F32)],
        compiler_params=pltpu.CompilerParams(
            dimension_semantics=("parallel", "parallel", "arbitrary"), vmem_limit_bytes=VMEM_LIMIT_BYTES),
        name="nsa",
    )(proj, proj, proj, proj, proj, proj, proj, gates,
      cmp_pos_k, w_ck1.astype(BF16), w_ck2.astype(BF16), cmp_pos_v, w_cv1.astype(BF16), w_cv2.astype(BF16),
      sweep_tab, cmp_tab, overlap, expand)


def _pool_body(u_ref, w_ref, scale_ref, o_ref):
    gi = pl.program_id(1)
    u = u_ref[...]
    seq = u.shape[0]
    t = lax.broadcasted_iota(jnp.int32, (seq, 1), 0)

    def shifted(x, k):
        return jnp.where(t >= k, pltpu.roll(x, k, 0), 0.0)

    sums = [u]
    for n in range(int(math.log2(POOL_WINDOWS[-1]))):
        sums.append(sums[-1] + shifted(sums[-1], 2 ** n))
    wsum = sums[int(math.log2(POOL_WINDOWS[0]))]
    for idx in range(1, len(POOL_WINDOWS)):
        wsum = jnp.where(gi >= idx, sums[int(math.log2(POOL_WINDOWS[idx]))], wsum)
    window = POOL_WINDOWS[0]
    for idx in range(1, len(POOL_WINDOWS)):
        window = jnp.where(gi >= idx, POOL_WINDOWS[idx], window)
    cnt = jnp.minimum(t + 1, window).astype(F32)
    d = (wsum / cnt - u).astype(BF16)
    o_ref[...] = (_dot(d, w_ref[0]) * scale_ref[...]).astype(o_ref.dtype)


def _pool(proj, batch, seq, w_pool, pool_scale):
    n_g, dg, _ = w_pool.shape
    assert all(w == 2 ** int(math.log2(w)) for w in POOL_WINDOWS) and list(POOL_WINDOWS) == sorted(POOL_WINDOWS)
    col0 = COL_U // dg
    return pl.pallas_call(
        _pool_body,
        grid=(batch, n_g),
        in_specs=[
            pl.BlockSpec((seq, dg), lambda b, gi: (b, col0 + gi)),
            pl.BlockSpec((1, dg, dg), lambda b, gi: (gi, 0, 0)),
            pl.BlockSpec((1, dg), lambda b, gi: (0, gi)),
        ],
        out_specs=pl.BlockSpec((seq, dg), lambda b, gi: (b, gi)),
        out_shape=jax.ShapeDtypeStruct((batch * seq, n_g * dg), BF16),
        compiler_params=pltpu.CompilerParams(
            dimension_semantics=("parallel", "parallel"), vmem_limit_bytes=VMEM_LIMIT_BYTES),
        name="pool",
    )(proj, w_pool.astype(BF16), pool_scale.reshape(1, -1))


def _outproj_body(x_ref, ya_ref, yp_ref, wa_ref, wp_ref, o_ref):
    o_ref[...] = x_ref[...] + _dot(ya_ref[...], wa_ref[...]) + _dot(yp_ref[...], wp_ref[...])


def _outproj(x, y_att, y_pool, w_out):
    n, d = x.shape
    da = y_att.shape[1]
    dp = y_pool.shape[1]
    w = w_out
    return pl.pallas_call(
        _outproj_body,
        grid=(d // OUT_TN, n // OUT_TM),
        in_specs=[
            pl.BlockSpec((OUT_TM, OUT_TN), lambda j, i: (i, j)),
            pl.BlockSpec((OUT_TM, da), lambda j, i: (i, 0)),
            pl.BlockSpec((OUT_TM, dp), lambda j, i: (i, 0)),
            pl.BlockSpec((da, OUT_TN), lambda j, i: (0, j)),
            pl.BlockSpec((dp, OUT_TN), lambda j, i: (da // dp, j)),
        ],
        out_specs=pl.BlockSpec((OUT_TM, OUT_TN), lambda j, i: (i, j)),
        out_shape=jax.ShapeDtypeStruct((n, d), F32),
        compiler_params=pltpu.CompilerParams(
            dimension_semantics=("parallel", "parallel"), vmem_limit_bytes=VMEM_LIMIT_BYTES),
        name="out_proj",
    )(x, y_att, y_pool, w, w)


def kernel(x, norm_ffn1, w_ffn1_gate, w_ffn1_up, w_ffn1_down, norm_mix, w_in, cmp_pos_k, w_cmp_k1, w_cmp_k2,
           cmp_pos_v, w_cmp_v1, w_cmp_v2, w_pool, pool_scale, w_out, rel_table, norm_ffn2, w_ffn2_gate,
           w_ffn2_up, w_ffn2_down, norm_final):
    batch, seq, d = x.shape
    depth = norm_ffn1.shape[0]
    xf = x.reshape(batch * seq, d)
    gf = norm_final.reshape(1, d)
    for l in range(depth):
        later = (w_ffn2_gate[l], w_ffn2_up[l], w_ffn2_down[l], (w_in[l].T, _inproj_stripe), w_out[l])
        xf, (w2_gate, w2_up, w2_down, w_in_b, w_out_b) = _ffn(
            xf, norm_ffn1[l].reshape(1, d), w_ffn1_gate[l].astype(BF16), w_ffn1_up[l].astype(BF16),
            w_ffn1_down[l].astype(BF16), gf, final_norm=False, to_cast=later)
        proj, gates = _inproj(xf, norm_mix[l].reshape(1, d), w_in_b)
        y_att = _nsa(proj, gates, batch, seq, cmp_pos_k[l], w_cmp_k1[l], w_cmp_k2[l],
                     cmp_pos_v[l], w_cmp_v1[l], w_cmp_v2[l], rel_table)
        y_pool = _pool(proj, batch, seq, w_pool[l], pool_scale[l])
        xf = _outproj(xf, y_att, y_pool, w_out_b)
        xf, _ = _ffn(xf, norm_ffn2[l].reshape(1, d), w2_gate, w2_up, w2_down, gf, final_norm=(l == depth - 1))
    if depth == 0:
        raise ValueError("depth must be positive")
    return xf.reshape(batch, seq, d)
```

```python
import functools
import math

import jax
import jax.numpy as jnp
import numpy as np
from jax import lax
from jax.experimental import pallas as pl
from jax.experimental.pallas import tpu as pltpu

HEAD_DIM = 128
N_ATT_HEADS = 16
N_KV_GROUPS = 4
HEADS_PER_GROUP = N_ATT_HEADS // N_KV_GROUPS
D_ATT = N_ATT_HEADS * HEAD_DIM
D_KV = N_KV_GROUPS * HEAD_DIM
N_POOL_GROUPS = 4
POOL_WINDOWS = (2, 4, 8, 16)
CMP_BLOCK = 32
CMP_STRIDE = 16
SEL_BLOCK = 64
SEL_TOP = 8
WINDOW = 512
REL_BUCKETS = 32
REL_MAX_DIST = 128
EPS = 1e-6
NEG = -1e30
FORCE_BONUS = 1e4
FUTURE_SCORE = -1e9

LANES = 128
MXU_DIM = 256
VMEM_LIMIT_BYTES = 58 * 2 ** 20

FFN_TM = 512
FFN_TF = MXU_DIM
FFN_TF_WIDE = 2 * MXU_DIM
FFN_TN = 1024
NORM_ROWS = 32
CAST_ROWS = 16
PROJ_TM = 512
PROJ_TN = 1024
PROJ_TG = MXU_DIM
ATT_BLK = 256
OUT_TM = 1024
OUT_TN = 1024

BF16 = jnp.bfloat16
F32 = jnp.float32


def _dot(a, b):
    return jnp.dot(a, b, preferred_element_type=F32)


def _dot_nt(a, b):
    return lax.dot_general(a, b, (((1,), (1,)), ((), ())), preferred_element_type=F32)


def _rms(x, g):
    return x * lax.rsqrt(jnp.mean(x * x, axis=-1, keepdims=True) + EPS) * g


def _ffn_body(*refs, final_norm, n_cast, tf, tail, prefetch_x):
    x_ref, g_ref, wg_ref, wu_ref, wd_ref, gf_ref = refs[:6]
    cast_in = refs[6:6 + n_cast]
    o_ref = refs[6 + n_cast]
    cast_out = refs[7 + n_cast:7 + 2 * n_cast]
    scratch = refs[7 + 2 * n_cast:]
    h_ref = scratch[0]
    i = pl.program_id(0)
    j = pl.program_id(1)
    last = pl.num_programs(1) - 1
    tm = o_ref.shape[0]

    if prefetch_x:
        xbuf_ref, sem = scratch[1:]

        def x_copy(tile):
            return pltpu.make_async_copy(x_ref.at[pl.ds(pl.multiple_of(tile * tm, tm), tm), :], xbuf_ref, sem)

        @pl.when((i == 0) & (j == 0))
        def _():
            x_copy(0).start()

        @pl.when(j == 0)
        def _():
            x_copy(i).wait()

        @pl.when((j == last - 1) & (i + 1 < pl.num_programs(0)))
        def _():
            x_copy(i + 1).start()

        x_tile = xbuf_ref
    else:
        x_tile = x_ref

    @pl.when(j == 0)
    def _():
        for r in range(0, tm, NORM_ROWS):
            x = x_tile[r:r + NORM_ROWS, :]
            h_ref[r:r + NORM_ROWS, :] = _rms(x, g_ref[...]).astype(BF16)
            o_ref[r:r + NORM_ROWS, :] = x

    def accumulate(width):
        h = h_ref[...]
        gate = _dot(h, wg_ref[:, :width])
        up = _dot(h, wu_ref[:, :width])
        act = (0.5 * (gate * jax.nn.sigmoid(gate)) * up).astype(BF16)
        for c in range(0, o_ref.shape[1], FFN_TN):
            o_ref[:, c:c + FFN_TN] += _dot(act, wd_ref[:width, c:c + FFN_TN])

    if tail == tf:
        accumulate(tf)
    else:
        pl.when(j < last)(functools.partial(accumulate, tf))
        pl.when(j == last)(functools.partial(accumulate, tail))

    if final_norm:
        @pl.when(j == last)
        def _():
            for r in range(0, tm, NORM_ROWS):
                o_ref[r:r + NORM_ROWS, :] = _rms(o_ref[r:r + NORM_ROWS, :], gf_ref[...])

    for src, dst in zip(cast_in, cast_out):
        dst[...] = src[...].astype(BF16)


def _cast_specs(a, n_i, n_j, out_stripe=None):
    r, c = a.shape
    if out_stripe is None:
        if r % n_i == 0 and c % n_j == 0 and (r // n_i) % CAST_ROWS == 0 and (c // n_j) % LANES == 0:
            return (pl.BlockSpec((r // n_i, c // n_j), lambda i, j: (i, j)),) * 2
        if r % n_j == 0 and c % n_i == 0 and (r // n_j) % CAST_ROWS == 0 and (c // n_i) % LANES == 0:
            return (pl.BlockSpec((r // n_j, c // n_i), lambda i, j: (j, i)),) * 2
        out_stripe = lambda b: b
    n_blocks, rem = divmod(r, CAST_ROWS)
    assert rem == 0 and n_blocks <= n_i * n_j
    stripe = lambda i, j: jnp.minimum(i * n_j + j, n_blocks - 1)
    return (pl.BlockSpec((CAST_ROWS, c), lambda i, j: (stripe(i, j), 0)),
            pl.BlockSpec((CAST_ROWS, c), lambda i, j: (out_stripe(stripe(i, j)), 0)))


def _ffn(x, g, w_gate, w_up, w_down, gf, *, final_norm, to_cast=()):
    n, d = x.shape
    d_ff = w_gate.shape[1]
    prefetch_x = not to_cast
    tf = FFN_TF_WIDE if prefetch_x else FFN_TF
    n_i, n_j = n // FFN_TM, pl.cdiv(d_ff, tf)
    tail = d_ff - (n_j - 1) * tf
    assert tail % MXU_DIM == 0 and n_j >= 2
    to_cast = [a if isinstance(a, tuple) else (a, None) for a in to_cast]
    cast_specs = [_cast_specs(a, n_i, n_j, out_stripe) for a, out_stripe in to_cast]
    to_cast = [a for a, _ in to_cast]
    scratch = [pltpu.VMEM((FFN_TM, d), BF16)]
    if prefetch_x:
        scratch += [pltpu.VMEM((FFN_TM, d), F32), pltpu.SemaphoreType.DMA(())]
    outs = pl.pallas_call(
        functools.partial(_ffn_body, final_norm=final_norm, n_cast=len(to_cast), tf=tf, tail=tail,
                          prefetch_x=prefetch_x),
        grid=(n_i, n_j),
        in_specs=[
            pl.BlockSpec(memory_space=pl.ANY) if prefetch_x else pl.BlockSpec((FFN_TM, d), lambda i, j: (i, 0)),
            pl.BlockSpec((1, d), lambda i, j: (0, 0)),
            pl.BlockSpec((d, tf), lambda i, j: (0, j)),
            pl.BlockSpec((d, tf), lambda i, j: (0, j)),
            pl.BlockSpec((tf, d), lambda i, j: (j, 0)),
            pl.BlockSpec((1, d), lambda i, j: (0, 0)),
        ] + [cs[0] for cs in cast_specs],
        out_specs=[pl.BlockSpec((FFN_TM, d), lambda i, j: (i, 0))] + [cs[1] for cs in cast_specs],
        out_shape=[jax.ShapeDtypeStruct((n, d), F32)] + [jax.ShapeDtypeStruct(a.shape, BF16) for a in to_cast],
        scratch_shapes=scratch,
        compiler_params=pltpu.CompilerParams(
            dimension_semantics=("arbitrary" if prefetch_x else "parallel", "arbitrary"),
            vmem_limit_bytes=VMEM_LIMIT_BYTES),
        name="ffn_final" if final_norm else "ffn",
    )(x, g, w_gate, w_up, w_down, gf, *to_cast)
    return outs[0], outs[1:]


COL_Q = 0
COL_KV = D_ATT
COL_U = D_ATT + 6 * D_KV
D_POOL_COLS = 2048
COL_GATES = COL_U + D_POOL_COLS
N_GATES = 3 * N_ATT_HEADS


def _inproj_body(x_ref, g_ref, w_ref, wg_ref, o_ref, og_ref, h_ref):
    j = pl.program_id(1)

    @pl.when(j == 0)
    def _():
        for r in range(0, x_ref.shape[0], NORM_ROWS):
            h_ref[r:r + NORM_ROWS, :] = _rms(x_ref[r:r + NORM_ROWS, :], g_ref[...]).astype(BF16)

    o_ref[...] = _dot_nt(h_ref[...], w_ref[...])

    @pl.when(j == pl.num_programs(1) - 1)
    def _():
        og_ref[...] = _dot_nt(h_ref[...], wg_ref[...])


def _inproj_stripe(b):
    n_qkv, n_g, n_u = COL_U // CAST_ROWS, N_GATES // CAST_ROWS, D_POOL_COLS // CAST_ROWS
    return jnp.where(b < n_qkv, b, jnp.where(b < n_qkv + n_g, b + n_u, b - n_g))


def _inproj(x, g, w_r):
    n, d = x.shape
    assert w_r.shape[0] == COL_GATES + N_GATES and N_GATES <= PROJ_TG
    assert COL_GATES % PROJ_TN == 0 and COL_U % CAST_ROWS == 0 and N_GATES % CAST_ROWS == 0
    wg = jnp.pad(w_r[COL_GATES:], ((0, PROJ_TG - N_GATES), (0, 0)))
    return pl.pallas_call(
        _inproj_body,
        grid=(n // PROJ_TM, COL_GATES // PROJ_TN),
        in_specs=[
            pl.BlockSpec((PROJ_TM, d), lambda i, j: (i, 0)),
            pl.BlockSpec((1, d), lambda i, j: (0, 0)),
            pl.BlockSpec((PROJ_TN, d), lambda i, j: (j, 0)),
            pl.BlockSpec((PROJ_TG, d), lambda i, j: (0, 0)),
        ],
        out_specs=[pl.BlockSpec((PROJ_TM, PROJ_TN), lambda i, j: (i, j)),
                   pl.BlockSpec((PROJ_TM, PROJ_TG), lambda i, j: (i, 0))],
        out_shape=[jax.ShapeDtypeStruct((n, COL_GATES), F32), jax.ShapeDtypeStruct((n, PROJ_TG), F32)],
        scratch_shapes=[pltpu.VMEM((PROJ_TM, d), BF16)],
        compiler_params=pltpu.CompilerParams(
            dimension_semantics=("parallel", "arbitrary"), vmem_limit_bytes=VMEM_LIMIT_BYTES),
        name="in_proj",
    )(x, g, w_r, wg)


def _rel_bucket_np(n):
    max_exact = REL_BUCKETS // 2
    n = np.maximum(n, 0)
    nf = np.maximum(n, 1).astype(np.float32)
    large = max_exact + (np.log(nf / max_exact) / math.log(REL_MAX_DIST / max_exact)
                         * (REL_BUCKETS - max_exact)).astype(np.int32)
    large = np.minimum(large, REL_BUCKETS - 1)
    return np.where(n < max_exact, n, large)


FAR_DIST = int(np.max(np.nonzero(_rel_bucket_np(np.arange(4 * REL_MAX_DIST)) < REL_BUCKETS - 1)[0])) + 1
CMP_ROW_OFF = -(-(FAR_DIST + CMP_BLOCK - 1) // CMP_STRIDE) - 1
TAB_DIAG, TAB_PREV, TAB_FAR, TAB_EDGE = 0, 1, 2, 3


def _pick(rel_table, idx):
    onehot = jnp.asarray(np.arange(REL_BUCKETS)[:, None] == idx[None, :], F32)
    return jnp.einsum("kh,kn->hn", rel_table.astype(F32), onehot, precision=lax.Precision.HIGHEST)


def _group_lanes(t):
    t = t.reshape((N_KV_GROUPS, HEADS_PER_GROUP) + t.shape[1:])
    t = jnp.moveaxis(t, 1, -2)
    return t.reshape(t.shape[:-2] + (HEADS_PER_GROUP * t.shape[-1],))


def _toeplitz_body(p_ref, o_ref):
    blk = o_ref.shape[-1]
    for t in range(o_ref.shape[0]):
        y = pltpu.roll(jnp.broadcast_to(p_ref[t:t + 1, :], (blk, 2 * blk)), 0, 1, stride=1, stride_axis=0)
        o_ref[t] = y[:, :blk]


def _bias_tables(rel_table, n_cmp):
    blk = ATT_BLK
    n_heads = rel_table.shape[1]
    far = REL_BUCKETS - 1
    assert blk >= FAR_DIST and WINDOW % blk == 0
    m = np.arange(2 * blk)
    d = np.where(m < blk, m, m - 2 * blk)
    idx = np.stack([_rel_bucket_np(d), _rel_bucket_np(blk + d), np.full_like(d, far), np.full_like(d, far)])
    mask = np.stack([np.where(d < 0, NEG, 0.0), np.zeros_like(d, np.float64), np.zeros_like(d, np.float64),
                     np.where(d >= 0, NEG, 0.0)]).astype(np.float32)
    profiles = _pick(rel_table, idx.reshape(-1)).reshape(n_heads, 4, 2 * blk) + mask[None]
    sweep = pl.pallas_call(
        _toeplitz_body,
        grid=(N_KV_GROUPS, HEADS_PER_GROUP),
        in_specs=[pl.BlockSpec((None, 4, 2 * blk), lambda g, h: (g * HEADS_PER_GROUP + h, 0, 0))],
        out_specs=pl.BlockSpec((None, 4, blk, blk), lambda g, h: (g, 0, 0, h)),
        out_shape=jax.ShapeDtypeStruct((N_KV_GROUPS, 4, blk, HEADS_PER_GROUP * blk), F32),
        name="bias_tables",
    )(profiles)

    rows = np.arange(n_cmp - CMP_ROW_OFF, n_cmp + (blk - CMP_BLOCK) // CMP_STRIDE + 1)
    dist = np.arange(blk)[None, :] - ((rows[:, None] - n_cmp) * CMP_STRIDE + CMP_BLOCK - 1)
    band = _pick(rel_table, _rel_bucket_np(dist).reshape(-1)).reshape(n_heads, len(rows), blk)
    band = band + np.where(dist < 0, NEG, 0.0).astype(np.float32)[None]
    before = jnp.broadcast_to(rel_table[far].astype(F32)[:, None, None], (n_heads, int(rows[0]), blk))
    after = jnp.full((n_heads, 2 * n_cmp - int(rows[-1]) - 1, blk), NEG, F32)
    cmp_tab = _group_lanes(jnp.concatenate([before, band, after], axis=1))
    return sweep, cmp_tab


def _compress(kv_ref, pos_ref, w1_ref, w2_ref):
    seq, dk = kv_ref.shape
    n_half = seq // CMP_STRIDE
    first = jnp.zeros((n_half, w1_ref.shape[1]), F32)
    second = jnp.zeros((n_half, w1_ref.shape[1]), F32)
    for l in range(CMP_STRIDE):
        rows = kv_ref[pl.ds(l, n_half, stride=CMP_STRIDE), :]
        a = (rows + pos_ref[l:l + 1, :]).astype(BF16)
        b = (rows + pos_ref[CMP_STRIDE + l:CMP_STRIDE + l + 1, :]).astype(BF16)
        first += _dot(a, w1_ref[l * dk:(l + 1) * dk, :])
        second += _dot(b, w1_ref[(CMP_STRIDE + l) * dk:(CMP_STRIDE + l + 1) * dk, :])
    pre = first + pltpu.roll(second, n_half - 1, 0)
    hid = (pre * jax.nn.sigmoid(pre)).astype(BF16)
    return _dot(hid, w2_ref[...])


def _nsa_body(q_ref, kc_ref, vc_ref, ks_ref, vs_ref, kw_ref, vw_ref, gates_ref,
              posk_ref, wk1_ref, wk2_ref, posv_ref, wv1_ref, wv2_ref,
              tab_ref, tabc_ref, overlap_ref, expand_ref,
              o_ref, kcmp_ref, vcmp_t_ref, ksel_ref, vsel_t_ref, kwin_ref, vwin_t_ref, gates_t_ref, s_ref):
    g = pl.program_id(1)
    qi = pl.program_id(2)
    tq = ATT_BLK
    hpg = HEADS_PER_GROUP
    lanes = hpg * tq
    n_cmp = kcmp_ref.shape[0]
    n_sb = overlap_ref.shape[0]
    seq = ks_ref.shape[0]
    assert tq & (tq - 1) == 0 and SEL_BLOCK & (SEL_BLOCK - 1) == 0

    @pl.when(qi == 0)
    def _():
        kcmp_ref[...] = _compress(kc_ref, posk_ref, wk1_ref, wk2_ref).astype(BF16)
        vcmp_t_ref[...] = _compress(vc_ref, posv_ref, wv1_ref, wv2_ref).astype(BF16).T
        for c in range(0, seq, LANES):
            ksel_ref[c:c + LANES, 0:HEAD_DIM] = ks_ref[c:c + LANES, :].astype(BF16)
            ksel_ref[c:c + LANES, HEAD_DIM:] = expand_ref[c:c + LANES, :]
            kwin_ref[c:c + LANES, :] = kw_ref[c:c + LANES, :].astype(BF16)
            vsel_t_ref[:, c:c + LANES] = vs_ref[c:c + LANES, :].T.astype(BF16)
            vwin_t_ref[:, c:c + LANES] = vw_ref[c:c + LANES, :].T.astype(BF16)

    q = q_ref[...] * (HEAD_DIM ** -0.5)
    q_t = jnp.concatenate([q[:, h * HEAD_DIM:(h + 1) * HEAD_DIM].T for h in range(hpg)],
                          axis=1).astype(BF16)
    t_lane = qi * tq + (lax.broadcasted_iota(jnp.int32, (1, lanes), 1) & (tq - 1))

    first_row = pl.multiple_of(n_cmp - qi * (tq // CMP_STRIDE), 8)
    logit = _dot(kcmp_ref[...], q_t) + tabc_ref[pl.ds(first_row, n_cmp), :]
    e = jnp.exp(logit - jnp.max(logit, axis=0, keepdims=True))
    any_valid = (t_lane >= CMP_BLOCK - 1).astype(F32)
    p = e / jnp.sum(e, axis=0, keepdims=True) * any_valid
    o_cmp = _dot(vcmp_t_ref[...], p.astype(BF16))

    psum = p[:, 0:tq]
    for h in range(1, hpg):
        psum = psum + p[:, h * tq:(h + 1) * tq]
    p_hi = psum.astype(BF16)
    r1 = psum - p_hi.astype(F32)
    p_mid = r1.astype(BF16)
    p_lo = (r1 - p_mid.astype(F32)).astype(BF16)
    ov = overlap_ref[...]
    imp = _dot(ov, p_hi) + _dot(ov, p_mid) + _dot(ov, p_lo)
    cur = lax.shift_right_logical(t_lane[:, 0:tq], int(math.log2(SEL_BLOCK)))
    jb = lax.broadcasted_iota(jnp.int32, (n_sb, tq), 0)
    forced = (jb == 0) | (jb == cur) | (jb == cur - 1)
    score = jnp.where(jb > cur, FUTURE_SCORE, imp + jnp.where(forced, FORCE_BONUS, 0.0))
    rank = jnp.zeros((n_sb, tq), jnp.int32)
    for j2 in range(n_sb):
        row = score[j2:j2 + 1, :]
        rank += ((row > score) | ((row == score) & (j2 < jb))).astype(jnp.int32)
    sel_neg = jnp.where(rank < SEL_TOP, 0.0, NEG).astype(BF16)
    n_aug = ksel_ref.shape[1] - HEAD_DIM
    q_sel = jnp.concatenate([q_t, jnp.concatenate([sel_neg] * hpg, axis=1),
                             jnp.zeros((n_aug - n_sb, lanes), BF16)], axis=0)

    def sweep(k_ref, v_t_ref, q_rhs, lo, edge):
        def rows(kb):
            return pl.ds(pl.multiple_of(kb * ATT_BLK, ATT_BLK), ATT_BLK)

        def scores(kb, m):
            dblk = qi - kb
            tab = jnp.minimum(dblk, TAB_FAR)
            if edge is not None:
                tab = jnp.where(dblk == edge, TAB_EDGE, tab)
            s = _dot(k_ref[rows(kb), :], q_rhs) + tab_ref[tab]
            s_ref[rows(kb), :] = s
            return jnp.maximum(m, jnp.max(s, axis=0, keepdims=True))

        def over_chunks(body, init):
            n = qi + 1 - lo
            out = lax.fori_loop(0, n // 2, lambda i, c: body(lo + 2 * i + 1, body(lo + 2 * i, c)), init)
            return lax.cond(n % 2 == 1, lambda c: body(qi, c), lambda c: c, out)

        m = over_chunks(scores, jnp.full((1, lanes), NEG, F32))

        def accumulate(kb, carry):
            l, acc = carry
            pr = jnp.exp(s_ref[rows(kb), :] - m)
            return (l + jnp.sum(pr, axis=0, keepdims=True),
                    acc + _dot(v_t_ref[:, rows(kb)], pr.astype(BF16)))

        l, acc = over_chunks(accumulate, (jnp.zeros((1, lanes), F32), jnp.zeros((HEAD_DIM, lanes), F32)))
        return acc / l

    o_sel = sweep(ksel_ref, vsel_t_ref, q_sel, 0, None)
    n_back = WINDOW // ATT_BLK
    o_win = sweep(kwin_ref, vwin_t_ref, q_t, jnp.maximum(qi - n_back, 0), n_back)

    for c in range(0, tq, LANES):
        gates_t_ref[:, c:c + LANES] = jax.nn.sigmoid(gates_ref[c:c + LANES, :]).T
    for h in range(hpg):
        head = g * hpg + h
        sl = slice(h * tq, (h + 1) * tq)
        y = (gates_t_ref[pl.ds(head, 1), :] * o_cmp[:, sl]
             + gates_t_ref[pl.ds(N_ATT_HEADS + head, 1), :] * o_sel[:, sl]
             + gates_t_ref[pl.ds(2 * N_ATT_HEADS + head, 1), :] * o_win[:, sl])
        for c in range(0, tq, LANES):
            o_ref[c:c + LANES, h * HEAD_DIM:(h + 1) * HEAD_DIM] = y[:, c:c + LANES].T.astype(o_ref.dtype)


def _nsa(proj, gates, batch, seq, cmp_pos_k, w_ck1, w_ck2, cmp_pos_v, w_cv1, w_cv2, rel_table):
    n_qt = seq // ATT_BLK
    hpg = HEADS_PER_GROUP
    n_cmp = seq // CMP_STRIDE
    n_sb = seq // SEL_BLOCK
    n_aug = LANES
    assert CMP_BLOCK == 2 * CMP_STRIDE and n_cmp % 8 == 0 and n_sb % 16 == 0 and n_sb <= n_aug
    assert (ATT_BLK // CMP_STRIDE) % 8 == 0 and n_qt * (ATT_BLK // CMP_STRIDE) <= n_cmp

    sweep_tab, cmp_tab = _bias_tables(rel_table, n_cmp)
    cs = np.arange(n_cmp)[None, :] * CMP_STRIDE
    ss = np.arange(n_sb)[:, None] * SEL_BLOCK
    overlap = (cs < ss + SEL_BLOCK) & (cs + CMP_BLOCK > ss) & (np.arange(n_cmp)[None, :] < n_cmp - 1)
    expand = np.arange(seq)[:, None] // SEL_BLOCK == np.arange(n_aug)[None, :]
    overlap = jnp.asarray(overlap, BF16)
    expand = jnp.asarray(expand, BF16)

    kv_col0 = COL_KV // HEAD_DIM

    def kv_spec(idx):
        return pl.BlockSpec((seq, HEAD_DIM), lambda b, g, qi, idx=idx: (b, kv_col0 + idx * N_KV_GROUPS + g))

    def whole(arr):
        return pl.BlockSpec(arr.shape, lambda b, g, qi, nd=arr.ndim: (0,) * nd)

    in_specs = [
        pl.BlockSpec((ATT_BLK, hpg * HEAD_DIM), lambda b, g, qi: (b * n_qt + qi, g)),
        kv_spec(0), kv_spec(1), kv_spec(2), kv_spec(3), kv_spec(4), kv_spec(5),
        pl.BlockSpec((ATT_BLK, LANES), lambda b, g, qi: (b * n_qt + qi, 0)),
        whole(cmp_pos_k), whole(w_ck1), whole(w_ck2), whole(cmp_pos_v), whole(w_cv1), whole(w_cv2),
        pl.BlockSpec((None,) + sweep_tab.shape[1:], lambda b, g, qi: (g, 0, 0, 0)),
        pl.BlockSpec((None,) + cmp_tab.shape[1:], lambda b, g, qi: (g, 0, 0)),
        whole(overlap), whole(expand),
    ]
    return pl.pallas_call(
        _nsa_body,
        grid=(batch, N_KV_GROUPS, n_qt),
        in_specs=in_specs,
        out_specs=pl.BlockSpec((ATT_BLK, hpg * HEAD_DIM), lambda b, g, qi: (b * n_qt + qi, g)),
        out_shape=jax.ShapeDtypeStruct((batch * seq, D_ATT), BF16),
        scratch_shapes=[pltpu.VMEM((n_cmp, HEAD_DIM), BF16), pltpu.VMEM((HEAD_DIM, n_cmp), BF16),
                        pltpu.VMEM((seq, HEAD_DIM + n_aug), BF16), pltpu.VMEM((HEAD_DIM, seq), BF16),
                        pltpu.VMEM((seq, HEAD_DIM), BF16), pltpu.VMEM((HEAD_DIM, seq), BF16),
                        pltpu.VMEM((LANES, ATT_BLK), F32), pltpu.VMEM((seq, hpg * ATT_BLK), F32)],
        compiler_params=pltpu.CompilerParams(
            dimension_semantics=("parallel", "parallel", "arbitrary"), vmem_limit_bytes=VMEM_LIMIT_BYTES),
        name="nsa",
    )(proj, proj, proj, proj, proj, proj, proj, gates,
      cmp_pos_k, w_ck1.astype(BF16), w_ck2.astype(BF16), cmp_pos_v, w_cv1.astype(BF16), w_cv2.astype(BF16),
      sweep_tab, cmp_tab, overlap, expand)


def _pool_body(u_ref, w_ref, scale_ref, o_ref):
    gi = pl.program_id(1)
    u = u_ref[...]
    seq = u.shape[0]
    t = lax.broadcasted_iota(jnp.int32, (seq, 1), 0)

    def shifted(x, k):
        return jnp.where(t >= k, pltpu.roll(x, k, 0), 0.0)

    sums = [u]
    for n in range(int(math.log2(POOL_WINDOWS[-1]))):
        sums.append(sums[-1] + shifted(sums[-1], 2 ** n))
    wsum = sums[int(math.log2(POOL_WINDOWS[0]))]
    for idx in range(1, len(POOL_WINDOWS)):
        wsum = jnp.where(gi >= idx, sums[int(math.log2(POOL_WINDOWS[idx]))], wsum)
    window = POOL_WINDOWS[0]
    for idx in range(1, len(POOL_WINDOWS)):
        window = jnp.where(gi >= idx, POOL_WINDOWS[idx], window)
    cnt = jnp.minimum(t + 1, window).astype(F32)
    d = (wsum / cnt - u).astype(BF16)
    o_ref[...] = (_dot(d, w_ref[0]) * scale_ref[...]).astype(o_ref.dtype)


def _pool(proj, batch, seq, w_pool, pool_scale):
    n_g, dg, _ = w_pool.shape
    assert all(w == 2 ** int(math.log2(w)) for w in POOL_WINDOWS) and list(POOL_WINDOWS) == sorted(POOL_WINDOWS)
    col0 = COL_U // dg
    return pl.pallas_call(
        _pool_body,
        grid=(batch, n_g),
        in_specs=[
            pl.BlockSpec((seq, dg), lambda b, gi: (b, col0 + gi)),
            pl.BlockSpec((1, dg, dg), lambda b, gi: (gi, 0, 0)),
            pl.BlockSpec((1, dg), lambda b, gi: (0, gi)),
        ],
        out_specs=pl.BlockSpec((seq, dg), lambda b, gi: (b, gi)),
        out_shape=jax.ShapeDtypeStruct((batch * seq, n_g * dg), BF16),
        compiler_params=pltpu.CompilerParams(
            dimension_semantics=("parallel", "parallel"), vmem_limit_bytes=VMEM_LIMIT_BYTES),
        name="pool",
    )(proj, w_pool.astype(BF16), pool_scale.reshape(1, -1))


def _outproj_body(x_ref, ya_ref, yp_ref, wa_ref, wp_ref, o_ref):
    o_ref[...] = x_ref[...] + _dot(ya_ref[...], wa_ref[...]) + _dot(yp_ref[...], wp_ref[...])


def _outproj(x, y_att, y_pool, w_out):
    n, d = x.shape
    da = y_att.shape[1]
    dp = y_pool.shape[1]
    w = w_out
    return pl.pallas_call(
        _outproj_body,
        grid=(d // OUT_TN, n // OUT_TM),
        in_specs=[
            pl.BlockSpec((OUT_TM, OUT_TN), lambda j, i: (i, j)),
            pl.BlockSpec((OUT_TM, da), lambda j, i: (i, 0)),
            pl.BlockSpec((OUT_TM, dp), lambda j, i: (i, 0)),
            pl.BlockSpec((da, OUT_TN), lambda j, i: (0, j)),
            pl.BlockSpec((dp, OUT_TN), lambda j, i: (da // dp, j)),
        ],
        out_specs=pl.BlockSpec((OUT_TM, OUT_TN), lambda j, i: (i, j)),
        out_shape=jax.ShapeDtypeStruct((n, d), F32),
        compiler_params=pltpu.CompilerParams(
            dimension_semantics=("parallel", "parallel"), vmem_limit_bytes=VMEM_LIMIT_BYTES),
        name="out_proj",
    )(x, y_att, y_pool, w, w)


def kernel(x, norm_ffn1, w_ffn1_gate, w_ffn1_up, w_ffn1_down, norm_mix, w_in, cmp_pos_k, w_cmp_k1, w_cmp_k2,
           cmp_pos_v, w_cmp_v1, w_cmp_v2, w_pool, pool_scale, w_out, rel_table, norm_ffn2, w_ffn2_gate,
           w_ffn2_up, w_ffn2_down, norm_final):
    batch, seq, d = x.shape
    depth = norm_ffn1.shape[0]
    xf = x.reshape(batch * seq, d)
    gf = norm_final.reshape(1, d)
    for l in range(depth):
        later = (w_ffn2_gate[l], w_ffn2_up[l], w_ffn2_down[l], (w_in[l].T, _inproj_stripe), w_out[l])
        xf, (w2_gate, w2_up, w2_down, w_in_b, w_out_b) = _ffn(
            xf, norm_ffn1[l].reshape(1, d), w_ffn1_gate[l].astype(BF16), w_ffn1_up[l].astype(BF16),
            w_ffn1_down[l].astype(BF16), gf, final_norm=False, to_cast=later)
        proj, gates = _inproj(xf, norm_mix[l].reshape(1, d), w_in_b)
        y_att = _nsa(proj, gates, batch, seq, cmp_pos_k[l], w_cmp_k1[l], w_cmp_k2[l],
                     cmp_pos_v[l], w_cmp_v1[l], w_cmp_v2[l], rel_table)
        y_pool = _pool(proj, batch, seq, w_pool[l], pool_scale[l])
        xf = _outproj(xf, y_att, y_pool, w_out_b)
        xf, _ = _ffn(xf, norm_ffn2[l].reshape(1, d), w2_gate, w2_up, w2_down, gf, final_norm=(l == depth - 1))
    if depth == 0:
        raise ValueError("depth must be positive")
    return xf.reshape(batch, seq, d)
```

```python
import functools
import math

import jax
import jax.numpy as jnp
import numpy as np
from jax import lax
from jax.experimental import pallas as pl
from jax.experimental.pallas import tpu as pltpu

HEAD_DIM = 128
N_ATT_HEADS = 16
N_KV_GROUPS = 4
HEADS_PER_GROUP = N_ATT_HEADS // N_KV_GROUPS
D_ATT = N_ATT_HEADS * HEAD_DIM
D_KV = N_KV_GROUPS * HEAD_DIM
N_POOL_GROUPS = 4
POOL_WINDOWS = (2, 4, 8, 16)
CMP_BLOCK = 32
CMP_STRIDE = 16
SEL_BLOCK = 64
SEL_TOP = 8
WINDOW = 512
REL_BUCKETS = 32
REL_MAX_DIST = 128
EPS = 1e-6
NEG = -1e30
FORCE_BONUS = 1e4
FUTURE_SCORE = -1e9

LANES = 128
MXU_DIM = 256
VMEM_LIMIT_BYTES = 58 * 2 ** 20

FFN_TM = 512
FFN_TF = MXU_DIM
FFN_TF_WIDE = 2 * MXU_DIM
FFN_TN = 1024
NORM_ROWS = 32
CAST_ROWS = 16
PROJ_TM = 1024
PROJ_ROWS = 512
PROJ_TN = 1024
PROJ_TG = MXU_DIM
ATT_BLK = 256
OUT_TM = 1024
OUT_TN = 1024

BF16 = jnp.bfloat16
F32 = jnp.float32


def _dot(a, b):
    return jnp.dot(a, b, preferred_element_type=F32)


def _dot_nt(a, b):
    return lax.dot_general(a, b, (((1,), (1,)), ((), ())), preferred_element_type=F32)


def _rms(x, g):
    return x * lax.rsqrt(jnp.mean(x * x, axis=-1, keepdims=True) + EPS) * g


def _prefetch_row_tile(x_hbm, xbuf_ref, sem):
    i = pl.program_id(0)
    j = pl.program_id(1)
    tm = xbuf_ref.shape[0]

    def x_copy(tile):
        return pltpu.make_async_copy(x_hbm.at[pl.ds(pl.multiple_of(tile * tm, tm), tm), :], xbuf_ref, sem)

    @pl.when((i == 0) & (j == 0))
    def _():
        x_copy(0).start()

    @pl.when(j == 0)
    def _():
        x_copy(i).wait()

    @pl.when((j == pl.num_programs(1) - 2) & (i + 1 < pl.num_programs(0)))
    def _():
        x_copy(i + 1).start()


def _ffn_body(*refs, final_norm, n_cast, tf, tail, prefetch_x):
    x_ref, g_ref, wg_ref, wu_ref, wd_ref, gf_ref = refs[:6]
    cast_in = refs[6:6 + n_cast]
    o_ref = refs[6 + n_cast]
    cast_out = refs[7 + n_cast:7 + 2 * n_cast]
    scratch = refs[7 + 2 * n_cast:]
    h_ref = scratch[0]
    j = pl.program_id(1)
    last = pl.num_programs(1) - 1
    tm = o_ref.shape[0]

    if prefetch_x:
        x_tile = scratch[1]
        _prefetch_row_tile(x_ref, *scratch[1:])
    else:
        x_tile = x_ref

    @pl.when(j == 0)
    def _():
        for r in range(0, tm, NORM_ROWS):
            x = x_tile[r:r + NORM_ROWS, :]
            h_ref[r:r + NORM_ROWS, :] = _rms(x, g_ref[...]).astype(BF16)
            o_ref[r:r + NORM_ROWS, :] = x

    def accumulate(width):
        h = h_ref[...]
        gate = _dot(h, wg_ref[:, :width])
        up = _dot(h, wu_ref[:, :width])
        act = (0.5 * (gate * jax.nn.sigmoid(gate)) * up).astype(BF16)
        for c in range(0, o_ref.shape[1], FFN_TN):
            o_ref[:, c:c + FFN_TN] += _dot(act, wd_ref[:width, c:c + FFN_TN])

    if tail == tf:
        accumulate(tf)
    else:
        pl.when(j < last)(functools.partial(accumulate, tf))
        pl.when(j == last)(functools.partial(accumulate, tail))

    if final_norm:
        @pl.when(j == last)
        def _():
            for r in range(0, tm, NORM_ROWS):
                o_ref[r:r + NORM_ROWS, :] = _rms(o_ref[r:r + NORM_ROWS, :], gf_ref[...])

    for src, dst in zip(cast_in, cast_out):
        dst[...] = src[...].astype(BF16)


def _cast_specs(a, n_i, n_j, out_stripe=None):
    r, c = a.shape
    if out_stripe is None:
        if r % n_i == 0 and c % n_j == 0 and (r // n_i) % CAST_ROWS == 0 and (c // n_j) % LANES == 0:
            return (pl.BlockSpec((r // n_i, c // n_j), lambda i, j: (i, j)),) * 2
        if r % n_j == 0 and c % n_i == 0 and (r // n_j) % CAST_ROWS == 0 and (c // n_i) % LANES == 0:
            return (pl.BlockSpec((r // n_j, c // n_i), lambda i, j: (j, i)),) * 2
        out_stripe = lambda b: b
    n_blocks, rem = divmod(r, CAST_ROWS)
    assert rem == 0 and n_blocks <= n_i * n_j
    stripe = lambda i, j: jnp.minimum(i * n_j + j, n_blocks - 1)
    return (pl.BlockSpec((CAST_ROWS, c), lambda i, j: (stripe(i, j), 0)),
            pl.BlockSpec((CAST_ROWS, c), lambda i, j: (out_stripe(stripe(i, j)), 0)))


def _ffn(x, g, w_gate, w_up, w_down, gf, *, final_norm, to_cast=()):
    n, d = x.shape
    d_ff = w_gate.shape[1]
    prefetch_x = not to_cast
    tf = FFN_TF_WIDE if prefetch_x else FFN_TF
    n_i, n_j = n // FFN_TM, pl.cdiv(d_ff, tf)
    tail = d_ff - (n_j - 1) * tf
    assert tail % MXU_DIM == 0 and n_j >= 2
    to_cast = [a if isinstance(a, tuple) else (a, None) for a in to_cast]
    cast_specs = [_cast_specs(a, n_i, n_j, out_stripe) for a, out_stripe in to_cast]
    to_cast = [a for a, _ in to_cast]
    scratch = [pltpu.VMEM((FFN_TM, d), BF16)]
    if prefetch_x:
        scratch += [pltpu.VMEM((FFN_TM, d), F32), pltpu.SemaphoreType.DMA(())]
    outs = pl.pallas_call(
        functools.partial(_ffn_body, final_norm=final_norm, n_cast=len(to_cast), tf=tf, tail=tail,
                          prefetch_x=prefetch_x),
        grid=(n_i, n_j),
        in_specs=[
            pl.BlockSpec(memory_space=pl.ANY) if prefetch_x else pl.BlockSpec((FFN_TM, d), lambda i, j: (i, 0)),
            pl.BlockSpec((1, d), lambda i, j: (0, 0)),
            pl.BlockSpec((d, tf), lambda i, j: (0, j)),
            pl.BlockSpec((d, tf), lambda i, j: (0, j)),
            pl.BlockSpec((tf, d), lambda i, j: (j, 0)),
            pl.BlockSpec((1, d), lambda i, j: (0, 0)),
        ] + [cs[0] for cs in cast_specs],
        out_specs=[pl.BlockSpec((FFN_TM, d), lambda i, j: (i, 0))] + [cs[1] for cs in cast_specs],
        out_shape=[jax.ShapeDtypeStruct((n, d), F32)] + [jax.ShapeDtypeStruct(a.shape, BF16) for a in to_cast],
        scratch_shapes=scratch,
        compiler_params=pltpu.CompilerParams(
            dimension_semantics=("arbitrary" if prefetch_x else "parallel", "arbitrary"),
            vmem_limit_bytes=VMEM_LIMIT_BYTES),
        name="ffn_final" if final_norm else "ffn",
    )(x, g, w_gate, w_up, w_down, gf, *to_cast)
    return outs[0], outs[1:]


COL_Q = 0
COL_KV = D_ATT
COL_U = D_ATT + 6 * D_KV
D_POOL_COLS = 2048
COL_GATES = COL_U + D_POOL_COLS
N_GATES = 3 * N_ATT_HEADS


def _inproj_body(x_hbm, g_ref, w_ref, wg_ref, o_ref, og_ref, h_ref, xbuf_ref, sem):
    j = pl.program_id(1)
    _prefetch_row_tile(x_hbm, xbuf_ref, sem)

    @pl.when(j == 0)
    def _():
        for r in range(0, xbuf_ref.shape[0], NORM_ROWS):
            h_ref[r:r + NORM_ROWS, :] = _rms(xbuf_ref[r:r + NORM_ROWS, :], g_ref[...]).astype(BF16)

    for r in range(0, o_ref.shape[0], PROJ_ROWS):
        o_ref[r:r + PROJ_ROWS, :] = _dot_nt(h_ref[r:r + PROJ_ROWS, :], w_ref[...])

    @pl.when(j == pl.num_programs(1) - 1)
    def _():
        og_ref[...] = _dot_nt(h_ref[...], wg_ref[...])


def _inproj_stripe(b):
    n_qkv, n_g, n_u = COL_U // CAST_ROWS, N_GATES // CAST_ROWS, D_POOL_COLS // CAST_ROWS
    return jnp.where(b < n_qkv, b, jnp.where(b < n_qkv + n_g, b + n_u, b - n_g))


def _inproj(x, g, w_r):
    n, d = x.shape
    assert w_r.shape[0] == COL_GATES + N_GATES and N_GATES <= PROJ_TG
    assert COL_GATES % PROJ_TN == 0 and COL_U % CAST_ROWS == 0 and N_GATES % CAST_ROWS == 0
    wg = jnp.pad(w_r[COL_GATES:], ((0, PROJ_TG - N_GATES), (0, 0)))
    return pl.pallas_call(
        _inproj_body,
        grid=(n // PROJ_TM, COL_GATES // PROJ_TN),
        in_specs=[
            pl.BlockSpec(memory_space=pl.ANY),
            pl.BlockSpec((1, d), lambda i, j: (0, 0)),
            pl.BlockSpec((PROJ_TN, d), lambda i, j: (j, 0)),
            pl.BlockSpec((PROJ_TG, d), lambda i, j: (0, 0), pipeline_mode=pl.Buffered(1)),
        ],
        out_specs=[pl.BlockSpec((PROJ_TM, PROJ_TN), lambda i, j: (i, j)),
                   pl.BlockSpec((PROJ_TM, PROJ_TG), lambda i, j: (i, 0))],
        out_shape=[jax.ShapeDtypeStruct((n, COL_GATES), F32), jax.ShapeDtypeStruct((n, PROJ_TG), F32)],
        scratch_shapes=[pltpu.VMEM((PROJ_TM, d), BF16), pltpu.VMEM((PROJ_TM, d), F32), pltpu.SemaphoreType.DMA(())],
        compiler_params=pltpu.CompilerParams(
            dimension_semantics=("arbitrary", "arbitrary"), vmem_limit_bytes=VMEM_LIMIT_BYTES),
        name="in_proj",
    )(x, g, w_r, wg)


def _rel_bucket_np(n):
    max_exact = REL_BUCKETS // 2
    n = np.maximum(n, 0)
    nf = np.maximum(n, 1).astype(np.float32)
    large = max_exact + (np.log(nf / max_exact) / math.log(REL_MAX_DIST / max_exact)
                         * (REL_BUCKETS - max_exact)).astype(np.int32)
    large = np.minimum(large, REL_BUCKETS - 1)
    return np.where(n < max_exact, n, large)


FAR_DIST = int(np.max(np.nonzero(_rel_bucket_np(np.arange(4 * REL_MAX_DIST)) < REL_BUCKETS - 1)[0])) + 1
CMP_ROW_OFF = -(-(FAR_DIST + CMP_BLOCK - 1) // CMP_STRIDE) - 1
TAB_DIAG, TAB_PREV, TAB_FAR, TAB_EDGE = 0, 1, 2, 3


def _pick(rel_table, idx):
    onehot = jnp.asarray(np.arange(REL_BUCKETS)[:, None] == idx[None, :], F32)
    return jnp.einsum("kh,kn->hn", rel_table.astype(F32), onehot, precision=lax.Precision.HIGHEST)


def _group_lanes(t):
    t = t.reshape((N_KV_GROUPS, HEADS_PER_GROUP) + t.shape[1:])
    t = jnp.moveaxis(t, 1, -2)
    return t.reshape(t.shape[:-2] + (HEADS_PER_GROUP * t.shape[-1],))


def _toeplitz_body(p_ref, o_ref):
    blk = o_ref.shape[-1]
    for t in range(o_ref.shape[0]):
        y = pltpu.roll(jnp.broadcast_to(p_ref[t:t + 1, :], (blk, 2 * blk)), 0, 1, stride=1, stride_axis=0)
        o_ref[t] = y[:, :blk]


def _bias_tables(rel_table, n_cmp):
    blk = ATT_BLK
    n_heads = rel_table.shape[1]
    far = REL_BUCKETS - 1
    assert blk >= FAR_DIST and WINDOW % blk == 0
    m = np.arange(2 * blk)
    d = np.where(m < blk, m, m - 2 * blk)
    idx = np.stack([_rel_bucket_np(d), _rel_bucket_np(blk + d), np.full_like(d, far), np.full_like(d, far)])
    mask = np.stack([np.where(d < 0, NEG, 0.0), np.zeros_like(d, np.float64), np.zeros_like(d, np.float64),
                     np.where(d >= 0, NEG, 0.0)]).astype(np.float32)
    profiles = _pick(rel_table, idx.reshape(-1)).reshape(n_heads, 4, 2 * blk) + mask[None]
    sweep = pl.pallas_call(
        _toeplitz_body,
        grid=(N_KV_GROUPS, HEADS_PER_GROUP),
        in_specs=[pl.BlockSpec((None, 4, 2 * blk), lambda g, h: (g * HEADS_PER_GROUP + h, 0, 0))],
        out_specs=pl.BlockSpec((None, 4, blk, blk), lambda g, h: (g, 0, 0, h)),
        out_shape=jax.ShapeDtypeStruct((N_KV_GROUPS, 4, blk, HEADS_PER_GROUP * blk), F32),
        name="bias_tables",
    )(profiles)

    rows = np.arange(n_cmp - CMP_ROW_OFF, n_cmp + (blk - CMP_BLOCK) // CMP_STRIDE + 1)
    dist = np.arange(blk)[None, :] - ((rows[:, None] - n_cmp) * CMP_STRIDE + CMP_BLOCK - 1)
    band = _pick(rel_table, _rel_bucket_np(dist).reshape(-1)).reshape(n_heads, len(rows), blk)
    band = band + np.where(dist < 0, NEG, 0.0).astype(np.float32)[None]
    before = jnp.broadcast_to(rel_table[far].astype(F32)[:, None, None], (n_heads, int(rows[0]), blk))
    after = jnp.full((n_heads, 2 * n_cmp - int(rows[-1]) - 1, blk), NEG, F32)
    cmp_tab = _group_lanes(jnp.concatenate([before, band, after], axis=1))
    return sweep, cmp_tab


def _compress(kv_ref, pos_ref, w1_ref, w2_ref):
    seq, dk = kv_ref.shape
    n_half = seq // CMP_STRIDE
    first = jnp.zeros((n_half, w1_ref.shape[1]), F32)
    second = jnp.zeros((n_half, w1_ref.shape[1]), F32)
    for l in range(CMP_STRIDE):
        rows = kv_ref[pl.ds(l, n_half, stride=CMP_STRIDE), :]
        a = (rows + pos_ref[l:l + 1, :]).astype(BF16)
        b = (rows + pos_ref[CMP_STRIDE + l:CMP_STRIDE + l + 1, :]).astype(BF16)
        first += _dot(a, w1_ref[l * dk:(l + 1) * dk, :])
        second += _dot(b, w1_ref[(CMP_STRIDE + l) * dk:(CMP_STRIDE + l + 1) * dk, :])
    pre = first + pltpu.roll(second, n_half - 1, 0)
    hid = (pre * jax.nn.sigmoid(pre)).astype(BF16)
    return _dot(hid, w2_ref[...])


def _nsa_body(q_ref, kc_ref, vc_ref, ks_ref, vs_ref, kw_ref, vw_ref, gates_ref,
              posk_ref, wk1_ref, wk2_ref, posv_ref, wv1_ref, wv2_ref,
              tab_ref, tabc_ref, overlap_ref, expand_ref,
              o_ref, kcmp_ref, vcmp_t_ref, ksel_ref, vsel_t_ref, kwin_ref, vwin_t_ref, gates_t_ref, s_ref):
    g = pl.program_id(1)
    qi = pl.program_id(2)
    tq = ATT_BLK
    hpg = HEADS_PER_GROUP
    lanes = hpg * tq
    n_cmp = kcmp_ref.shape[0]
    n_sb = overlap_ref.shape[0]
    seq = ks_ref.shape[0]
    assert tq & (tq - 1) == 0 and SEL_BLOCK & (SEL_BLOCK - 1) == 0

    @pl.when(qi == 0)
    def _():
        kcmp_ref[...] = _compress(kc_ref, posk_ref, wk1_ref, wk2_ref).astype(BF16)
        vcmp_t_ref[...] = _compress(vc_ref, posv_ref, wv1_ref, wv2_ref).astype(BF16).T
        for c in range(0, seq, LANES):
            ksel_ref[c:c + LANES, 0:HEAD_DIM] = ks_ref[c:c + LANES, :].astype(BF16)
            ksel_ref[c:c + LANES, HEAD_DIM:] = expand_ref[c:c + LANES, :]
            kwin_ref[c:c + LANES, :] = kw_ref[c:c + LANES, :].astype(BF16)
            vsel_t_ref[:, c:c + LANES] = vs_ref[c:c + LANES, :].T.astype(BF16)
            vwin_t_ref[:, c:c + LANES] = vw_ref[c:c + LANES, :].T.astype(BF16)

    q = q_ref[...] * (HEAD_DIM ** -0.5)
    q_t = jnp.concatenate([q[:, h * HEAD_DIM:(h + 1) * HEAD_DIM].T for h in range(hpg)],
                          axis=1).astype(BF16)
    t_lane = qi * tq + (lax.broadcasted_iota(jnp.int32, (1, lanes), 1) & (tq - 1))

    first_row = pl.multiple_of(n_cmp - qi * (tq // CMP_STRIDE), 8)
    logit = _dot(kcmp_ref[...], q_t) + tabc_ref[pl.ds(first_row, n_cmp), :]
    e = jnp.exp(logit - jnp.max(logit, axis=0, keepdims=True))
    any_valid = (t_lane >= CMP_BLOCK - 1).astype(F32)
    p = e / jnp.sum(e, axis=0, keepdims=True) * any_valid
    o_cmp = _dot(vcmp_t_ref[...], p.astype(BF16))

    psum = p[:, 0:tq]
    for h in range(1, hpg):
        psum = psum + p[:, h * tq:(h + 1) * tq]
    p_hi = psum.astype(BF16)
    r1 = psum - p_hi.astype(F32)
    p_mid = r1.astype(BF16)
    p_lo = (r1 - p_mid.astype(F32)).astype(BF16)
    ov = overlap_ref[...]
    imp = _dot(ov, p_hi) + _dot(ov, p_mid) + _dot(ov, p_lo)
    cur = lax.shift_right_logical(t_lane[:, 0:tq], int(math.log2(SEL_BLOCK)))
    jb = lax.broadcasted_iota(jnp.int32, (n_sb, tq), 0)
    forced = (jb == 0) | (jb == cur) | (jb == cur - 1)
    score = jnp.where(jb > cur, FUTURE_SCORE, imp + jnp.where(forced, FORCE_BONUS, 0.0))
    rank = jnp.zeros((n_sb, tq), jnp.int32)
    for j2 in range(n_sb):
        row = score[j2:j2 + 1, :]
        rank += ((row > score) | ((row == score) & (j2 < jb))).astype(jnp.int32)
    sel_neg = jnp.where(rank < SEL_TOP, 0.0, NEG).astype(BF16)
    n_aug = ksel_ref.shape[1] - HEAD_DIM
    q_sel = jnp.concatenate([q_t, jnp.concatenate([sel_neg] * hpg, axis=1),
                             jnp.zeros((n_aug - n_sb, lanes), BF16)], axis=0)

    def sweep(k_ref, v_t_ref, q_rhs, lo, edge):
        def rows(kb):
            return pl.ds(pl.multiple_of(kb * ATT_BLK, ATT_BLK), ATT_BLK)

        def scores(kb, m):
            dblk = qi - kb
            tab = jnp.minimum(dblk, TAB_FAR)
            if edge is not None:
                tab = jnp.where(dblk == edge, TAB_EDGE, tab)
            s = _dot(k_ref[rows(kb), :], q_rhs) + tab_ref[tab]
            s_ref[rows(kb), :] = s
            return jnp.maximum(m, jnp.max(s, axis=0, keepdims=True))

        def over_chunks(body, init):
            n = qi + 1 - lo
            out = lax.fori_loop(0, n // 2, lambda i, c: body(lo + 2 * i + 1, body(lo + 2 * i, c)), init)
            return lax.cond(n % 2 == 1, lambda c: body(qi, c), lambda c: c, out)

        m = over_chunks(scores, jnp.full((1, lanes), NEG, F32))

        def accumulate(kb, carry):
            l, acc = carry
            pr = jnp.exp(s_ref[rows(kb), :] - m)
            return (l + jnp.sum(pr, axis=0, keepdims=True),
                    acc + _dot(v_t_ref[:, rows(kb)], pr.astype(BF16)))

        l, acc = over_chunks(accumulate, (jnp.zeros((1, lanes), F32), jnp.zeros((HEAD_DIM, lanes), F32)))
        return acc / l

    o_sel = sweep(ksel_ref, vsel_t_ref, q_sel, 0, None)
    n_back = WINDOW // ATT_BLK
    o_win = sweep(kwin_ref, vwin_t_ref, q_t, jnp.maximum(qi - n_back, 0), n_back)

    for c in range(0, tq, LANES):
        gates_t_ref[:, c:c + LANES] = jax.nn.sigmoid(gates_ref[c:c + LANES, :]).T
    for h in range(hpg):
        head = g * hpg + h
        sl = slice(h * tq, (h + 1) * tq)
        y = (gates_t_ref[pl.ds(head, 1), :] * o_cmp[:, sl]
             + gates_t_ref[pl.ds(N_ATT_HEADS + head, 1), :] * o_sel[:, sl]
             + gates_t_ref[pl.ds(2 * N_ATT_HEADS + head, 1), :] * o_win[:, sl])
        for c in range(0, tq, LANES):
            o_ref[c:c + LANES, h * HEAD_DIM:(h + 1) * HEAD_DIM] = y[:, c:c + LANES].T.astype(o_ref.dtype)


def _nsa(proj, gates, batch, seq, cmp_pos_k, w_ck1, w_ck2, cmp_pos_v, w_cv1, w_cv2, rel_table):
    n_qt = seq // ATT_BLK
    hpg = HEADS_PER_GROUP
    n_cmp = seq // CMP_STRIDE
    n_sb = seq // SEL_BLOCK
    n_aug = LANES
    assert CMP_BLOCK == 2 * CMP_STRIDE and n_cmp % 8 == 0 and n_sb % 16 == 0 and n_sb <= n_aug
    assert (ATT_BLK // CMP_STRIDE) % 8 == 0 and n_qt * (ATT_BLK // CMP_STRIDE) <= n_cmp

    sweep_tab, cmp_tab = _bias_tables(rel_table, n_cmp)
    cs = np.arange(n_cmp)[None, :] * CMP_STRIDE
    ss = np.arange(n_sb)[:, None] * SEL_BLOCK
    overlap = (cs < ss + SEL_BLOCK) & (cs + CMP_BLOCK > ss) & (np.arange(n_cmp)[None, :] < n_cmp - 1)
    expand = np.arange(seq)[:, None] // SEL_BLOCK == np.arange(n_aug)[None, :]
    overlap = jnp.asarray(overlap, BF16)
    expand = jnp.asarray(expand, BF16)

    kv_col0 = COL_KV // HEAD_DIM

    def kv_spec(idx):
        return pl.BlockSpec((seq, HEAD_DIM), lambda b, g, qi, idx=idx: (b, kv_col0 + idx * N_KV_GROUPS + g))

    def whole(arr):
        return pl.BlockSpec(arr.shape, lambda b, g, qi, nd=arr.ndim: (0,) * nd)

    in_specs = [
        pl.BlockSpec((ATT_BLK, hpg * HEAD_DIM), lambda b, g, qi: (b * n_qt + qi, g)),
        kv_spec(0), kv_spec(1), kv_spec(2), kv_spec(3), kv_spec(4), kv_spec(5),
        pl.BlockSpec((ATT_BLK, LANES), lambda b, g, qi: (b * n_qt + qi, 0)),
        whole(cmp_pos_k), whole(w_ck1), whole(w_ck2), whole(cmp_pos_v), whole(w_cv1), whole(w_cv2),
        pl.BlockSpec((None,) + sweep_tab.shape[1:], lambda b, g, qi: (g, 0, 0, 0)),
        pl.BlockSpec((None,) + cmp_tab.shape[1:], lambda b, g, qi: (g, 0, 0)),
        whole(overlap), whole(expand),
    ]
    return pl.pallas_call(
        _nsa_body,
        grid=(batch, N_KV_GROUPS, n_qt),
        in_specs=in_specs,
        out_specs=pl.BlockSpec((ATT_BLK, hpg * HEAD_DIM), lambda b, g, qi: (b * n_qt + qi, g)),
        out_shape=jax.ShapeDtypeStruct((batch * seq, D_ATT), BF16),
        scratch_shapes=[pltpu.VMEM((n_cmp, HEAD_DIM), BF16), pltpu.VMEM((HEAD_DIM, n_cmp), BF16),
                        pltpu.VMEM((seq, HEAD_DIM + n_aug), BF16), pltpu.VMEM((HEAD_DIM, seq), BF16),
                        pltpu.VMEM((seq, HEAD_DIM), BF16), pltpu.VMEM((HEAD_DIM, seq), BF16),
                        pltpu.VMEM((LANES, ATT_BLK), F32), pltpu.VMEM((seq, hpg * ATT_BLK), F32)],
        compiler_params=pltpu.CompilerParams(
            dimension_semantics=("parallel", "parallel", "arbitrary"), vmem_limit_bytes=VMEM_LIMIT_BYTES),
        name="nsa",
    )(proj, proj, proj, proj, proj, proj, proj, gates,
      cmp_pos_k, w_ck1.astype(BF16), w_ck2.astype(BF16), cmp_pos_v, w_cv1.astype(BF16), w_cv2.astype(BF16),
      sweep_tab, cmp_tab, overlap, expand)


def _pool_body(u_ref, w_ref, scale_ref, o_ref):
    gi = pl.program_id(1)
    u = u_ref[...]
    seq = u.shape[0]
    t = lax.broadcasted_iota(jnp.int32, (seq, 1), 0)

    def shifted(x, k):
        return jnp.where(t >= k, pltpu.roll(x, k, 0), 0.0)

    sums = [u]
    for n in range(int(math.log2(POOL_WINDOWS[-1]))):
        sums.append(sums[-1] + shifted(sums[-1], 2 ** n))
    wsum = sums[int(math.log2(POOL_WINDOWS[0]))]
    for idx in range(1, len(POOL_WINDOWS)):
        wsum = jnp.where(gi >= idx, sums[int(math.log2(POOL_WINDOWS[idx]))], wsum)
    window = POOL_WINDOWS[0]
    for idx in range(1, len(POOL_WINDOWS)):
        window = jnp.where(gi >= idx, POOL_WINDOWS[idx], window)
    cnt = jnp.minimum(t + 1, window).astype(F32)
    d = (wsum / cnt - u).astype(BF16)
    o_ref[...] = (_dot(d, w_ref[0]) * scale_ref[...]).astype(o_ref.dtype)


def _pool(proj, batch, seq, w_pool, pool_scale):
    n_g, dg, _ = w_pool.shape
    assert all(w == 2 ** int(math.log2(w)) for w in POOL_WINDOWS) and list(POOL_WINDOWS) == sorted(POOL_WINDOWS)
    col0 = COL_U // dg
    return pl.pallas_call(
        _pool_body,
        grid=(batch, n_g),
        in_specs=[
            pl.BlockSpec((seq, dg), lambda b, gi: (b, col0 + gi)),
            pl.BlockSpec((1, dg, dg), lambda b, gi: (gi, 0, 0)),
            pl.BlockSpec((1, dg), lambda b, gi: (0, gi)),
        ],
        out_specs=pl.BlockSpec((seq, dg), lambda b, gi: (b, gi)),
        out_shape=jax.ShapeDtypeStruct((batch * seq, n_g * dg), BF16),
        compiler_params=pltpu.CompilerParams(
            dimension_semantics=("parallel", "parallel"), vmem_limit_bytes=VMEM_LIMIT_BYTES),
        name="pool",
    )(proj, w_pool.astype(BF16), pool_scale.reshape(1, -1))


def _outproj_body(x_ref, ya_ref, yp_ref, wa_ref, wp_ref, o_ref):
    o_ref[...] = x_ref[...] + _dot(ya_ref[...], wa_ref[...]) + _dot(yp_ref[...], wp_ref[...])


def _outproj(x, y_att, y_pool, w_out):
    n, d = x.shape
    da = y_att.shape[1]
    dp = y_pool.shape[1]
    w = w_out
    return pl.pallas_call(
        _outproj_body,
        grid=(d // OUT_TN, n // OUT_TM),
        in_specs=[
            pl.BlockSpec((OUT_TM, OUT_TN), lambda j, i: (i, j)),
            pl.BlockSpec((OUT_TM, da), lambda j, i: (i, 0)),
            pl.BlockSpec((OUT_TM, dp), lambda j, i: (i, 0)),
            pl.BlockSpec((da, OUT_TN), lambda j, i: (0, j)),
            pl.BlockSpec((dp, OUT_TN), lambda j, i: (da // dp, j)),
        ],
        out_specs=pl.BlockSpec((OUT_TM, OUT_TN), lambda j, i: (i, j)),
        out_shape=jax.ShapeDtypeStruct((n, d), F32),
        compiler_params=pltpu.CompilerParams(
            dimension_semantics=("parallel", "parallel"), vmem_limit_bytes=VMEM_LIMIT_BYTES),
        name="out_proj",
    )(x, y_att, y_pool, w, w)


def kernel(x, norm_ffn1, w_ffn1_gate, w_ffn1_up, w_ffn1_down, norm_mix, w_in, cmp_pos_k, w_cmp_k1, w_cmp_k2,
           cmp_pos_v, w_cmp_v1, w_cmp_v2, w_pool, pool_scale, w_out, rel_table, norm_ffn2, w_ffn2_gate,
           w_ffn2_up, w_ffn2_down, norm_final):
    batch, seq, d = x.shape
    depth = norm_ffn1.shape[0]
    xf = x.reshape(batch * seq, d)
    gf = norm_final.reshape(1, d)
    for l in range(depth):
        later = (w_ffn2_gate[l], w_ffn2_up[l], w_ffn2_down[l], (w_in[l].T, _inproj_stripe), w_out[l])
        xf, (w2_gate, w2_up, w2_down, w_in_b, w_out_b) = _ffn(
            xf, norm_ffn1[l].reshape(1, d), w_ffn1_gate[l].astype(BF16), w_ffn1_up[l].astype(BF16),
            w_ffn1_down[l].astype(BF16), gf, final_norm=False, to_cast=later)
        proj, gates = _inproj(xf, norm_mix[l].reshape(1, d), w_in_b)
        y_att = _nsa(proj, gates, batch, seq, cmp_pos_k[l], w_cmp_k1[l], w_cmp_k2[l],
                     cmp_pos_v[l], w_cmp_v1[l], w_cmp_v2[l], rel_table)
        y_pool = _pool(proj, batch, seq, w_pool[l], pool_scale[l])
        xf = _outproj(xf, y_att, y_pool, w_out_b)
        xf, _ = _ffn(xf, norm_ffn2[l].reshape(1, d), w2_gate, w2_up, w2_down, gf, final_norm=(l == depth - 1))
    if depth == 0:
        raise ValueError("depth must be positive")
    return xf.reshape(batch, seq, d)
```

```python
import functools
import math

import jax
import jax.numpy as jnp
import numpy as np
from jax import lax
from jax.experimental import pallas as pl
from jax.experimental.pallas import tpu as pltpu

HEAD_DIM = 128
N_ATT_HEADS = 16
N_KV_GROUPS = 4
HEADS_PER_GROUP = N_ATT_HEADS // N_KV_GROUPS
D_ATT = N_ATT_HEADS * HEAD_DIM
D_KV = N_KV_GROUPS * HEAD_DIM
N_POOL_GROUPS = 4
POOL_WINDOWS = (2, 4, 8, 16)
CMP_BLOCK = 32
CMP_STRIDE = 16
SEL_BLOCK = 64
SEL_TOP = 8
WINDOW = 512
REL_BUCKETS = 32
REL_MAX_DIST = 128
EPS = 1e-6
NEG = -1e30
FORCE_BONUS = 1e4
FUTURE_SCORE = -1e9

LANES = 128
MXU_DIM = 256
VMEM_LIMIT_BYTES = 58 * 2 ** 20

FFN_TM = 512
FFN_TF = MXU_DIM
FFN_TF_WIDE = 2 * MXU_DIM
FFN_TN = 1024
NORM_ROWS = 32
CAST_ROWS = 16
PROJ_TM = 1024
PROJ_ROWS = 512
PROJ_TN = 1024
PROJ_TG = MXU_DIM
ATT_BLK = 256
OUT_TM = 1024
OUT_TN = 1024

BF16 = jnp.bfloat16
F32 = jnp.float32


def _dot(a, b):
    return jnp.dot(a, b, preferred_element_type=F32)


def _dot_nt(a, b):
    return lax.dot_general(a, b, (((1,), (1,)), ((), ())), preferred_element_type=F32)


def _rms(x, g):
    return x * lax.rsqrt(jnp.mean(x * x, axis=-1, keepdims=True) + EPS) * g


def _prefetch_row_tile(x_hbm, xbuf_ref, sem):
    i = pl.program_id(0)
    j = pl.program_id(1)
    tm = xbuf_ref.shape[0]

    def x_copy(tile):
        return pltpu.make_async_copy(x_hbm.at[pl.ds(pl.multiple_of(tile * tm, tm), tm), :], xbuf_ref, sem)

    @pl.when((i == 0) & (j == 0))
    def _():
        x_copy(0).start()

    @pl.when(j == 0)
    def _():
        x_copy(i).wait()

    @pl.when((j == pl.num_programs(1) - 2) & (i + 1 < pl.num_programs(0)))
    def _():
        x_copy(i + 1).start()


def _ffn_body(*refs, final_norm, n_cast, tf, tail, prefetch_x):
    x_ref, g_ref, wg_ref, wu_ref, wd_ref, gf_ref = refs[:6]
    cast_in = refs[6:6 + n_cast]
    o_ref = refs[6 + n_cast]
    cast_out = refs[7 + n_cast:7 + 2 * n_cast]
    scratch = refs[7 + 2 * n_cast:]
    h_ref = scratch[0]
    j = pl.program_id(1)
    last = pl.num_programs(1) - 1
    tm = o_ref.shape[0]

    if prefetch_x:
        x_tile = scratch[1]
        _prefetch_row_tile(x_ref, *scratch[1:])
    else:
        x_tile = x_ref

    @pl.when(j == 0)
    def _():
        for r in range(0, tm, NORM_ROWS):
            x = x_tile[r:r + NORM_ROWS, :]
            h_ref[r:r + NORM_ROWS, :] = _rms(x, g_ref[...]).astype(BF16)
            o_ref[r:r + NORM_ROWS, :] = x

    def accumulate(width):
        h = h_ref[...]
        gate = _dot(h, wg_ref[:, :width])
        up = _dot(h, wu_ref[:, :width])
        act = (0.5 * (gate * jax.nn.sigmoid(gate)) * up).astype(BF16)
        for c in range(0, o_ref.shape[1], FFN_TN):
            o_ref[:, c:c + FFN_TN] += _dot(act, wd_ref[:width, c:c + FFN_TN])

    if tail == tf:
        accumulate(tf)
    else:
        pl.when(j < last)(functools.partial(accumulate, tf))
        pl.when(j == last)(functools.partial(accumulate, tail))

    if final_norm:
        @pl.when(j == last)
        def _():
            for r in range(0, tm, NORM_ROWS):
                o_ref[r:r + NORM_ROWS, :] = _rms(o_ref[r:r + NORM_ROWS, :], gf_ref[...])

    for src, dst in zip(cast_in, cast_out):
        dst[...] = src[...].astype(BF16)


def _stripe_spec(a, n_steps, step_of):
    r, c = a.shape
    rows = CAST_ROWS
    while pl.cdiv(r, rows) > n_steps:
        rows += CAST_ROWS
    n_blocks = pl.cdiv(r, rows)
    return pl.BlockSpec((rows, c), lambda *idx: (jnp.minimum(step_of(*idx), n_blocks - 1), 0))


def _cast_specs(a, n_i, n_j, out_stripe=None):
    r, c = a.shape
    if out_stripe is None:
        if r % n_i == 0 and c % n_j == 0 and (r // n_i) % CAST_ROWS == 0 and (c // n_j) % LANES == 0:
            return (pl.BlockSpec((r // n_i, c // n_j), lambda i, j: (i, j)),) * 2
        if r % n_j == 0 and c % n_i == 0 and (r // n_j) % CAST_ROWS == 0 and (c // n_i) % LANES == 0:
            return (pl.BlockSpec((r // n_j, c // n_i), lambda i, j: (j, i)),) * 2
        out_stripe = lambda b: b
    n_blocks, rem = divmod(r, CAST_ROWS)
    assert rem == 0 and n_blocks <= n_i * n_j
    stripe = lambda i, j: jnp.minimum(i * n_j + j, n_blocks - 1)
    return (pl.BlockSpec((CAST_ROWS, c), lambda i, j: (stripe(i, j), 0)),
            pl.BlockSpec((CAST_ROWS, c), lambda i, j: (out_stripe(stripe(i, j)), 0)))


def _ffn(x, g, w_gate, w_up, w_down, gf, *, final_norm, to_cast=()):
    n, d = x.shape
    d_ff = w_gate.shape[1]
    prefetch_x = not to_cast
    tf = FFN_TF_WIDE if prefetch_x else FFN_TF
    n_i, n_j = n // FFN_TM, pl.cdiv(d_ff, tf)
    tail = d_ff - (n_j - 1) * tf
    assert tail % MXU_DIM == 0 and n_j >= 2
    to_cast = [a if isinstance(a, tuple) else (a, None) for a in to_cast]
    cast_specs = [_cast_specs(a, n_i, n_j, out_stripe) for a, out_stripe in to_cast]
    to_cast = [a for a, _ in to_cast]
    scratch = [pltpu.VMEM((FFN_TM, d), BF16)]
    if prefetch_x:
        scratch += [pltpu.VMEM((FFN_TM, d), F32), pltpu.SemaphoreType.DMA(())]
    outs = pl.pallas_call(
        functools.partial(_ffn_body, final_norm=final_norm, n_cast=len(to_cast), tf=tf, tail=tail,
                          prefetch_x=prefetch_x),
        grid=(n_i, n_j),
        in_specs=[
            pl.BlockSpec(memory_space=pl.ANY) if prefetch_x else pl.BlockSpec((FFN_TM, d), lambda i, j: (i, 0)),
            pl.BlockSpec((1, d), lambda i, j: (0, 0)),
            pl.BlockSpec((d, tf), lambda i, j: (0, j)),
            pl.BlockSpec((d, tf), lambda i, j: (0, j)),
            pl.BlockSpec((tf, d), lambda i, j: (j, 0)),
            pl.BlockSpec((1, d), lambda i, j: (0, 0)),
        ] + [cs[0] for cs in cast_specs],
        out_specs=[pl.BlockSpec((FFN_TM, d), lambda i, j: (i, 0))] + [cs[1] for cs in cast_specs],
        out_shape=[jax.ShapeDtypeStruct((n, d), F32)] + [jax.ShapeDtypeStruct(a.shape, BF16) for a in to_cast],
        scratch_shapes=scratch,
        compiler_params=pltpu.CompilerParams(
            dimension_semantics=("arbitrary" if prefetch_x else "parallel", "arbitrary"),
            vmem_limit_bytes=VMEM_LIMIT_BYTES),
        name="ffn_final" if final_norm else "ffn",
    )(x, g, w_gate, w_up, w_down, gf, *to_cast)
    return outs[0], outs[1:]


COL_Q = 0
COL_KV = D_ATT
COL_U = D_ATT + 6 * D_KV
D_POOL_COLS = 2048
COL_GATES = COL_U + D_POOL_COLS
N_GATES = 3 * N_ATT_HEADS


def _inproj_body(x_hbm, g_ref, w_ref, wg_ref, o_ref, og_ref, h_ref, xbuf_ref, sem):
    j = pl.program_id(1)
    _prefetch_row_tile(x_hbm, xbuf_ref, sem)

    @pl.when(j == 0)
    def _():
        for r in range(0, xbuf_ref.shape[0], NORM_ROWS):
            h_ref[r:r + NORM_ROWS, :] = _rms(xbuf_ref[r:r + NORM_ROWS, :], g_ref[...]).astype(BF16)

    for r in range(0, o_ref.shape[0], PROJ_ROWS):
        o_ref[r:r + PROJ_ROWS, :] = _dot_nt(h_ref[r:r + PROJ_ROWS, :], w_ref[...])

    @pl.when(j == pl.num_programs(1) - 1)
    def _():
        og_ref[...] = _dot_nt(h_ref[...], wg_ref[...])


def _inproj_stripe(b):
    n_qkv, n_g, n_u = COL_U // CAST_ROWS, N_GATES // CAST_ROWS, D_POOL_COLS // CAST_ROWS
    return jnp.where(b < n_qkv, b, jnp.where(b < n_qkv + n_g, b + n_u, b - n_g))


def _inproj(x, g, w_r):
    n, d = x.shape
    assert w_r.shape[0] == COL_GATES + N_GATES and N_GATES <= PROJ_TG
    assert COL_GATES % PROJ_TN == 0 and COL_U % CAST_ROWS == 0 and N_GATES % CAST_ROWS == 0
    wg = jnp.pad(w_r[COL_GATES:], ((0, PROJ_TG - N_GATES), (0, 0)))
    return pl.pallas_call(
        _inproj_body,
        grid=(n // PROJ_TM, COL_GATES // PROJ_TN),
        in_specs=[
            pl.BlockSpec(memory_space=pl.ANY),
            pl.BlockSpec((1, d), lambda i, j: (0, 0)),
            pl.BlockSpec((PROJ_TN, d), lambda i, j: (j, 0)),
            pl.BlockSpec((PROJ_TG, d), lambda i, j: (0, 0), pipeline_mode=pl.Buffered(1)),
        ],
        out_specs=[pl.BlockSpec((PROJ_TM, PROJ_TN), lambda i, j: (i, j)),
                   pl.BlockSpec((PROJ_TM, PROJ_TG), lambda i, j: (i, 0))],
        out_shape=[jax.ShapeDtypeStruct((n, COL_GATES), F32), jax.ShapeDtypeStruct((n, PROJ_TG), F32)],
        scratch_shapes=[pltpu.VMEM((PROJ_TM, d), BF16), pltpu.VMEM((PROJ_TM, d), F32), pltpu.SemaphoreType.DMA(())],
        compiler_params=pltpu.CompilerParams(
            dimension_semantics=("arbitrary", "arbitrary"), vmem_limit_bytes=VMEM_LIMIT_BYTES),
        name="in_proj",
    )(x, g, w_r, wg)


def _rel_bucket_np(n):
    max_exact = REL_BUCKETS // 2
    n = np.maximum(n, 0)
    nf = np.maximum(n, 1).astype(np.float32)
    large = max_exact + (np.log(nf / max_exact) / math.log(REL_MAX_DIST / max_exact)
                         * (REL_BUCKETS - max_exact)).astype(np.int32)
    large = np.minimum(large, REL_BUCKETS - 1)
    return np.where(n < max_exact, n, large)


FAR_DIST = int(np.max(np.nonzero(_rel_bucket_np(np.arange(4 * REL_MAX_DIST)) < REL_BUCKETS - 1)[0])) + 1
CMP_ROW_OFF = -(-(FAR_DIST + CMP_BLOCK - 1) // CMP_STRIDE) - 1
TAB_DIAG, TAB_PREV, TAB_FAR, TAB_EDGE = 0, 1, 2, 3


def _pick(rel_table, idx):
    onehot = jnp.asarray(np.arange(REL_BUCKETS)[:, None] == idx[None, :], F32)
    return jnp.einsum("kh,kn->hn", rel_table.astype(F32), onehot, precision=lax.Precision.HIGHEST)


def _group_lanes(t):
    t = t.reshape((N_KV_GROUPS, HEADS_PER_GROUP) + t.shape[1:])
    t = jnp.moveaxis(t, 1, -2)
    return t.reshape(t.shape[:-2] + (HEADS_PER_GROUP * t.shape[-1],))


def _toeplitz_body(p_ref, o_ref):
    blk = o_ref.shape[-1]
    for t in range(o_ref.shape[0]):
        y = pltpu.roll(jnp.broadcast_to(p_ref[t:t + 1, :], (blk, 2 * blk)), 0, 1, stride=1, stride_axis=0)
        o_ref[t] = y[:, :blk]


def _bias_tables(rel_table, n_cmp):
    blk = ATT_BLK
    n_heads = rel_table.shape[1]
    far = REL_BUCKETS - 1
    assert blk >= FAR_DIST and WINDOW % blk == 0
    m = np.arange(2 * blk)
    d = np.where(m < blk, m, m - 2 * blk)
    idx = np.stack([_rel_bucket_np(d), _rel_bucket_np(blk + d), np.full_like(d, far), np.full_like(d, far)])
    mask = np.stack([np.where(d < 0, NEG, 0.0), np.zeros_like(d, np.float64), np.zeros_like(d, np.float64),
                     np.where(d >= 0, NEG, 0.0)]).astype(np.float32)
    profiles = _pick(rel_table, idx.reshape(-1)).reshape(n_heads, 4, 2 * blk) + mask[None]
    sweep = pl.pallas_call(
        _toeplitz_body,
        grid=(N_KV_GROUPS, HEADS_PER_GROUP),
        in_specs=[pl.BlockSpec((None, 4, 2 * blk), lambda g, h: (g * HEADS_PER_GROUP + h, 0, 0))],
        out_specs=pl.BlockSpec((None, 4, blk, blk), lambda g, h: (g, 0, 0, h)),
        out_shape=jax.ShapeDtypeStruct((N_KV_GROUPS, 4, blk, HEADS_PER_GROUP * blk), F32),
        name="bias_tables",
    )(profiles)

    rows = np.arange(n_cmp - CMP_ROW_OFF, n_cmp + (blk - CMP_BLOCK) // CMP_STRIDE + 1)
    dist = np.arange(blk)[None, :] - ((rows[:, None] - n_cmp) * CMP_STRIDE + CMP_BLOCK - 1)
    band = _pick(rel_table, _rel_bucket_np(dist).reshape(-1)).reshape(n_heads, len(rows), blk)
    band = band + np.where(dist < 0, NEG, 0.0).astype(np.float32)[None]
    before = jnp.broadcast_to(rel_table[far].astype(F32)[:, None, None], (n_heads, int(rows[0]), blk))
    after = jnp.full((n_heads, 2 * n_cmp - int(rows[-1]) - 1, blk), NEG, F32)
    cmp_tab = _group_lanes(jnp.concatenate([before, band, after], axis=1))
    return sweep, cmp_tab


def _compress(kv_ref, pos_ref, w1_ref, w2_ref):
    seq, dk = kv_ref.shape
    n_half = seq // CMP_STRIDE
    first = jnp.zeros((n_half, w1_ref.shape[1]), F32)
    second = jnp.zeros((n_half, w1_ref.shape[1]), F32)
    for l in range(CMP_STRIDE):
        rows = kv_ref[pl.ds(l, n_half, stride=CMP_STRIDE), :]
        a = (rows + pos_ref[l:l + 1, :]).astype(BF16)
        b = (rows + pos_ref[CMP_STRIDE + l:CMP_STRIDE + l + 1, :]).astype(BF16)
        first += _dot(a, w1_ref[l * dk:(l + 1) * dk, :])
        second += _dot(b, w1_ref[(CMP_STRIDE + l) * dk:(CMP_STRIDE + l + 1) * dk, :])
    pre = first + pltpu.roll(second, n_half - 1, 0)
    hid = (pre * jax.nn.sigmoid(pre)).astype(BF16)
    return _dot(hid, w2_ref[...])


N_NSA_INPUTS = 18


def _nsa_body(*refs, n_cast):
    (q_ref, kc_ref, vc_ref, ks_ref, vs_ref, kw_ref, vw_ref, gates_ref,
     posk_ref, wk1_ref, wk2_ref, posv_ref, wv1_ref, wv2_ref,
     tab_ref, tabc_ref, overlap_ref, expand_ref) = refs[:N_NSA_INPUTS]
    cast_in = refs[N_NSA_INPUTS:N_NSA_INPUTS + n_cast]
    o_ref = refs[N_NSA_INPUTS + n_cast]
    cast_out = refs[N_NSA_INPUTS + n_cast + 1:N_NSA_INPUTS + 2 * n_cast + 1]
    (kcmp_ref, vcmp_t_ref, ksel_ref, vsel_t_ref, kwin_ref, vwin_t_ref, gates_t_ref,
     s_ref) = refs[N_NSA_INPUTS + 2 * n_cast + 1:]
    g = pl.program_id(1)
    qi = pl.program_id(2)
    tq = ATT_BLK
    hpg = HEADS_PER_GROUP
    lanes = hpg * tq
    n_cmp = kcmp_ref.shape[0]
    n_sb = overlap_ref.shape[0]
    seq = ks_ref.shape[0]
    assert tq & (tq - 1) == 0 and SEL_BLOCK & (SEL_BLOCK - 1) == 0

    @pl.when(qi == 0)
    def _():
        kcmp_ref[...] = _compress(kc_ref, posk_ref, wk1_ref, wk2_ref).astype(BF16)
        vcmp_t_ref[...] = _compress(vc_ref, posv_ref, wv1_ref, wv2_ref).astype(BF16).T
        for c in range(0, seq, LANES):
            ksel_ref[c:c + LANES, 0:HEAD_DIM] = ks_ref[c:c + LANES, :].astype(BF16)
            ksel_ref[c:c + LANES, HEAD_DIM:] = expand_ref[c:c + LANES, :]
            kwin_ref[c:c + LANES, :] = kw_ref[c:c + LANES, :].astype(BF16)
            vsel_t_ref[:, c:c + LANES] = vs_ref[c:c + LANES, :].T.astype(BF16)
            vwin_t_ref[:, c:c + LANES] = vw_ref[c:c + LANES, :].T.astype(BF16)

    q = q_ref[...] * (HEAD_DIM ** -0.5)
    q_t = jnp.concatenate([q[:, h * HEAD_DIM:(h + 1) * HEAD_DIM].T for h in range(hpg)],
                          axis=1).astype(BF16)
    t_lane = qi * tq + (lax.broadcasted_iota(jnp.int32, (1, lanes), 1) & (tq - 1))

    first_row = pl.multiple_of(n_cmp - qi * (tq // CMP_STRIDE), 8)
    logit = _dot(kcmp_ref[...], q_t) + tabc_ref[pl.ds(first_row, n_cmp), :]
    e = jnp.exp(logit - jnp.max(logit, axis=0, keepdims=True))
    any_valid = (t_lane >= CMP_BLOCK - 1).astype(F32)
    p = e / jnp.sum(e, axis=0, keepdims=True) * any_valid
    o_cmp = _dot(vcmp_t_ref[...], p.astype(BF16))

    psum = p[:, 0:tq]
    for h in range(1, hpg):
        psum = psum + p[:, h * tq:(h + 1) * tq]
    p_hi = psum.astype(BF16)
    r1 = psum - p_hi.astype(F32)
    p_mid = r1.astype(BF16)
    p_lo = (r1 - p_mid.astype(F32)).astype(BF16)
    ov = overlap_ref[...]
    imp = _dot(ov, p_hi) + _dot(ov, p_mid) + _dot(ov, p_lo)
    cur = lax.shift_right_logical(t_lane[:, 0:tq], int(math.log2(SEL_BLOCK)))
    jb = lax.broadcasted_iota(jnp.int32, (n_sb, tq), 0)
    forced = (jb == 0) | (jb == cur) | (jb == cur - 1)
    score = jnp.where(jb > cur, FUTURE_SCORE, imp + jnp.where(forced, FORCE_BONUS, 0.0))
    rank = jnp.zeros((n_sb, tq), jnp.int32)
    for j2 in range(n_sb):
        row = score[j2:j2 + 1, :]
        rank += ((row > score) | ((row == score) & (j2 < jb))).astype(jnp.int32)
    sel_neg = jnp.where(rank < SEL_TOP, 0.0, NEG).astype(BF16)
    n_aug = ksel_ref.shape[1] - HEAD_DIM
    q_sel = jnp.concatenate([q_t, jnp.concatenate([sel_neg] * hpg, axis=1),
                             jnp.zeros((n_aug - n_sb, lanes), BF16)], axis=0)

    def sweep(k_ref, v_t_ref, q_rhs, lo, edge):
        def rows(kb):
            return pl.ds(pl.multiple_of(kb * ATT_BLK, ATT_BLK), ATT_BLK)

        def scores(kb, m):
            dblk = qi - kb
            tab = jnp.minimum(dblk, TAB_FAR)
            if edge is not None:
                tab = jnp.where(dblk == edge, TAB_EDGE, tab)
            s = _dot(k_ref[rows(kb), :], q_rhs) + tab_ref[tab]
            s_ref[rows(kb), :] = s
            return jnp.maximum(m, jnp.max(s, axis=0, keepdims=True))

        def over_chunks(body, init):
            n = qi + 1 - lo
            out = lax.fori_loop(0, n // 2, lambda i, c: body(lo + 2 * i + 1, body(lo + 2 * i, c)), init)
            return lax.cond(n % 2 == 1, lambda c: body(qi, c), lambda c: c, out)

        m = over_chunks(scores, jnp.full((1, lanes), NEG, F32))

        def accumulate(kb, carry):
            l, acc = carry
            pr = jnp.exp(s_ref[rows(kb), :] - m)
            return (l + jnp.sum(pr, axis=0, keepdims=True),
                    acc + _dot(v_t_ref[:, rows(kb)], pr.astype(BF16)))

        l, acc = over_chunks(accumulate, (jnp.zeros((1, lanes), F32), jnp.zeros((HEAD_DIM, lanes), F32)))
        return acc / l

    o_sel = sweep(ksel_ref, vsel_t_ref, q_sel, 0, None)
    n_back = WINDOW // ATT_BLK
    o_win = sweep(kwin_ref, vwin_t_ref, q_t, jnp.maximum(qi - n_back, 0), n_back)

    for c in range(0, tq, LANES):
        gates_t_ref[:, c:c + LANES] = jax.nn.sigmoid(gates_ref[c:c + LANES, :]).T
    for h in range(hpg):
        head = g * hpg + h
        sl = slice(h * tq, (h + 1) * tq)
        y = (gates_t_ref[pl.ds(head, 1), :] * o_cmp[:, sl]
             + gates_t_ref[pl.ds(N_ATT_HEADS + head, 1), :] * o_sel[:, sl]
             + gates_t_ref[pl.ds(2 * N_ATT_HEADS + head, 1), :] * o_win[:, sl])
        for c in range(0, tq, LANES):
            o_ref[c:c + LANES, h * HEAD_DIM:(h + 1) * HEAD_DIM] = y[:, c:c + LANES].T.astype(o_ref.dtype)

    for src, dst in zip(cast_in, cast_out):
        dst[...] = src[...].astype(BF16)


def _nsa(proj, gates, batch, seq, cmp_pos_k, w_ck1, w_ck2, cmp_pos_v, w_cv1, w_cv2, rel_table, to_cast=()):
    n_qt = seq // ATT_BLK
    hpg = HEADS_PER_GROUP
    n_cmp = seq // CMP_STRIDE
    n_sb = seq // SEL_BLOCK
    n_aug = LANES
    assert CMP_BLOCK == 2 * CMP_STRIDE and n_cmp % 8 == 0 and n_sb % 16 == 0 and n_sb <= n_aug
    assert (ATT_BLK // CMP_STRIDE) % 8 == 0 and n_qt * (ATT_BLK // CMP_STRIDE) <= n_cmp

    sweep_tab, cmp_tab = _bias_tables(rel_table, n_cmp)
    cs = np.arange(n_cmp)[None, :] * CMP_STRIDE
    ss = np.arange(n_sb)[:, None] * SEL_BLOCK
    overlap = (cs < ss + SEL_BLOCK) & (cs + CMP_BLOCK > ss) & (np.arange(n_cmp)[None, :] < n_cmp - 1)
    expand = np.arange(seq)[:, None] // SEL_BLOCK == np.arange(n_aug)[None, :]
    overlap = jnp.asarray(overlap, BF16)
    expand = jnp.asarray(expand, BF16)

    kv_col0 = COL_KV // HEAD_DIM

    def kv_spec(idx):
        return pl.BlockSpec((seq, HEAD_DIM), lambda b, g, qi, idx=idx: (b, kv_col0 + idx * N_KV_GROUPS + g))

    def whole(arr):
        return pl.BlockSpec(arr.shape, lambda b, g, qi, nd=arr.ndim: (0,) * nd)

    in_specs = [
        pl.BlockSpec((ATT_BLK, hpg * HEAD_DIM), lambda b, g, qi: (b * n_qt + qi, g)),
        kv_spec(0), kv_spec(1), kv_spec(2), kv_spec(3), kv_spec(4), kv_spec(5),
        pl.BlockSpec((ATT_BLK, LANES), lambda b, g, qi: (b * n_qt + qi, 0)),
        whole(cmp_pos_k), whole(w_ck1), whole(w_ck2), whole(cmp_pos_v), whole(w_cv1), whole(w_cv2),
        pl.BlockSpec((None,) + sweep_tab.shape[1:], lambda b, g, qi: (g, 0, 0, 0)),
        pl.BlockSpec((None,) + cmp_tab.shape[1:], lambda b, g, qi: (g, 0, 0)),
        whole(overlap), whole(expand),
    ]
    assert len(in_specs) == N_NSA_INPUTS
    n_steps = batch * N_KV_GROUPS * n_qt
    cast_specs = [_stripe_spec(a, n_steps, lambda b, g, qi: (b * N_KV_GROUPS + g) * n_qt + qi) for a in to_cast]
    outs = pl.pallas_call(
        functools.partial(_nsa_body, n_cast=len(to_cast)),
        grid=(batch, N_KV_GROUPS, n_qt),
        in_specs=in_specs + cast_specs,
        out_specs=[pl.BlockSpec((ATT_BLK, hpg * HEAD_DIM), lambda b, g, qi: (b * n_qt + qi, g))] + cast_specs,
        out_shape=[jax.ShapeDtypeStruct((batch * seq, D_ATT), BF16)]
        + [jax.ShapeDtypeStruct(a.shape, BF16) for a in to_cast],
        scratch_shapes=[pltpu.VMEM((n_cmp, HEAD_DIM), BF16), pltpu.VMEM((HEAD_DIM, n_cmp), BF16),
                        pltpu.VMEM((seq, HEAD_DIM + n_aug), BF16), pltpu.VMEM((HEAD_DIM, seq), BF16),
                        pltpu.VMEM((seq, HEAD_DIM), BF16), pltpu.VMEM((HEAD_DIM, seq), BF16),
                        pltpu.VMEM((LANES, ATT_BLK), F32), pltpu.VMEM((seq, hpg * ATT_BLK), F32)],
        compiler_params=pltpu.CompilerParams(
            dimension_semantics=(("arbitrary",) * 3 if to_cast else ("parallel", "parallel", "arbitrary")),
            vmem_limit_bytes=VMEM_LIMIT_BYTES),
        name="nsa",
    )(proj, proj, proj, proj, proj, proj, proj, gates,
      cmp_pos_k, w_ck1.astype(BF16), w_ck2.astype(BF16), cmp_pos_v, w_cv1.astype(BF16), w_cv2.astype(BF16),
      sweep_tab, cmp_tab, overlap, expand, *to_cast)
    return outs[0], outs[1:]


def _pool_body(u_ref, w_ref, scale_ref, o_ref):
    gi = pl.program_id(1)
    u = u_ref[...]
    seq = u.shape[0]
    t = lax.broadcasted_iota(jnp.int32, (seq, 1), 0)

    def shifted(x, k):
        return jnp.where(t >= k, pltpu.roll(x, k, 0), 0.0)

    sums = [u]
    for n in range(int(math.log2(POOL_WINDOWS[-1]))):
        sums.append(sums[-1] + shifted(sums[-1], 2 ** n))
    wsum = sums[int(math.log2(POOL_WINDOWS[0]))]
    for idx in range(1, len(POOL_WINDOWS)):
        wsum = jnp.where(gi >= idx, sums[int(math.log2(POOL_WINDOWS[idx]))], wsum)
    window = POOL_WINDOWS[0]
    for idx in range(1, len(POOL_WINDOWS)):
        window = jnp.where(gi >= idx, POOL_WINDOWS[idx], window)
    cnt = jnp.minimum(t + 1, window).astype(F32)
    d = (wsum / cnt - u).astype(BF16)
    o_ref[...] = (_dot(d, w_ref[0]) * scale_ref[...]).astype(o_ref.dtype)


def _pool(proj, batch, seq, w_pool, pool_scale):
    n_g, dg, _ = w_pool.shape
    assert all(w == 2 ** int(math.log2(w)) for w in POOL_WINDOWS) and list(POOL_WINDOWS) == sorted(POOL_WINDOWS)
    col0 = COL_U // dg
    return pl.pallas_call(
        _pool_body,
        grid=(batch, n_g),
        in_specs=[
            pl.BlockSpec((seq, dg), lambda b, gi: (b, col0 + gi)),
            pl.BlockSpec((1, dg, dg), lambda b, gi: (gi, 0, 0)),
            pl.BlockSpec((1, dg), lambda b, gi: (0, gi)),
        ],
        out_specs=pl.BlockSpec((seq, dg), lambda b, gi: (b, gi)),
        out_shape=jax.ShapeDtypeStruct((batch * seq, n_g * dg), BF16),
        compiler_params=pltpu.CompilerParams(
            dimension_semantics=("parallel", "parallel"), vmem_limit_bytes=VMEM_LIMIT_BYTES),
        name="pool",
    )(proj, w_pool.astype(BF16), pool_scale.reshape(1, -1))


def _outproj_body(x_ref, ya_ref, yp_ref, wa_ref, wp_ref, o_ref):
    o_ref[...] = x_ref[...] + _dot(ya_ref[...], wa_ref[...]) + _dot(yp_ref[...], wp_ref[...])


def _outproj(x, y_att, y_pool, w_out):
    n, d = x.shape
    da = y_att.shape[1]
    dp = y_pool.shape[1]
    w = w_out
    return pl.pallas_call(
        _outproj_body,
        grid=(d // OUT_TN, n // OUT_TM),
        in_specs=[
            pl.BlockSpec((OUT_TM, OUT_TN), lambda j, i: (i, j)),
            pl.BlockSpec((OUT_TM, da), lambda j, i: (i, 0)),
            pl.BlockSpec((OUT_TM, dp), lambda j, i: (i, 0)),
            pl.BlockSpec((da, OUT_TN), lambda j, i: (0, j)),
            pl.BlockSpec((dp, OUT_TN), lambda j, i: (da // dp, j)),
        ],
        out_specs=pl.BlockSpec((OUT_TM, OUT_TN), lambda j, i: (i, j)),
        out_shape=jax.ShapeDtypeStruct((n, d), F32),
        compiler_params=pltpu.CompilerParams(
            dimension_semantics=("parallel", "parallel"), vmem_limit_bytes=VMEM_LIMIT_BYTES),
        name="out_proj",
    )(x, y_att, y_pool, w, w)


def kernel(x, norm_ffn1, w_ffn1_gate, w_ffn1_up, w_ffn1_down, norm_mix, w_in, cmp_pos_k, w_cmp_k1, w_cmp_k2,
           cmp_pos_v, w_cmp_v1, w_cmp_v2, w_pool, pool_scale, w_out, rel_table, norm_ffn2, w_ffn2_gate,
           w_ffn2_up, w_ffn2_down, norm_final):
    batch, seq, d = x.shape
    depth = norm_ffn1.shape[0]
    xf = x.reshape(batch * seq, d)
    gf = norm_final.reshape(1, d)
    for l in range(depth):
        xf, _ = _ffn(xf, norm_ffn1[l].reshape(1, d), w_ffn1_gate[l].astype(BF16), w_ffn1_up[l].astype(BF16),
                     w_ffn1_down[l].astype(BF16), gf, final_norm=False)
        w_in_t = w_in[l].T
        w_r = jnp.concatenate([w_in_t[:COL_U], w_in_t[COL_U + N_GATES:], w_in_t[COL_U:COL_U + N_GATES]]).astype(BF16)
        proj, gates = _inproj(xf, norm_mix[l].reshape(1, d), w_r)
        y_att, (w2_gate, w2_up, w2_down) = _nsa(
            proj, gates, batch, seq, cmp_pos_k[l], w_cmp_k1[l], w_cmp_k2[l], cmp_pos_v[l], w_cmp_v1[l], w_cmp_v2[l],
            rel_table, to_cast=(w_ffn2_gate[l], w_ffn2_up[l], w_ffn2_down[l]))
        y_pool = _pool(proj, batch, seq, w_pool[l], pool_scale[l])
        xf = _outproj(xf, y_att, y_pool, w_out[l].astype(BF16))
        xf, _ = _ffn(xf, norm_ffn2[l].reshape(1, d), w2_gate, w2_up, w2_down, gf, final_norm=(l == depth - 1))
    if depth == 0:
        raise ValueError("depth must be positive")
    return xf.reshape(batch, seq, d)
```

```python
import functools
import math

import jax
import jax.numpy as jnp
import numpy as np
from jax import lax
from jax.experimental import pallas as pl
from jax.experimental.pallas import tpu as pltpu

HEAD_DIM = 128
N_ATT_HEADS = 16
N_KV_GROUPS = 4
HEADS_PER_GROUP = N_ATT_HEADS // N_KV_GROUPS
D_ATT = N_ATT_HEADS * HEAD_DIM
D_KV = N_KV_GROUPS * HEAD_DIM
N_POOL_GROUPS = 4
POOL_WINDOWS = (2, 4, 8, 16)
CMP_BLOCK = 32
CMP_STRIDE = 16
SEL_BLOCK = 64
SEL_TOP = 8
WINDOW = 512
REL_BUCKETS = 32
REL_MAX_DIST = 128
EPS = 1e-6
NEG = -1e30
FORCE_BONUS = 1e4
FUTURE_SCORE = -1e9

LANES = 128
MXU_DIM = 256
VMEM_LIMIT_BYTES = 60 * 2 ** 20

FFN_TM = 512
FFN_TF = 2 * MXU_DIM
FFN_TN = 1024
NORM_ROWS = 32
CAST_ROWS = 16
PROJ_TM = 1024
PROJ_ROWS = 512
PROJ_TN = 512
PROJ_TG = MXU_DIM
ATT_BLK = 256
OUT_TM = 1024
OUT_TN = 1024

BF16 = jnp.bfloat16
F32 = jnp.float32


def _dot(a, b):
    return jnp.dot(a, b, preferred_element_type=F32)


def _dot_nt(a, b):
    return lax.dot_general(a, b, (((1,), (1,)), ((), ())), preferred_element_type=F32)


def _rms(x, g):
    return x * lax.rsqrt(jnp.mean(x * x, axis=-1, keepdims=True) + EPS) * g


def _prefetch_row_tile(x_hbm, xbuf_ref, sem):
    i = pl.program_id(0)
    j = pl.program_id(1)
    tm = xbuf_ref.shape[0]

    def x_copy(tile):
        return pltpu.make_async_copy(x_hbm.at[pl.ds(pl.multiple_of(tile * tm, tm), tm), :], xbuf_ref, sem)

    @pl.when((i == 0) & (j == 0))
    def _():
        x_copy(0).start()

    @pl.when(j == 0)
    def _():
        x_copy(i).wait()

    @pl.when((j == pl.num_programs(1) - 2) & (i + 1 < pl.num_programs(0)))
    def _():
        x_copy(i + 1).start()


def _ffn_body(*refs, final_norm, n_cast, tf, tail):
    x_hbm, g_ref, wg_ref, wu_ref, wd_ref, gf_ref = refs[:6]
    cast_in = refs[6:6 + n_cast]
    o_ref = refs[6 + n_cast]
    cast_out = refs[7 + n_cast:7 + 2 * n_cast]
    h_ref, x_tile, sem = refs[7 + 2 * n_cast:]
    j = pl.program_id(1)
    last = pl.num_programs(1) - 1
    tm = o_ref.shape[0]
    _prefetch_row_tile(x_hbm, x_tile, sem)

    @pl.when(j == 0)
    def _():
        for r in range(0, tm, NORM_ROWS):
            x = x_tile[r:r + NORM_ROWS, :]
            h_ref[r:r + NORM_ROWS, :] = _rms(x, g_ref[...]).astype(BF16)
            o_ref[r:r + NORM_ROWS, :] = x

    def accumulate(width):
        h = h_ref[...]
        gate = _dot(h, wg_ref[:, :width])
        up = _dot(h, wu_ref[:, :width])
        act = (0.5 * (gate * jax.nn.sigmoid(gate)) * up).astype(BF16)
        for c in range(0, o_ref.shape[1], FFN_TN):
            o_ref[:, c:c + FFN_TN] += _dot(act, wd_ref[:width, c:c + FFN_TN])

    if tail == tf:
        accumulate(tf)
    else:
        pl.when(j < last)(functools.partial(accumulate, tf))
        pl.when(j == last)(functools.partial(accumulate, tail))

    if final_norm:
        @pl.when(j == last)
        def _():
            for r in range(0, tm, NORM_ROWS):
                o_ref[r:r + NORM_ROWS, :] = _rms(o_ref[r:r + NORM_ROWS, :], gf_ref[...])

    for src, dst in zip(cast_in, cast_out):
        dst[...] = src[...].astype(BF16)


def _stripe_spec(a, n_steps, step_of):
    r, c = a.shape
    rows = CAST_ROWS
    while pl.cdiv(r, rows) > n_steps:
        rows += CAST_ROWS
    n_blocks = pl.cdiv(r, rows)
    return pl.BlockSpec((rows, c), lambda *idx: (jnp.minimum(step_of(*idx), n_blocks - 1), 0))


def _ffn(x, g, w_gate, w_up, w_down, gf, *, final_norm, to_cast=()):
    n, d = x.shape
    d_ff = w_gate.shape[1]
    tf = FFN_TF
    n_i, n_j = n // FFN_TM, pl.cdiv(d_ff, tf)
    tail = d_ff - (n_j - 1) * tf
    assert tail % MXU_DIM == 0 and n_j >= 2
    cast_specs = [_stripe_spec(a, n_i * n_j, lambda i, j: i * n_j + j) for a in to_cast]
    outs = pl.pallas_call(
        functools.partial(_ffn_body, final_norm=final_norm, n_cast=len(to_cast), tf=tf, tail=tail),
        grid=(n_i, n_j),
        in_specs=[
            pl.BlockSpec(memory_space=pl.ANY),
            pl.BlockSpec((1, d), lambda i, j: (0, 0)),
            pl.BlockSpec((d, tf), lambda i, j: (0, j)),
            pl.BlockSpec((d, tf), lambda i, j: (0, j)),
            pl.BlockSpec((tf, d), lambda i, j: (j, 0)),
            pl.BlockSpec((1, d), lambda i, j: (0, 0)),
        ] + cast_specs,
        out_specs=[pl.BlockSpec((FFN_TM, d), lambda i, j: (i, 0))] + cast_specs,
        out_shape=[jax.ShapeDtypeStruct((n, d), F32)] + [jax.ShapeDtypeStruct(a.shape, BF16) for a in to_cast],
        scratch_shapes=[pltpu.VMEM((FFN_TM, d), BF16), pltpu.VMEM((FFN_TM, d), F32), pltpu.SemaphoreType.DMA(())],
        compiler_params=pltpu.CompilerParams(
            dimension_semantics=("arbitrary", "arbitrary"), vmem_limit_bytes=VMEM_LIMIT_BYTES),
        name="ffn_final" if final_norm else "ffn",
    )(x, g, w_gate, w_up, w_down, gf, *to_cast)
    return outs[0], outs[1:]


COL_Q = 0
COL_KV = D_ATT
COL_U = D_ATT + 6 * D_KV
D_POOL_COLS = 2048
COL_GATES = COL_U + D_POOL_COLS
N_GATES = 3 * N_ATT_HEADS


def _inproj_body(x_hbm, g_ref, wa_ref, wu_ref, wg_ref, o_ref, og_ref, h_ref, xbuf_ref, sem, *, n_a):
    j = pl.program_id(1)
    _prefetch_row_tile(x_hbm, xbuf_ref, sem)

    @pl.when(j == 0)
    def _():
        for r in range(0, xbuf_ref.shape[0], NORM_ROWS):
            h_ref[r:r + NORM_ROWS, :] = _rms(xbuf_ref[r:r + NORM_ROWS, :], g_ref[...]).astype(BF16)

    def project(w_ref, dst_ref):
        for r in range(0, dst_ref.shape[0], PROJ_ROWS):
            dst_ref[r:r + PROJ_ROWS, :] = _dot_nt(h_ref[r:r + PROJ_ROWS, :], w_ref[...])

    pl.when(j < n_a)(functools.partial(project, wa_ref, o_ref))
    pl.when(j >= n_a)(functools.partial(project, wu_ref, o_ref))
    pl.when(j == pl.num_programs(1) - 1)(functools.partial(project, wg_ref, og_ref))


def _inproj(x, g, w_in_t):
    n, d = x.shape
    assert w_in_t.shape[0] == COL_GATES + N_GATES and N_GATES <= PROJ_TG
    assert COL_U % PROJ_TN == 0 and D_POOL_COLS % PROJ_TN == 0
    wu = w_in_t[COL_U + N_GATES:]
    wg = jnp.pad(w_in_t[COL_U:COL_U + N_GATES], ((0, PROJ_TG - N_GATES), (0, 0)))
    n_a = COL_U // PROJ_TN
    n_u = D_POOL_COLS // PROJ_TN
    return pl.pallas_call(
        functools.partial(_inproj_body, n_a=n_a),
        grid=(n // PROJ_TM, n_a + n_u),
        in_specs=[
            pl.BlockSpec(memory_space=pl.ANY),
            pl.BlockSpec((1, d), lambda i, j: (0, 0)),
            pl.BlockSpec((PROJ_TN, d), lambda i, j: (jnp.minimum(j, n_a - 1), 0)),
            pl.BlockSpec((PROJ_TN, d), lambda i, j: (jnp.maximum(j - n_a, 0), 0)),
            pl.BlockSpec((PROJ_TG, d), lambda i, j: (0, 0), pipeline_mode=pl.Buffered(1)),
        ],
        out_specs=[pl.BlockSpec((PROJ_TM, PROJ_TN), lambda i, j: (i, j)),
                   pl.BlockSpec((PROJ_TM, PROJ_TG), lambda i, j: (i, 0))],
        out_shape=[jax.ShapeDtypeStruct((n, COL_GATES), F32), jax.ShapeDtypeStruct((n, PROJ_TG), F32)],
        scratch_shapes=[pltpu.VMEM((PROJ_TM, d), BF16), pltpu.VMEM((PROJ_TM, d), F32), pltpu.SemaphoreType.DMA(())],
        compiler_params=pltpu.CompilerParams(
            dimension_semantics=("arbitrary", "arbitrary"), vmem_limit_bytes=VMEM_LIMIT_BYTES),
        name="in_proj",
    )(x, g, w_in_t, wu, wg)


def _rel_bucket_np(n):
    max_exact = REL_BUCKETS // 2
    n = np.maximum(n, 0)
    nf = np.maximum(n, 1).astype(np.float32)
    large = max_exact + (np.log(nf / max_exact) / math.log(REL_MAX_DIST / max_exact)
                         * (REL_BUCKETS - max_exact)).astype(np.int32)
    large = np.minimum(large, REL_BUCKETS - 1)
    return np.where(n < max_exact, n, large)


FAR_DIST = int(np.max(np.nonzero(_rel_bucket_np(np.arange(4 * REL_MAX_DIST)) < REL_BUCKETS - 1)[0])) + 1
CMP_ROW_OFF = -(-(FAR_DIST + CMP_BLOCK - 1) // CMP_STRIDE) - 1
TAB_DIAG, TAB_PREV, TAB_FAR, TAB_EDGE = 0, 1, 2, 3


def _pick(rel_table, idx):
    onehot = jnp.asarray(np.arange(REL_BUCKETS)[:, None] == idx[None, :], F32)
    return jnp.einsum("kh,kn->hn", rel_table.astype(F32), onehot, precision=lax.Precision.HIGHEST)


def _group_lanes(t):
    t = t.reshape((N_KV_GROUPS, HEADS_PER_GROUP) + t.shape[1:])
    t = jnp.moveaxis(t, 1, -2)
    return t.reshape(t.shape[:-2] + (HEADS_PER_GROUP * t.shape[-1],))


def _toeplitz_body(p_ref, o_ref):
    blk = o_ref.shape[-1]
    for t in range(o_ref.shape[0]):
        y = pltpu.roll(jnp.broadcast_to(p_ref[t:t + 1, :], (blk, 2 * blk)), 0, 1, stride=1, stride_axis=0)
        o_ref[t] = y[:, :blk]


def _bias_tables(rel_table, n_cmp):
    blk = ATT_BLK
    n_heads = rel_table.shape[1]
    far = REL_BUCKETS - 1
    assert blk >= FAR_DIST and WINDOW % blk == 0
    m = np.arange(2 * blk)
    d = np.where(m < blk, m, m - 2 * blk)
    idx = np.stack([_rel_bucket_np(d), _rel_bucket_np(blk + d), np.full_like(d, far), np.full_like(d, far)])
    mask = np.stack([np.where(d < 0, NEG, 0.0), np.zeros_like(d, np.float64), np.zeros_like(d, np.float64),
                     np.where(d >= 0, NEG, 0.0)]).astype(np.float32)
    profiles = _pick(rel_table, idx.reshape(-1)).reshape(n_heads, 4, 2 * blk) + mask[None]
    sweep = pl.pallas_call(
        _toeplitz_body,
        grid=(N_KV_GROUPS, HEADS_PER_GROUP),
        in_specs=[pl.BlockSpec((None, 4, 2 * blk), lambda g, h: (g * HEADS_PER_GROUP + h, 0, 0))],
        out_specs=pl.BlockSpec((None, 4, blk, blk), lambda g, h: (g, 0, 0, h)),
        out_shape=jax.ShapeDtypeStruct((N_KV_GROUPS, 4, blk, HEADS_PER_GROUP * blk), F32),
        name="bias_tables",
    )(profiles)

    rows = np.arange(n_cmp - CMP_ROW_OFF, n_cmp + (blk - CMP_BLOCK) // CMP_STRIDE + 1)
    dist = np.arange(blk)[None, :] - ((rows[:, None] - n_cmp) * CMP_STRIDE + CMP_BLOCK - 1)
    band = _pick(rel_table, _rel_bucket_np(dist).reshape(-1)).reshape(n_heads, len(rows), blk)
    band = band + np.where(dist < 0, NEG, 0.0).astype(np.float32)[None]
    before = jnp.broadcast_to(rel_table[far].astype(F32)[:, None, None], (n_heads, int(rows[0]), blk))
    after = jnp.full((n_heads, 2 * n_cmp - int(rows[-1]) - 1, blk), NEG, F32)
    cmp_tab = _group_lanes(jnp.concatenate([before, band, after], axis=1))
    return sweep, cmp_tab


def _compress(kv_ref, pos_ref, w1_ref, w2_ref):
    seq, dk = kv_ref.shape
    n_half = seq // CMP_STRIDE
    first = jnp.zeros((n_half, w1_ref.shape[1]), F32)
    second = jnp.zeros((n_half, w1_ref.shape[1]), F32)
    for l in range(CMP_STRIDE):
        rows = kv_ref[pl.ds(l, n_half, stride=CMP_STRIDE), :]
        a = (rows + pos_ref[l:l + 1, :]).astype(BF16)
        b = (rows + pos_ref[CMP_STRIDE + l:CMP_STRIDE + l + 1, :]).astype(BF16)
        first += _dot(a, w1_ref[l * dk:(l + 1) * dk, :])
        second += _dot(b, w1_ref[(CMP_STRIDE + l) * dk:(CMP_STRIDE + l + 1) * dk, :])
    pre = first + pltpu.roll(second, n_half - 1, 0)
    hid = (pre * jax.nn.sigmoid(pre)).astype(BF16)
    return _dot(hid, w2_ref[...])


N_NSA_INPUTS = 18


def _nsa_body(*refs, n_cast):
    (q_ref, kc_ref, vc_ref, ks_ref, vs_ref, kw_ref, vw_ref, gates_ref,
     posk_ref, wk1_ref, wk2_ref, posv_ref, wv1_ref, wv2_ref,
     tab_ref, tabc_ref, overlap_ref, expand_ref) = refs[:N_NSA_INPUTS]
    cast_in = refs[N_NSA_INPUTS:N_NSA_INPUTS + n_cast]
    o_ref = refs[N_NSA_INPUTS + n_cast]
    cast_out = refs[N_NSA_INPUTS + n_cast + 1:N_NSA_INPUTS + 2 * n_cast + 1]
    (kcmp_ref, vcmp_t_ref, ksel_ref, vsel_t_ref, kwin_ref, vwin_t_ref, gates_t_ref,
     s_ref) = refs[N_NSA_INPUTS + 2 * n_cast + 1:]
    g = pl.program_id(1)
    qi = pl.program_id(2)
    tq = ATT_BLK
    hpg = HEADS_PER_GROUP
    lanes = hpg * tq
    n_cmp = kcmp_ref.shape[0]
    n_sb = overlap_ref.shape[0]
    seq = ks_ref.shape[0]
    assert tq & (tq - 1) == 0 and SEL_BLOCK & (SEL_BLOCK - 1) == 0

    @pl.when(qi == 0)
    def _():
        kcmp_ref[...] = _compress(kc_ref, posk_ref, wk1_ref, wk2_ref).astype(BF16)
        vcmp_t_ref[...] = _compress(vc_ref, posv_ref, wv1_ref, wv2_ref).astype(BF16).T
        for c in range(0, seq, LANES):
            ksel_ref[c:c + LANES, 0:HEAD_DIM] = ks_ref[c:c + LANES, :].astype(BF16)
            ksel_ref[c:c + LANES, HEAD_DIM:] = expand_ref[c:c + LANES, :]
            kwin_ref[c:c + LANES, :] = kw_ref[c:c + LANES, :].astype(BF16)
            vsel_t_ref[:, c:c + LANES] = vs_ref[c:c + LANES, :].T.astype(BF16)
            vwin_t_ref[:, c:c + LANES] = vw_ref[c:c + LANES, :].T.astype(BF16)

    q = q_ref[...] * (HEAD_DIM ** -0.5)
    q_t = jnp.concatenate([q[:, h * HEAD_DIM:(h + 1) * HEAD_DIM].T for h in range(hpg)],
                          axis=1).astype(BF16)
    t_lane = qi * tq + (lax.broadcasted_iota(jnp.int32, (1, lanes), 1) & (tq - 1))

    first_row = pl.multiple_of(n_cmp - qi * (tq // CMP_STRIDE), 8)
    logit = _dot(kcmp_ref[...], q_t) + tabc_ref[pl.ds(first_row, n_cmp), :]
    e = jnp.exp(logit - jnp.max(logit, axis=0, keepdims=True))
    any_valid = (t_lane >= CMP_BLOCK - 1).astype(F32)
    p = e / jnp.sum(e, axis=0, keepdims=True) * any_valid
    o_cmp = _dot(vcmp_t_ref[...], p.astype(BF16))

    psum = p[:, 0:tq]
    for h in range(1, hpg):
        psum = psum + p[:, h * tq:(h + 1) * tq]
    p_hi = psum.astype(BF16)
    r1 = psum - p_hi.astype(F32)
    p_mid = r1.astype(BF16)
    p_lo = (r1 - p_mid.astype(F32)).astype(BF16)
    ov = overlap_ref[...]
    imp = _dot(ov, p_hi) + _dot(ov, p_mid) + _dot(ov, p_lo)
    cur = lax.shift_right_logical(t_lane[:, 0:tq], int(math.log2(SEL_BLOCK)))
    jb = lax.broadcasted_iota(jnp.int32, (n_sb, tq), 0)
    forced = (jb == 0) | (jb == cur) | (jb == cur - 1)
    score = jnp.where(jb > cur, FUTURE_SCORE, imp + jnp.where(forced, FORCE_BONUS, 0.0))
    rank = jnp.zeros((n_sb, tq), jnp.int32)
    for j2 in range(n_sb):
        row = score[j2:j2 + 1, :]
        rank += ((row > score) | ((row == score) & (j2 < jb))).astype(jnp.int32)
    sel_neg = jnp.where(rank < SEL_TOP, 0.0, NEG).astype(BF16)
    n_aug = ksel_ref.shape[1] - HEAD_DIM
    q_sel = jnp.concatenate([q_t, jnp.concatenate([sel_neg] * hpg, axis=1),
                             jnp.zeros((n_aug - n_sb, lanes), BF16)], axis=0)

    def sweep(k_ref, v_t_ref, q_rhs, lo, edge):
        def rows(kb):
            return pl.ds(pl.multiple_of(kb * ATT_BLK, ATT_BLK), ATT_BLK)

        def scores(kb, m):
            dblk = qi - kb
            tab = jnp.minimum(dblk, TAB_FAR)
            if edge is not None:
                tab = jnp.where(dblk == edge, TAB_EDGE, tab)
            s = _dot(k_ref[rows(kb), :], q_rhs) + tab_ref[tab]
            s_ref[rows(kb), :] = s
            return jnp.maximum(m, jnp.max(s, axis=0, keepdims=True))

        def over_chunks(body, init):
            n = qi + 1 - lo
            out = lax.fori_loop(0, n // 2, lambda i, c: body(lo + 2 * i + 1, body(lo + 2 * i, c)), init)
            return lax.cond(n % 2 == 1, lambda c: body(qi, c), lambda c: c, out)

        m = over_chunks(scores, jnp.full((1, lanes), NEG, F32))

        def accumulate(kb, carry):
            l, acc = carry
            pr = jnp.exp(s_ref[rows(kb), :] - m)
            return (l + jnp.sum(pr, axis=0, keepdims=True),
                    acc + _dot(v_t_ref[:, rows(kb)], pr.astype(BF16)))

        l, acc = over_chunks(accumulate, (jnp.zeros((1, lanes), F32), jnp.zeros((HEAD_DIM, lanes), F32)))
        return acc / l

    o_sel = sweep(ksel_ref, vsel_t_ref, q_sel, 0, None)
    n_back = WINDOW // ATT_BLK
    o_win = sweep(kwin_ref, vwin_t_ref, q_t, jnp.maximum(qi - n_back, 0), n_back)

    for c in range(0, tq, LANES):
        gates_t_ref[:, c:c + LANES] = jax.nn.sigmoid(gates_ref[c:c + LANES, :]).T
    for h in range(hpg):
        head = g * hpg + h
        sl = slice(h * tq, (h + 1) * tq)
        y = (gates_t_ref[pl.ds(head, 1), :] * o_cmp[:, sl]
             + gates_t_ref[pl.ds(N_ATT_HEADS + head, 1), :] * o_sel[:, sl]
             + gates_t_ref[pl.ds(2 * N_ATT_HEADS + head, 1), :] * o_win[:, sl])
        for c in range(0, tq, LANES):
            o_ref[c:c + LANES, h * HEAD_DIM:(h + 1) * HEAD_DIM] = y[:, c:c + LANES].T.astype(o_ref.dtype)

    for src, dst in zip(cast_in, cast_out):
        dst[...] = src[...].astype(BF16)


def _nsa(proj, gates, batch, seq, cmp_pos_k, w_ck1, w_ck2, cmp_pos_v, w_cv1, w_cv2, rel_table, to_cast=()):
    n_qt = seq // ATT_BLK
    hpg = HEADS_PER_GROUP
    n_cmp = seq // CMP_STRIDE
    n_sb = seq // SEL_BLOCK
    n_aug = LANES
    assert CMP_BLOCK == 2 * CMP_STRIDE and n_cmp % 8 == 0 and n_sb % 16 == 0 and n_sb <= n_aug
    assert (ATT_BLK // CMP_STRIDE) % 8 == 0 and n_qt * (ATT_BLK // CMP_STRIDE) <= n_cmp

    sweep_tab, cmp_tab = _bias_tables(rel_table, n_cmp)
    cs = np.arange(n_cmp)[None, :] * CMP_STRIDE
    ss = np.arange(n_sb)[:, None] * SEL_BLOCK
    overlap = (cs < ss + SEL_BLOCK) & (cs + CMP_BLOCK > ss) & (np.arange(n_cmp)[None, :] < n_cmp - 1)
    expand = np.arange(seq)[:, None] // SEL_BLOCK == np.arange(n_aug)[None, :]
    overlap = jnp.asarray(overlap, BF16)
    expand = jnp.asarray(expand, BF16)

    kv_col0 = COL_KV // HEAD_DIM

    def kv_spec(idx):
        return pl.BlockSpec((seq, HEAD_DIM), lambda b, g, qi, idx=idx: (b, kv_col0 + idx * N_KV_GROUPS + g))

    def whole(arr):
        return pl.BlockSpec(arr.shape, lambda b, g, qi, nd=arr.ndim: (0,) * nd)

    in_specs = [
        pl.BlockSpec((ATT_BLK, hpg * HEAD_DIM), lambda b, g, qi: (b * n_qt + qi, g)),
        kv_spec(0), kv_spec(1), kv_spec(2), kv_spec(3), kv_spec(4), kv_spec(5),
        pl.BlockSpec((ATT_BLK, LANES), lambda b, g, qi: (b * n_qt + qi, 0)),
        whole(cmp_pos_k), whole(w_ck1), whole(w_ck2), whole(cmp_pos_v), whole(w_cv1), whole(w_cv2),
        pl.BlockSpec((None,) + sweep_tab.shape[1:], lambda b, g, qi: (g, 0, 0, 0)),
        pl.BlockSpec((None,) + cmp_tab.shape[1:], lambda b, g, qi: (g, 0, 0)),
        whole(overlap), whole(expand),
    ]
    assert len(in_specs) == N_NSA_INPUTS
    n_steps = batch * N_KV_GROUPS * n_qt
    cast_specs = [_stripe_spec(a, n_steps, lambda b, g, qi: (b * N_KV_GROUPS + g) * n_qt + qi) for a in to_cast]
    outs = pl.pallas_call(
        functools.partial(_nsa_body, n_cast=len(to_cast)),
        grid=(batch, N_KV_GROUPS, n_qt),
        in_specs=in_specs + cast_specs,
        out_specs=[pl.BlockSpec((ATT_BLK, hpg * HEAD_DIM), lambda b, g, qi: (b * n_qt + qi, g))] + cast_specs,
        out_shape=[jax.ShapeDtypeStruct((batch * seq, D_ATT), BF16)]
        + [jax.ShapeDtypeStruct(a.shape, BF16) for a in to_cast],
        scratch_shapes=[pltpu.VMEM((n_cmp, HEAD_DIM), BF16), pltpu.VMEM((HEAD_DIM, n_cmp), BF16),
                        pltpu.VMEM((seq, HEAD_DIM + n_aug), BF16), pltpu.VMEM((HEAD_DIM, seq), BF16),
                        pltpu.VMEM((seq, HEAD_DIM), BF16), pltpu.VMEM((HEAD_DIM, seq), BF16),
                        pltpu.VMEM((LANES, ATT_BLK), F32), pltpu.VMEM((seq, hpg * ATT_BLK), F32)],
        compiler_params=pltpu.CompilerParams(
            dimension_semantics=(("arbitrary",) * 3 if to_cast else ("parallel", "parallel", "arbitrary")),
            vmem_limit_bytes=VMEM_LIMIT_BYTES),
        name="nsa",
    )(proj, proj, proj, proj, proj, proj, proj, gates,
      cmp_pos_k, w_ck1.astype(BF16), w_ck2.astype(BF16), cmp_pos_v, w_cv1.astype(BF16), w_cv2.astype(BF16),
      sweep_tab, cmp_tab, overlap, expand, *to_cast)
    return outs[0], outs[1:]


def _pool_body(u_ref, w_ref, scale_ref, o_ref):
    gi = pl.program_id(1)
    u = u_ref[...]
    seq = u.shape[0]
    t = lax.broadcasted_iota(jnp.int32, (seq, 1), 0)

    def shifted(x, k):
        return jnp.where(t >= k, pltpu.roll(x, k, 0), 0.0)

    sums = [u]
    for n in range(int(math.log2(POOL_WINDOWS[-1]))):
        sums.append(sums[-1] + shifted(sums[-1], 2 ** n))
    wsum = sums[int(math.log2(POOL_WINDOWS[0]))]
    for idx in range(1, len(POOL_WINDOWS)):
        wsum = jnp.where(gi >= idx, sums[int(math.log2(POOL_WINDOWS[idx]))], wsum)
    window = POOL_WINDOWS[0]
    for idx in range(1, len(POOL_WINDOWS)):
        window = jnp.where(gi >= idx, POOL_WINDOWS[idx], window)
    cnt = jnp.minimum(t + 1, window).astype(F32)
    d = (wsum / cnt - u).astype(BF16)
    o_ref[...] = (_dot(d, w_ref[0]) * scale_ref[...]).astype(o_ref.dtype)


def _pool(proj, batch, seq, w_pool, pool_scale):
    n_g, dg, _ = w_pool.shape
    assert all(w == 2 ** int(math.log2(w)) for w in POOL_WINDOWS) and list(POOL_WINDOWS) == sorted(POOL_WINDOWS)
    col0 = COL_U // dg
    return pl.pallas_call(
        _pool_body,
        grid=(batch, n_g),
        in_specs=[
            pl.BlockSpec((seq, dg), lambda b, gi: (b, col0 + gi)),
            pl.BlockSpec((1, dg, dg), lambda b, gi: (gi, 0, 0)),
            pl.BlockSpec((1, dg), lambda b, gi: (0, gi)),
        ],
        out_specs=pl.BlockSpec((seq, dg), lambda b, gi: (b, gi)),
        out_shape=jax.ShapeDtypeStruct((batch * seq, n_g * dg), BF16),
        compiler_params=pltpu.CompilerParams(
            dimension_semantics=("parallel", "parallel"), vmem_limit_bytes=VMEM_LIMIT_BYTES),
        name="pool",
    )(proj, w_pool.astype(BF16), pool_scale.reshape(1, -1))


def _outproj_body(x_ref, ya_ref, yp_ref, wa_ref, wp_ref, o_ref):
    o_ref[...] = x_ref[...] + _dot(ya_ref[...], wa_ref[...]) + _dot(yp_ref[...], wp_ref[...])


def _outproj(x, y_att, y_pool, w_out):
    n, d = x.shape
    da = y_att.shape[1]
    dp = y_pool.shape[1]
    w = w_out
    return pl.pallas_call(
        _outproj_body,
        grid=(d // OUT_TN, n // OUT_TM),
        in_specs=[
            pl.BlockSpec((OUT_TM, OUT_TN), lambda j, i: (i, j)),
            pl.BlockSpec((OUT_TM, da), lambda j, i: (i, 0)),
            pl.BlockSpec((OUT_TM, dp), lambda j, i: (i, 0)),
            pl.BlockSpec((da, OUT_TN), lambda j, i: (0, j)),
            pl.BlockSpec((dp, OUT_TN), lambda j, i: (da // dp, j)),
        ],
        out_specs=pl.BlockSpec((OUT_TM, OUT_TN), lambda j, i: (i, j)),
        out_shape=jax.ShapeDtypeStruct((n, d), F32),
        compiler_params=pltpu.CompilerParams(
            dimension_semantics=("parallel", "parallel"), vmem_limit_bytes=VMEM_LIMIT_BYTES),
        name="out_proj",
    )(x, y_att, y_pool, w, w)


def kernel(x, norm_ffn1, w_ffn1_gate, w_ffn1_up, w_ffn1_down, norm_mix, w_in, cmp_pos_k, w_cmp_k1, w_cmp_k2,
           cmp_pos_v, w_cmp_v1, w_cmp_v2, w_pool, pool_scale, w_out, rel_table, norm_ffn2, w_ffn2_gate,
           w_ffn2_up, w_ffn2_down, norm_final):
    batch, seq, d = x.shape
    depth = norm_ffn1.shape[0]
    xf = x.reshape(batch * seq, d)
    gf = norm_final.reshape(1, d)
    for l in range(depth):
        xf, (w_in_t, w_out_b) = _ffn(
            xf, norm_ffn1[l].reshape(1, d), w_ffn1_gate[l].astype(BF16), w_ffn1_up[l].astype(BF16),
            w_ffn1_down[l].astype(BF16), gf, final_norm=False, to_cast=(w_in[l].T, w_out[l]))
        proj, gates = _inproj(xf, norm_mix[l].reshape(1, d), w_in_t)
        y_att, (w2_gate, w2_up, w2_down) = _nsa(
            proj, gates, batch, seq, cmp_pos_k[l], w_cmp_k1[l], w_cmp_k2[l], cmp_pos_v[l], w_cmp_v1[l], w_cmp_v2[l],
            rel_table, to_cast=(w_ffn2_gate[l], w_ffn2_up[l], w_ffn2_down[l]))
        y_pool = _pool(proj, batch, seq, w_pool[l], pool_scale[l])
        xf = _outproj(xf, y_att, y_pool, w_out_b)
        xf, _ = _ffn(xf, norm_ffn2[l].reshape(1, d), w2_gate, w2_up, w2_down, gf, final_norm=(l == depth - 1))
    if depth == 0:
        raise ValueError("depth must be positive")
    return xf.reshape(batch, seq, d)
```

```python
import functools
import math

import jax
import jax.numpy as jnp
import numpy as np
from jax import lax
from jax.experimental import pallas as pl
from jax.experimental.pallas import tpu as pltpu

HEAD_DIM = 128
N_ATT_HEADS = 16
N_KV_GROUPS = 4
HEADS_PER_GROUP = N_ATT_HEADS // N_KV_GROUPS
D_ATT = N_ATT_HEADS * HEAD_DIM
D_KV = N_KV_GROUPS * HEAD_DIM
N_POOL_GROUPS = 4
POOL_WINDOWS = (2, 4, 8, 16)
CMP_BLOCK = 32
CMP_STRIDE = 16
SEL_BLOCK = 64
SEL_TOP = 8
WINDOW = 512
REL_BUCKETS = 32
REL_MAX_DIST = 128
EPS = 1e-6
NEG = -1e30
FORCE_BONUS = 1e4
FUTURE_SCORE = -1e9

LANES = 128
MXU_DIM = 256
VMEM_LIMIT_BYTES = 60 * 2 ** 20

FFN_TM = 512
FFN_TF = 2 * MXU_DIM
FFN_TN = 1024
NORM_ROWS = 32
CAST_ROWS = 16
PROJ_TM = 1024
PROJ_ROWS = 512
PROJ_TN = 512
PROJ_TG = MXU_DIM
ATT_BLK = 256
OUT_TM = 1024
OUT_TN = 1024

BF16 = jnp.bfloat16
F32 = jnp.float32


def _dot(a, b):
    return jnp.dot(a, b, preferred_element_type=F32)


def _dot_nt(a, b):
    return lax.dot_general(a, b, (((1,), (1,)), ((), ())), preferred_element_type=F32)


def _rms(x, g):
    return x * lax.rsqrt(jnp.mean(x * x, axis=-1, keepdims=True) + EPS) * g


def _prefetch_row_tile(x_hbm, xbuf_ref, sem):
    i = pl.program_id(0)
    j = pl.program_id(1)
    tm = xbuf_ref.shape[0]

    def x_copy(tile):
        return pltpu.make_async_copy(x_hbm.at[pl.ds(pl.multiple_of(tile * tm, tm), tm), :], xbuf_ref, sem)

    @pl.when((i == 0) & (j == 0))
    def _():
        x_copy(0).start()

    @pl.when(j == 0)
    def _():
        x_copy(i).wait()

    @pl.when((j == 1) & (i + 1 < pl.num_programs(0)))
    def _():
        x_copy(i + 1).start()


def _ffn_body(*refs, final_norm, n_cast, tf, tail):
    x_hbm, g_ref, wg_ref, wu_ref, wd_ref, gf_ref = refs[:6]
    cast_in = refs[6:6 + n_cast]
    o_ref = refs[6 + n_cast]
    cast_out = refs[7 + n_cast:7 + 2 * n_cast]
    h_ref, x_tile, sem = refs[7 + 2 * n_cast:]
    j = pl.program_id(1)
    last = pl.num_programs(1) - 1
    tm = o_ref.shape[0]
    _prefetch_row_tile(x_hbm, x_tile, sem)

    @pl.when(j == 0)
    def _():
        for r in range(0, tm, NORM_ROWS):
            x = x_tile[r:r + NORM_ROWS, :]
            h_ref[r:r + NORM_ROWS, :] = _rms(x, g_ref[...]).astype(BF16)
            o_ref[r:r + NORM_ROWS, :] = x

    def accumulate(width):
        h = h_ref[...]
        gate = _dot(h, wg_ref[:, :width])
        up = _dot(h, wu_ref[:, :width])
        act = (0.5 * (gate * jax.nn.sigmoid(gate)) * up).astype(BF16)
        for c in range(0, o_ref.shape[1], FFN_TN):
            o_ref[:, c:c + FFN_TN] += _dot(act, wd_ref[:width, c:c + FFN_TN])

    if tail == tf:
        accumulate(tf)
    else:
        pl.when(j < last)(functools.partial(accumulate, tf))
        pl.when(j == last)(functools.partial(accumulate, tail))

    if final_norm:
        @pl.when(j == last)
        def _():
            for r in range(0, tm, NORM_ROWS):
                o_ref[r:r + NORM_ROWS, :] = _rms(o_ref[r:r + NORM_ROWS, :], gf_ref[...])

    for src, dst in zip(cast_in, cast_out):
        dst[...] = src[...].astype(BF16)


def _stripe_spec(a, n_steps, step_of):
    r, c = a.shape
    rows = CAST_ROWS
    while pl.cdiv(r, rows) > n_steps:
        rows += CAST_ROWS
    n_blocks = pl.cdiv(r, rows)
    return pl.BlockSpec((rows, c), lambda *idx: (jnp.minimum(step_of(*idx), n_blocks - 1), 0))


def _ffn(x, g, w_gate, w_up, w_down, gf, *, final_norm, to_cast=()):
    n, d = x.shape
    d_ff = w_gate.shape[1]
    tf = FFN_TF
    n_i, n_j = n // FFN_TM, pl.cdiv(d_ff, tf)
    tail = d_ff - (n_j - 1) * tf
    assert tail % MXU_DIM == 0 and n_j >= 2
    cast_specs = [_stripe_spec(a, n_i * n_j, lambda i, j: i * n_j + j) for a in to_cast]
    outs = pl.pallas_call(
        functools.partial(_ffn_body, final_norm=final_norm, n_cast=len(to_cast), tf=tf, tail=tail),
        grid=(n_i, n_j),
        in_specs=[
            pl.BlockSpec(memory_space=pl.ANY),
            pl.BlockSpec((1, d), lambda i, j: (0, 0)),
            pl.BlockSpec((d, tf), lambda i, j: (0, j)),
            pl.BlockSpec((d, tf), lambda i, j: (0, j)),
            pl.BlockSpec((tf, d), lambda i, j: (j, 0)),
            pl.BlockSpec((1, d), lambda i, j: (0, 0)),
        ] + cast_specs,
        out_specs=[pl.BlockSpec((FFN_TM, d), lambda i, j: (i, 0))] + cast_specs,
        out_shape=[jax.ShapeDtypeStruct((n, d), F32)] + [jax.ShapeDtypeStruct(a.shape, BF16) for a in to_cast],
        scratch_shapes=[pltpu.VMEM((FFN_TM, d), BF16), pltpu.VMEM((FFN_TM, d), F32), pltpu.SemaphoreType.DMA(())],
        compiler_params=pltpu.CompilerParams(
            dimension_semantics=("arbitrary", "arbitrary"), vmem_limit_bytes=VMEM_LIMIT_BYTES),
        name="ffn_final" if final_norm else "ffn",
    )(x, g, w_gate, w_up, w_down, gf, *to_cast)
    return outs[0], outs[1:]


COL_Q = 0
COL_KV = D_ATT
COL_U = D_ATT + 6 * D_KV
D_POOL_COLS = 2048
COL_GATES = COL_U + D_POOL_COLS
N_GATES = 3 * N_ATT_HEADS


def _inproj_body(x_hbm, g_ref, wa_ref, wu_ref, wg_ref, o_ref, og_ref, h_ref, xbuf_ref, sem, *, n_a):
    j = pl.program_id(1)
    _prefetch_row_tile(x_hbm, xbuf_ref, sem)

    @pl.when(j == 0)
    def _():
        for r in range(0, xbuf_ref.shape[0], NORM_ROWS):
            h_ref[r:r + NORM_ROWS, :] = _rms(xbuf_ref[r:r + NORM_ROWS, :], g_ref[...]).astype(BF16)

    def project(w_ref, dst_ref):
        for r in range(0, dst_ref.shape[0], PROJ_ROWS):
            dst_ref[r:r + PROJ_ROWS, :] = _dot_nt(h_ref[r:r + PROJ_ROWS, :], w_ref[...])

    pl.when(j < n_a)(functools.partial(project, wa_ref, o_ref))
    pl.when(j >= n_a)(functools.partial(project, wu_ref, o_ref))
    pl.when(j == pl.num_programs(1) - 1)(functools.partial(project, wg_ref, og_ref))


def _inproj(x, g, w_in_t):
    n, d = x.shape
    assert w_in_t.shape[0] == COL_GATES + N_GATES and N_GATES <= PROJ_TG
    assert COL_U % PROJ_TN == 0 and D_POOL_COLS % PROJ_TN == 0
    wu = w_in_t[COL_U + N_GATES:]
    wg = jnp.pad(w_in_t[COL_U:COL_U + N_GATES], ((0, PROJ_TG - N_GATES), (0, 0)))
    n_a = COL_U // PROJ_TN
    n_u = D_POOL_COLS // PROJ_TN
    return pl.pallas_call(
        functools.partial(_inproj_body, n_a=n_a),
        grid=(n // PROJ_TM, n_a + n_u),
        in_specs=[
            pl.BlockSpec(memory_space=pl.ANY),
            pl.BlockSpec((1, d), lambda i, j: (0, 0)),
            pl.BlockSpec((PROJ_TN, d), lambda i, j: (jnp.minimum(j, n_a - 1), 0)),
            pl.BlockSpec((PROJ_TN, d), lambda i, j: (jnp.maximum(j - n_a, 0), 0)),
            pl.BlockSpec((PROJ_TG, d), lambda i, j: (0, 0), pipeline_mode=pl.Buffered(1)),
        ],
        out_specs=[pl.BlockSpec((PROJ_TM, PROJ_TN), lambda i, j: (i, j)),
                   pl.BlockSpec((PROJ_TM, PROJ_TG), lambda i, j: (i, 0))],
        out_shape=[jax.ShapeDtypeStruct((n, COL_GATES), F32), jax.ShapeDtypeStruct((n, PROJ_TG), F32)],
        scratch_shapes=[pltpu.VMEM((PROJ_TM, d), BF16), pltpu.VMEM((PROJ_TM, d), F32), pltpu.SemaphoreType.DMA(())],
        compiler_params=pltpu.CompilerParams(
            dimension_semantics=("arbitrary", "arbitrary"), vmem_limit_bytes=VMEM_LIMIT_BYTES),
        name="in_proj",
    )(x, g, w_in_t, wu, wg)


def _rel_bucket_np(n):
    max_exact = REL_BUCKETS // 2
    n = np.maximum(n, 0)
    nf = np.maximum(n, 1).astype(np.float32)
    large = max_exact + (np.log(nf / max_exact) / math.log(REL_MAX_DIST / max_exact)
                         * (REL_BUCKETS - max_exact)).astype(np.int32)
    large = np.minimum(large, REL_BUCKETS - 1)
    return np.where(n < max_exact, n, large)


FAR_DIST = int(np.max(np.nonzero(_rel_bucket_np(np.arange(4 * REL_MAX_DIST)) < REL_BUCKETS - 1)[0])) + 1
CMP_ROW_OFF = -(-(FAR_DIST + CMP_BLOCK - 1) // CMP_STRIDE) - 1
TAB_DIAG, TAB_PREV, TAB_FAR, TAB_EDGE = 0, 1, 2, 3


def _pick(rel_table, idx):
    onehot = jnp.asarray(np.arange(REL_BUCKETS)[:, None] == idx[None, :], F32)
    return jnp.einsum("kh,kn->hn", rel_table.astype(F32), onehot, precision=lax.Precision.HIGHEST)


def _group_lanes(t):
    t = t.reshape((N_KV_GROUPS, HEADS_PER_GROUP) + t.shape[1:])
    t = jnp.moveaxis(t, 1, -2)
    return t.reshape(t.shape[:-2] + (HEADS_PER_GROUP * t.shape[-1],))


def _toeplitz_body(p_ref, o_ref):
    blk = o_ref.shape[-1]
    for t in range(o_ref.shape[0]):
        y = pltpu.roll(jnp.broadcast_to(p_ref[t:t + 1, :], (blk, 2 * blk)), 0, 1, stride=1, stride_axis=0)
        o_ref[t] = y[:, :blk]


def _bias_tables(rel_table, n_cmp):
    blk = ATT_BLK
    n_heads = rel_table.shape[1]
    far = REL_BUCKETS - 1
    assert blk >= FAR_DIST and WINDOW % blk == 0
    m = np.arange(2 * blk)
    d = np.where(m < blk, m, m - 2 * blk)
    idx = np.stack([_rel_bucket_np(d), _rel_bucket_np(blk + d), np.full_like(d, far), np.full_like(d, far)])
    mask = np.stack([np.where(d < 0, NEG, 0.0), np.zeros_like(d, np.float64), np.zeros_like(d, np.float64),
                     np.where(d >= 0, NEG, 0.0)]).astype(np.float32)
    profiles = _pick(rel_table, idx.reshape(-1)).reshape(n_heads, 4, 2 * blk) + mask[None]
    sweep = pl.pallas_call(
        _toeplitz_body,
        grid=(N_KV_GROUPS, HEADS_PER_GROUP),
        in_specs=[pl.BlockSpec((None, 4, 2 * blk), lambda g, h: (g * HEADS_PER_GROUP + h, 0, 0))],
        out_specs=pl.BlockSpec((None, 4, blk, blk), lambda g, h: (g, 0, 0, h)),
        out_shape=jax.ShapeDtypeStruct((N_KV_GROUPS, 4, blk, HEADS_PER_GROUP * blk), F32),
        name="bias_tables",
    )(profiles)

    rows = np.arange(n_cmp - CMP_ROW_OFF, n_cmp + (blk - CMP_BLOCK) // CMP_STRIDE + 1)
    dist = np.arange(blk)[None, :] - ((rows[:, None] - n_cmp) * CMP_STRIDE + CMP_BLOCK - 1)
    band = _pick(rel_table, _rel_bucket_np(dist).reshape(-1)).reshape(n_heads, len(rows), blk)
    band = band + np.where(dist < 0, NEG, 0.0).astype(np.float32)[None]
    before = jnp.broadcast_to(rel_table[far].astype(F32)[:, None, None], (n_heads, int(rows[0]), blk))
    after = jnp.full((n_heads, 2 * n_cmp - int(rows[-1]) - 1, blk), NEG, F32)
    cmp_tab = _group_lanes(jnp.concatenate([before, band, after], axis=1))
    return sweep, cmp_tab


def _compress(kv_ref, pos_ref, w1_ref, w2_ref):
    seq, dk = kv_ref.shape
    n_half = seq // CMP_STRIDE
    first = jnp.zeros((n_half, w1_ref.shape[1]), F32)
    second = jnp.zeros((n_half, w1_ref.shape[1]), F32)
    for l in range(CMP_STRIDE):
        rows = kv_ref[pl.ds(l, n_half, stride=CMP_STRIDE), :]
        a = (rows + pos_ref[l:l + 1, :]).astype(BF16)
        b = (rows + pos_ref[CMP_STRIDE + l:CMP_STRIDE + l + 1, :]).astype(BF16)
        first += _dot(a, w1_ref[l * dk:(l + 1) * dk, :])
        second += _dot(b, w1_ref[(CMP_STRIDE + l) * dk:(CMP_STRIDE + l + 1) * dk, :])
    pre = first + pltpu.roll(second, n_half - 1, 0)
    hid = (pre * jax.nn.sigmoid(pre)).astype(BF16)
    return _dot(hid, w2_ref[...])


N_NSA_INPUTS = 18


def _nsa_body(*refs, n_cast):
    (q_ref, kc_ref, vc_ref, ks_ref, vs_ref, kw_ref, vw_ref, gates_ref,
     posk_ref, wk1_ref, wk2_ref, posv_ref, wv1_ref, wv2_ref,
     tab_ref, tabc_ref, overlap_ref, expand_ref) = refs[:N_NSA_INPUTS]
    cast_in = refs[N_NSA_INPUTS:N_NSA_INPUTS + n_cast]
    o_ref = refs[N_NSA_INPUTS + n_cast]
    cast_out = refs[N_NSA_INPUTS + n_cast + 1:N_NSA_INPUTS + 2 * n_cast + 1]
    (kcmp_ref, vcmp_t_ref, ksel_ref, vsel_t_ref, kwin_ref, vwin_t_ref, gates_t_ref,
     s_ref) = refs[N_NSA_INPUTS + 2 * n_cast + 1:]
    g = pl.program_id(1)
    qi = pl.program_id(2)
    tq = ATT_BLK
    hpg = HEADS_PER_GROUP
    lanes = hpg * tq
    n_cmp = kcmp_ref.shape[0]
    n_sb = overlap_ref.shape[0]
    seq = ks_ref.shape[0]
    assert tq & (tq - 1) == 0 and SEL_BLOCK & (SEL_BLOCK - 1) == 0

    @pl.when(qi == 0)
    def _():
        kcmp_ref[...] = _compress(kc_ref, posk_ref, wk1_ref, wk2_ref).astype(BF16)
        vcmp_t_ref[...] = _compress(vc_ref, posv_ref, wv1_ref, wv2_ref).astype(BF16).T
        for c in range(0, seq, LANES):
            ksel_ref[c:c + LANES, 0:HEAD_DIM] = ks_ref[c:c + LANES, :].astype(BF16)
            ksel_ref[c:c + LANES, HEAD_DIM:] = expand_ref[c:c + LANES, :]
            kwin_ref[c:c + LANES, :] = kw_ref[c:c + LANES, :].astype(BF16)
            vsel_t_ref[:, c:c + LANES] = vs_ref[c:c + LANES, :].T.astype(BF16)
            vwin_t_ref[:, c:c + LANES] = vw_ref[c:c + LANES, :].T.astype(BF16)

    q = q_ref[...] * (HEAD_DIM ** -0.5)
    q_t = jnp.concatenate([q[:, h * HEAD_DIM:(h + 1) * HEAD_DIM].T for h in range(hpg)],
                          axis=1).astype(BF16)
    t_lane = qi * tq + (lax.broadcasted_iota(jnp.int32, (1, lanes), 1) & (tq - 1))

    first_row = pl.multiple_of(n_cmp - qi * (tq // CMP_STRIDE), 8)
    logit = _dot(kcmp_ref[...], q_t) + tabc_ref[pl.ds(first_row, n_cmp), :]
    e = jnp.exp(logit - jnp.max(logit, axis=0, keepdims=True))
    any_valid = (t_lane >= CMP_BLOCK - 1).astype(F32)
    p = e / jnp.sum(e, axis=0, keepdims=True) * any_valid
    o_cmp = _dot(vcmp_t_ref[...], p.astype(BF16))

    psum = p[:, 0:tq]
    for h in range(1, hpg):
        psum = psum + p[:, h * tq:(h + 1) * tq]
    p_hi = psum.astype(BF16)
    r1 = psum - p_hi.astype(F32)
    p_mid = r1.astype(BF16)
    p_lo = (r1 - p_mid.astype(F32)).astype(BF16)
    ov = overlap_ref[...]
    imp = _dot(ov, p_hi) + _dot(ov, p_mid) + _dot(ov, p_lo)
    cur = lax.shift_right_logical(t_lane[:, 0:tq], int(math.log2(SEL_BLOCK)))
    jb = lax.broadcasted_iota(jnp.int32, (n_sb, tq), 0)
    forced = (jb == 0) | (jb == cur) | (jb == cur - 1)
    score = jnp.where(jb > cur, FUTURE_SCORE, imp + jnp.where(forced, FORCE_BONUS, 0.0))
    rank = jnp.zeros((n_sb, tq), jnp.int32)
    for j2 in range(n_sb):
        row = score[j2:j2 + 1, :]
        rank += ((row > score) | ((row == score) & (j2 < jb))).astype(jnp.int32)
    sel_neg = jnp.where(rank < SEL_TOP, 0.0, NEG).astype(BF16)
    n_aug = ksel_ref.shape[1] - HEAD_DIM
    q_sel = jnp.concatenate([q_t, jnp.concatenate([sel_neg] * hpg, axis=1),
                             jnp.zeros((n_aug - n_sb, lanes), BF16)], axis=0)

    def sweep(k_ref, v_t_ref, q_rhs, lo, edge):
        def rows(kb):
            return pl.ds(pl.multiple_of(kb * ATT_BLK, ATT_BLK), ATT_BLK)

        def scores(kb, m):
            dblk = qi - kb
            tab = jnp.minimum(dblk, TAB_FAR)
            if edge is not None:
                tab = jnp.where(dblk == edge, TAB_EDGE, tab)
            s = _dot(k_ref[rows(kb), :], q_rhs) + tab_ref[tab]
            s_ref[rows(kb), :] = s
            return jnp.maximum(m, jnp.max(s, axis=0, keepdims=True))

        def over_chunks(body, init):
            n = qi + 1 - lo
            out = lax.fori_loop(0, n // 2, lambda i, c: body(lo + 2 * i + 1, body(lo + 2 * i, c)), init)
            return lax.cond(n % 2 == 1, lambda c: body(qi, c), lambda c: c, out)

        m = over_chunks(scores, jnp.full((1, lanes), NEG, F32))

        def accumulate(kb, carry):
            l, acc = carry
            pr = jnp.exp(s_ref[rows(kb), :] - m)
            return (l + jnp.sum(pr, axis=0, keepdims=True),
                    acc + _dot(v_t_ref[:, rows(kb)], pr.astype(BF16)))

        l, acc = over_chunks(accumulate, (jnp.zeros((1, lanes), F32), jnp.zeros((HEAD_DIM, lanes), F32)))
        return acc / l

    o_sel = sweep(ksel_ref, vsel_t_ref, q_sel, 0, None)
    n_back = WINDOW // ATT_BLK
    o_win = sweep(kwin_ref, vwin_t_ref, q_t, jnp.maximum(qi - n_back, 0), n_back)

    for c in range(0, tq, LANES):
        gates_t_ref[:, c:c + LANES] = jax.nn.sigmoid(gates_ref[c:c + LANES, :]).T
    for h in range(hpg):
        head = g * hpg + h
        sl = slice(h * tq, (h + 1) * tq)
        y = (gates_t_ref[pl.ds(head, 1), :] * o_cmp[:, sl]
             + gates_t_ref[pl.ds(N_ATT_HEADS + head, 1), :] * o_sel[:, sl]
             + gates_t_ref[pl.ds(2 * N_ATT_HEADS + head, 1), :] * o_win[:, sl])
        for c in range(0, tq, LANES):
            o_ref[c:c + LANES, h * HEAD_DIM:(h + 1) * HEAD_DIM] = y[:, c:c + LANES].T.astype(o_ref.dtype)

    for src, dst in zip(cast_in, cast_out):
        dst[...] = src[...].astype(BF16)


def _nsa(proj, gates, batch, seq, cmp_pos_k, w_ck1, w_ck2, cmp_pos_v, w_cv1, w_cv2, rel_table, to_cast=()):
    n_qt = seq // ATT_BLK
    hpg = HEADS_PER_GROUP
    n_cmp = seq // CMP_STRIDE
    n_sb = seq // SEL_BLOCK
    n_aug = LANES
    assert CMP_BLOCK == 2 * CMP_STRIDE and n_cmp % 8 == 0 and n_sb % 16 == 0 and n_sb <= n_aug
    assert (ATT_BLK // CMP_STRIDE) % 8 == 0 and n_qt * (ATT_BLK // CMP_STRIDE) <= n_cmp

    sweep_tab, cmp_tab = _bias_tables(rel_table, n_cmp)
    cs = np.arange(n_cmp)[None, :] * CMP_STRIDE
    ss = np.arange(n_sb)[:, None] * SEL_BLOCK
    overlap = (cs < ss + SEL_BLOCK) & (cs + CMP_BLOCK > ss) & (np.arange(n_cmp)[None, :] < n_cmp - 1)
    expand = np.arange(seq)[:, None] // SEL_BLOCK == np.arange(n_aug)[None, :]
    overlap = jnp.asarray(overlap, BF16)
    expand = jnp.asarray(expand, BF16)

    kv_col0 = COL_KV // HEAD_DIM

    def kv_spec(idx):
        return pl.BlockSpec((seq, HEAD_DIM), lambda b, g, qi, idx=idx: (b, kv_col0 + idx * N_KV_GROUPS + g))

    def whole(arr):
        return pl.BlockSpec(arr.shape, lambda b, g, qi, nd=arr.ndim: (0,) * nd)

    in_specs = [
        pl.BlockSpec((ATT_BLK, hpg * HEAD_DIM), lambda b, g, qi: (b * n_qt + qi, g)),
        kv_spec(0), kv_spec(1), kv_spec(2), kv_spec(3), kv_spec(4), kv_spec(5),
        pl.BlockSpec((ATT_BLK, LANES), lambda b, g, qi: (b * n_qt + qi, 0)),
        whole(cmp_pos_k), whole(w_ck1), whole(w_ck2), whole(cmp_pos_v), whole(w_cv1), whole(w_cv2),
        pl.BlockSpec((None,) + sweep_tab.shape[1:], lambda b, g, qi: (g, 0, 0, 0)),
        pl.BlockSpec((None,) + cmp_tab.shape[1:], lambda b, g, qi: (g, 0, 0)),
        whole(overlap), whole(expand),
    ]
    assert len(in_specs) == N_NSA_INPUTS
    n_steps = batch * N_KV_GROUPS * n_qt
    cast_specs = [_stripe_spec(a, n_steps, lambda b, g, qi: (b * N_KV_GROUPS + g) * n_qt + qi) for a in to_cast]
    outs = pl.pallas_call(
        functools.partial(_nsa_body, n_cast=len(to_cast)),
        grid=(batch, N_KV_GROUPS, n_qt),
        in_specs=in_specs + cast_specs,
        out_specs=[pl.BlockSpec((ATT_BLK, hpg * HEAD_DIM), lambda b, g, qi: (b * n_qt + qi, g))] + cast_specs,
        out_shape=[jax.ShapeDtypeStruct((batch * seq, D_ATT), BF16)]
        + [jax.ShapeDtypeStruct(a.shape, BF16) for a in to_cast],
        scratch_shapes=[pltpu.VMEM((n_cmp, HEAD_DIM), BF16), pltpu.VMEM((HEAD_DIM, n_cmp), BF16),
                        pltpu.VMEM((seq, HEAD_DIM + n_aug), BF16), pltpu.VMEM((HEAD_DIM, seq), BF16),
                        pltpu.VMEM((seq, HEAD_DIM), BF16), pltpu.VMEM((HEAD_DIM, seq), BF16),
                        pltpu.VMEM((LANES, ATT_BLK), F32), pltpu.VMEM((seq, hpg * ATT_BLK), F32)],
        compiler_params=pltpu.CompilerParams(
            dimension_semantics=(("arbitrary",) * 3 if to_cast else ("parallel", "parallel", "arbitrary")),
            vmem_limit_bytes=VMEM_LIMIT_BYTES),
        name="nsa",
    )(proj, proj, proj, proj, proj, proj, proj, gates,
      cmp_pos_k, w_ck1.astype(BF16), w_ck2.astype(BF16), cmp_pos_v, w_cv1.astype(BF16), w_cv2.astype(BF16),
      sweep_tab, cmp_tab, overlap, expand, *to_cast)
    return outs[0], outs[1:]


def _pool_body(u_ref, w_ref, scale_ref, o_ref):
    gi = pl.program_id(1)
    u = u_ref[...]
    seq = u.shape[0]
    t = lax.broadcasted_iota(jnp.int32, (seq, 1), 0)

    def shifted(x, k):
        return jnp.where(t >= k, pltpu.roll(x, k, 0), 0.0)

    sums = [u]
    for n in range(int(math.log2(POOL_WINDOWS[-1]))):
        sums.append(sums[-1] + shifted(sums[-1], 2 ** n))
    wsum = sums[int(math.log2(POOL_WINDOWS[0]))]
    for idx in range(1, len(POOL_WINDOWS)):
        wsum = jnp.where(gi >= idx, sums[int(math.log2(POOL_WINDOWS[idx]))], wsum)
    window = POOL_WINDOWS[0]
    for idx in range(1, len(POOL_WINDOWS)):
        window = jnp.where(gi >= idx, POOL_WINDOWS[idx], window)
    cnt = jnp.minimum(t + 1, window).astype(F32)
    d = (wsum / cnt - u).astype(BF16)
    o_ref[...] = (_dot(d, w_ref[0]) * scale_ref[...]).astype(o_ref.dtype)


def _pool(proj, batch, seq, w_pool, pool_scale):
    n_g, dg, _ = w_pool.shape
    assert all(w == 2 ** int(math.log2(w)) for w in POOL_WINDOWS) and list(POOL_WINDOWS) == sorted(POOL_WINDOWS)
    col0 = COL_U // dg
    return pl.pallas_call(
        _pool_body,
        grid=(batch, n_g),
        in_specs=[
            pl.BlockSpec((seq, dg), lambda b, gi: (b, col0 + gi)),
            pl.BlockSpec((1, dg, dg), lambda b, gi: (gi, 0, 0)),
            pl.BlockSpec((1, dg), lambda b, gi: (0, gi)),
        ],
        out_specs=pl.BlockSpec((seq, dg), lambda b, gi: (b, gi)),
        out_shape=jax.ShapeDtypeStruct((batch * seq, n_g * dg), BF16),
        compiler_params=pltpu.CompilerParams(
            dimension_semantics=("parallel", "parallel"), vmem_limit_bytes=VMEM_LIMIT_BYTES),
        name="pool",
    )(proj, w_pool.astype(BF16), pool_scale.reshape(1, -1))


def _outproj_body(x_ref, ya_ref, yp_ref, wa_ref, wp_ref, o_ref):
    o_ref[...] = x_ref[...] + _dot(ya_ref[...], wa_ref[...]) + _dot(yp_ref[...], wp_ref[...])


def _outproj(x, y_att, y_pool, w_out):
    n, d = x.shape
    da = y_att.shape[1]
    dp = y_pool.shape[1]
    w = w_out
    return pl.pallas_call(
        _outproj_body,
        grid=(d // OUT_TN, n // OUT_TM),
        in_specs=[
            pl.BlockSpec((OUT_TM, OUT_TN), lambda j, i: (i, j)),
            pl.BlockSpec((OUT_TM, da), lambda j, i: (i, 0)),
            pl.BlockSpec((OUT_TM, dp), lambda j, i: (i, 0)),
            pl.BlockSpec((da, OUT_TN), lambda j, i: (0, j)),
            pl.BlockSpec((dp, OUT_TN), lambda j, i: (da // dp, j)),
        ],
        out_specs=pl.BlockSpec((OUT_TM, OUT_TN), lambda j, i: (i, j)),
        out_shape=jax.ShapeDtypeStruct((n, d), F32),
        compiler_params=pltpu.CompilerParams(
            dimension_semantics=("parallel", "parallel"), vmem_limit_bytes=VMEM_LIMIT_BYTES),
        name="out_proj",
    )(x, y_att, y_pool, w, w)


def kernel(x, norm_ffn1, w_ffn1_gate, w_ffn1_up, w_ffn1_down, norm_mix, w_in, cmp_pos_k, w_cmp_k1, w_cmp_k2,
           cmp_pos_v, w_cmp_v1, w_cmp_v2, w_pool, pool_scale, w_out, rel_table, norm_ffn2, w_ffn2_gate,
           w_ffn2_up, w_ffn2_down, norm_final):
    batch, seq, d = x.shape
    depth = norm_ffn1.shape[0]
    xf = x.reshape(batch * seq, d)
    gf = norm_final.reshape(1, d)
    for l in range(depth):
        xf, (w_in_t, w_out_b) = _ffn(
            xf, norm_ffn1[l].reshape(1, d), w_ffn1_gate[l].astype(BF16), w_ffn1_up[l].astype(BF16),
            w_ffn1_down[l].astype(BF16), gf, final_norm=False, to_cast=(w_in[l].T, w_out[l]))
        proj, gates = _inproj(xf, norm_mix[l].reshape(1, d), w_in_t)
        y_att, (w2_gate, w2_up, w2_down) = _nsa(
            proj, gates, batch, seq, cmp_pos_k[l], w_cmp_k1[l], w_cmp_k2[l], cmp_pos_v[l], w_cmp_v1[l], w_cmp_v2[l],
            rel_table, to_cast=(w_ffn2_gate[l], w_ffn2_up[l], w_ffn2_down[l]))
        y_pool = _pool(proj, batch, seq, w_pool[l], pool_scale[l])
        xf = _outproj(xf, y_att, y_pool, w_out_b)
        xf, _ = _ffn(xf, norm_ffn2[l].reshape(1, d), w2_gate, w2_up, w2_down, gf, final_norm=(l == depth - 1))
    if depth == 0:
        raise ValueError("depth must be positive")
    return xf.reshape(batch, seq, d)
```

```python
import functools
import math

import jax
import jax.numpy as jnp
import numpy as np
from jax import lax
from jax.experimental import pallas as pl
from jax.experimental.pallas import tpu as pltpu

HEAD_DIM = 128
N_ATT_HEADS = 16
N_KV_GROUPS = 4
HEADS_PER_GROUP = N_ATT_HEADS // N_KV_GROUPS
D_ATT = N_ATT_HEADS * HEAD_DIM
D_KV = N_KV_GROUPS * HEAD_DIM
N_POOL_GROUPS = 4
POOL_WINDOWS = (2, 4, 8, 16)
CMP_BLOCK = 32
CMP_STRIDE = 16
SEL_BLOCK = 64
SEL_TOP = 8
WINDOW = 512
REL_BUCKETS = 32
REL_MAX_DIST = 128
EPS = 1e-6
NEG = -1e30
FORCE_BONUS = 1e4
FUTURE_SCORE = -1e9

LANES = 128
MXU_DIM = 256
VMEM_LIMIT_BYTES = 60 * 2 ** 20

FFN_TM = 512
FFN_TF = 2 * MXU_DIM
FFN_TN = 1024
NORM_ROWS = 32
CAST_ROWS = 16
PROJ_TM = 1024
PROJ_ROWS = 512
PROJ_TN = 512
PROJ_TG = MXU_DIM
ATT_BLK = 256
OUT_TM = 1024
OUT_TN = 1024

BF16 = jnp.bfloat16
F32 = jnp.float32


def _dot(a, b):
    return jnp.dot(a, b, preferred_element_type=F32)


def _dot_nt(a, b):
    return lax.dot_general(a, b, (((1,), (1,)), ((), ())), preferred_element_type=F32)


def _rms(x, g):
    return x * lax.rsqrt(jnp.mean(x * x, axis=-1, keepdims=True) + EPS) * g


def _prefetch_row_tile(x_hbm, xbuf_ref, sem):
    i = pl.program_id(0)
    j = pl.program_id(1)
    tm = xbuf_ref.shape[0]

    def x_copy(tile):
        return pltpu.make_async_copy(x_hbm.at[pl.ds(pl.multiple_of(tile * tm, tm), tm), :], xbuf_ref, sem)

    @pl.when((i == 0) & (j == 0))
    def _():
        x_copy(0).start()

    @pl.when(j == 0)
    def _():
        x_copy(i).wait()

    @pl.when((j == 1) & (i + 1 < pl.num_programs(0)))
    def _():
        x_copy(i + 1).start()


def _ffn_body(*refs, final_norm, n_cast, tf, tail):
    x_hbm, g_ref, wg_ref, wu_ref, wd_ref, gf_ref = refs[:6]
    cast_in = refs[6:6 + n_cast]
    o_ref = refs[6 + n_cast]
    cast_out = refs[7 + n_cast:7 + 2 * n_cast]
    h_ref, x_tile, sem = refs[7 + 2 * n_cast:]
    j = pl.program_id(1)
    last = pl.num_programs(1) - 1
    tm = o_ref.shape[0]
    _prefetch_row_tile(x_hbm, x_tile, sem)

    @pl.when(j == 0)
    def _():
        for r in range(0, tm, NORM_ROWS):
            x = x_tile[r:r + NORM_ROWS, :]
            h_ref[r:r + NORM_ROWS, :] = _rms(x, g_ref[...]).astype(BF16)
            o_ref[r:r + NORM_ROWS, :] = x

    def accumulate(width):
        h = h_ref[...]
        gate = _dot(h, wg_ref[:, :width])
        up = _dot(h, wu_ref[:, :width])
        act = (0.5 * (gate * jax.nn.sigmoid(gate)) * up).astype(BF16)
        for c in range(0, o_ref.shape[1], FFN_TN):
            o_ref[:, c:c + FFN_TN] += _dot(act, wd_ref[:width, c:c + FFN_TN])

    if tail == tf:
        accumulate(tf)
    else:
        pl.when(j < last)(functools.partial(accumulate, tf))
        pl.when(j == last)(functools.partial(accumulate, tail))

    if final_norm:
        @pl.when(j == last)
        def _():
            for r in range(0, tm, NORM_ROWS):
                o_ref[r:r + NORM_ROWS, :] = _rms(o_ref[r:r + NORM_ROWS, :], gf_ref[...])

    for src, dst in zip(cast_in, cast_out):
        dst[...] = src[...].astype(BF16)


def _stripe_spec(a, n_steps, step_of):
    r, c = a.shape
    rows = CAST_ROWS
    while pl.cdiv(r, rows) > n_steps:
        rows += CAST_ROWS
    n_blocks = pl.cdiv(r, rows)
    return pl.BlockSpec((rows, c), lambda *idx: (jnp.minimum(step_of(*idx), n_blocks - 1), 0))


def _ffn(x, g, w_gate, w_up, w_down, gf, *, final_norm, to_cast=()):
    n, d = x.shape
    d_ff = w_gate.shape[1]
    tf = FFN_TF
    n_i, n_j = n // FFN_TM, pl.cdiv(d_ff, tf)
    tail = d_ff - (n_j - 1) * tf
    assert tail % MXU_DIM == 0 and n_j >= 2
    cast_specs = [_stripe_spec(a, n_i * n_j, lambda i, j: i * n_j + j) for a in to_cast]
    outs = pl.pallas_call(
        functools.partial(_ffn_body, final_norm=final_norm, n_cast=len(to_cast), tf=tf, tail=tail),
        grid=(n_i, n_j),
        in_specs=[
            pl.BlockSpec(memory_space=pl.ANY),
            pl.BlockSpec((1, d), lambda i, j: (0, 0)),
            pl.BlockSpec((d, tf), lambda i, j: (0, j)),
            pl.BlockSpec((d, tf), lambda i, j: (0, j)),
            pl.BlockSpec((tf, d), lambda i, j: (j, 0)),
            pl.BlockSpec((1, d), lambda i, j: (0, 0)),
        ] + cast_specs,
        out_specs=[pl.BlockSpec((FFN_TM, d), lambda i, j: (i, 0))] + cast_specs,
        out_shape=[jax.ShapeDtypeStruct((n, d), F32)] + [jax.ShapeDtypeStruct(a.shape, BF16) for a in to_cast],
        scratch_shapes=[pltpu.VMEM((FFN_TM, d), BF16), pltpu.VMEM((FFN_TM, d), F32), pltpu.SemaphoreType.DMA(())],
        compiler_params=pltpu.CompilerParams(
            dimension_semantics=("arbitrary", "arbitrary"), vmem_limit_bytes=VMEM_LIMIT_BYTES),
        name="ffn_final" if final_norm else "ffn",
    )(x, g, w_gate, w_up, w_down, gf, *to_cast)
    return outs[0], outs[1:]


COL_Q = 0
COL_KV = D_ATT
COL_U = D_ATT + 6 * D_KV
D_POOL_COLS = 2048
COL_GATES = COL_U + D_POOL_COLS
N_GATES = 3 * N_ATT_HEADS


def _inproj_body(x_hbm, g_ref, wa_ref, wu_ref, wg_ref, o_ref, og_ref, h_ref, xbuf_ref, sem, *, n_a):
    j = pl.program_id(1)
    _prefetch_row_tile(x_hbm, xbuf_ref, sem)

    @pl.when(j == 0)
    def _():
        for r in range(0, xbuf_ref.shape[0], NORM_ROWS):
            h_ref[r:r + NORM_ROWS, :] = _rms(xbuf_ref[r:r + NORM_ROWS, :], g_ref[...]).astype(BF16)

    def project(w_ref, dst_ref):
        for r in range(0, dst_ref.shape[0], PROJ_ROWS):
            dst_ref[r:r + PROJ_ROWS, :] = _dot_nt(h_ref[r:r + PROJ_ROWS, :], w_ref[...])

    pl.when(j < n_a)(functools.partial(project, wa_ref, o_ref))
    pl.when(j >= n_a)(functools.partial(project, wu_ref, o_ref))
    pl.when(j == pl.num_programs(1) - 1)(functools.partial(project, wg_ref, og_ref))


def _inproj(x, g, w_in_t):
    n, d = x.shape
    assert w_in_t.shape[0] == COL_GATES + N_GATES and N_GATES <= PROJ_TG
    assert COL_U % PROJ_TN == 0 and D_POOL_COLS % PROJ_TN == 0
    wu = w_in_t[COL_U + N_GATES:]
    wg = jnp.pad(w_in_t[COL_U:COL_U + N_GATES], ((0, PROJ_TG - N_GATES), (0, 0)))
    n_a = COL_U // PROJ_TN
    n_u = D_POOL_COLS // PROJ_TN
    return pl.pallas_call(
        functools.partial(_inproj_body, n_a=n_a),
        grid=(n // PROJ_TM, n_a + n_u),
        in_specs=[
            pl.BlockSpec(memory_space=pl.ANY),
            pl.BlockSpec((1, d), lambda i, j: (0, 0)),
            pl.BlockSpec((PROJ_TN, d), lambda i, j: (jnp.minimum(j, n_a - 1), 0)),
            pl.BlockSpec((PROJ_TN, d), lambda i, j: (jnp.maximum(j - n_a, 0), 0)),
            pl.BlockSpec((PROJ_TG, d), lambda i, j: (0, 0), pipeline_mode=pl.Buffered(1)),
        ],
        out_specs=[pl.BlockSpec((PROJ_TM, PROJ_TN), lambda i, j: (i, j)),
                   pl.BlockSpec((PROJ_TM, PROJ_TG), lambda i, j: (i, 0))],
        out_shape=[jax.ShapeDtypeStruct((n, COL_GATES), F32), jax.ShapeDtypeStruct((n, PROJ_TG), F32)],
        scratch_shapes=[pltpu.VMEM((PROJ_TM, d), BF16), pltpu.VMEM((PROJ_TM, d), F32), pltpu.SemaphoreType.DMA(())],
        compiler_params=pltpu.CompilerParams(
            dimension_semantics=("arbitrary", "arbitrary"), vmem_limit_bytes=VMEM_LIMIT_BYTES),
        name="in_proj",
    )(x, g, w_in_t, wu, wg)


def _rel_bucket_np(n):
    max_exact = REL_BUCKETS // 2
    n = np.maximum(n, 0)
    nf = np.maximum(n, 1).astype(np.float32)
    large = max_exact + (np.log(nf / max_exact) / math.log(REL_MAX_DIST / max_exact)
                         * (REL_BUCKETS - max_exact)).astype(np.int32)
    large = np.minimum(large, REL_BUCKETS - 1)
    return np.where(n < max_exact, n, large)


FAR_DIST = int(np.max(np.nonzero(_rel_bucket_np(np.arange(4 * REL_MAX_DIST)) < REL_BUCKETS - 1)[0])) + 1
CMP_ROW_OFF = -(-(FAR_DIST + CMP_BLOCK - 1) // CMP_STRIDE) - 1
TAB_DIAG, TAB_PREV, TAB_FAR, TAB_EDGE = 0, 1, 2, 3


def _pick(rel_table, idx):
    onehot = jnp.asarray(np.arange(REL_BUCKETS)[:, None] == idx[None, :], F32)
    return jnp.einsum("kh,kn->hn", rel_table.astype(F32), onehot, precision=lax.Precision.HIGHEST)


def _group_lanes(t):
    t = t.reshape((N_KV_GROUPS, HEADS_PER_GROUP) + t.shape[1:])
    t = jnp.moveaxis(t, 1, -2)
    return t.reshape(t.shape[:-2] + (HEADS_PER_GROUP * t.shape[-1],))


def _toeplitz_body(p_ref, o_ref):
    blk = o_ref.shape[-1]
    for t in range(o_ref.shape[0]):
        y = pltpu.roll(jnp.broadcast_to(p_ref[t:t + 1, :], (blk, 2 * blk)), 0, 1, stride=1, stride_axis=0)
        o_ref[t] = y[:, :blk]


def _bias_tables(rel_table, n_cmp):
    blk = ATT_BLK
    n_heads = rel_table.shape[1]
    far = REL_BUCKETS - 1
    assert blk >= FAR_DIST and WINDOW % blk == 0
    m = np.arange(2 * blk)
    d = np.where(m < blk, m, m - 2 * blk)
    idx = np.stack([_rel_bucket_np(d), _rel_bucket_np(blk + d), np.full_like(d, far), np.full_like(d, far)])
    mask = np.stack([np.where(d < 0, NEG, 0.0), np.zeros_like(d, np.float64), np.zeros_like(d, np.float64),
                     np.where(d >= 0, NEG, 0.0)]).astype(np.float32)
    profiles = _pick(rel_table, idx.reshape(-1)).reshape(n_heads, 4, 2 * blk) + mask[None]
    sweep = pl.pallas_call(
        _toeplitz_body,
        grid=(N_KV_GROUPS, HEADS_PER_GROUP),
        in_specs=[pl.BlockSpec((None, 4, 2 * blk), lambda g, h: (g * HEADS_PER_GROUP + h, 0, 0))],
        out_specs=pl.BlockSpec((None, 4, blk, blk), lambda g, h: (g, 0, 0, h)),
        out_shape=jax.ShapeDtypeStruct((N_KV_GROUPS, 4, blk, HEADS_PER_GROUP * blk), F32),
        name="bias_tables",
    )(profiles)

    rows = np.arange(n_cmp - CMP_ROW_OFF, n_cmp + (blk - CMP_BLOCK) // CMP_STRIDE + 1)
    dist = np.arange(blk)[None, :] - ((rows[:, None] - n_cmp) * CMP_STRIDE + CMP_BLOCK - 1)
    band = _pick(rel_table, _rel_bucket_np(dist).reshape(-1)).reshape(n_heads, len(rows), blk)
    band = band + np.where(dist < 0, NEG, 0.0).astype(np.float32)[None]
    before = jnp.broadcast_to(rel_table[far].astype(F32)[:, None, None], (n_heads, int(rows[0]), blk))
    after = jnp.full((n_heads, 2 * n_cmp - int(rows[-1]) - 1, blk), NEG, F32)
    cmp_tab = _group_lanes(jnp.concatenate([before, band, after], axis=1))
    return sweep, cmp_tab


def _compress(kv_ref, pos_ref, w1_ref, w2_ref):
    seq, dk = kv_ref.shape
    n_half = seq // CMP_STRIDE
    first = jnp.zeros((n_half, w1_ref.shape[1]), F32)
    second = jnp.zeros((n_half, w1_ref.shape[1]), F32)
    for l in range(CMP_STRIDE):
        rows = kv_ref[pl.ds(l, n_half, stride=CMP_STRIDE), :]
        a = (rows + pos_ref[l:l + 1, :]).astype(BF16)
        b = (rows + pos_ref[CMP_STRIDE + l:CMP_STRIDE + l + 1, :]).astype(BF16)
        first += _dot(a, w1_ref[l * dk:(l + 1) * dk, :])
        second += _dot(b, w1_ref[(CMP_STRIDE + l) * dk:(CMP_STRIDE + l + 1) * dk, :])
    pre = first + pltpu.roll(second, n_half - 1, 0)
    hid = (pre * jax.nn.sigmoid(pre)).astype(BF16)
    return _dot(hid, w2_ref[...])


N_NSA_INPUTS = 18


def _nsa_body(*refs, n_cast):
    (q_ref, kc_ref, vc_ref, ks_ref, vs_ref, kw_ref, vw_ref, gates_ref,
     posk_ref, wk1_ref, wk2_ref, posv_ref, wv1_ref, wv2_ref,
     tab_ref, tabc_ref, overlap_ref, expand_ref) = refs[:N_NSA_INPUTS]
    cast_in = refs[N_NSA_INPUTS:N_NSA_INPUTS + n_cast]
    o_ref = refs[N_NSA_INPUTS + n_cast]
    cast_out = refs[N_NSA_INPUTS + n_cast + 1:N_NSA_INPUTS + 2 * n_cast + 1]
    (kcmp_ref, vcmp_t_ref, ksel_ref, vsel_t_ref, kwin_ref, vwin_t_ref, gates_t_ref,
     s_ref) = refs[N_NSA_INPUTS + 2 * n_cast + 1:]
    g = pl.program_id(1)
    qi = pl.program_id(2)
    tq = ATT_BLK
    hpg = HEADS_PER_GROUP
    lanes = hpg * tq
    n_cmp = kcmp_ref.shape[0]
    n_sb = overlap_ref.shape[0]
    seq = ks_ref.shape[0]
    assert tq & (tq - 1) == 0 and SEL_BLOCK & (SEL_BLOCK - 1) == 0

    @pl.when(qi == 0)
    def _():
        kcmp_ref[...] = _compress(kc_ref, posk_ref, wk1_ref, wk2_ref).astype(BF16)
        vcmp_t_ref[...] = _compress(vc_ref, posv_ref, wv1_ref, wv2_ref).astype(BF16).T
        for c in range(0, seq, LANES):
            ksel_ref[c:c + LANES, 0:HEAD_DIM] = ks_ref[c:c + LANES, :].astype(BF16)
            ksel_ref[c:c + LANES, HEAD_DIM:] = expand_ref[c:c + LANES, :]
            kwin_ref[c:c + LANES, :] = kw_ref[c:c + LANES, :].astype(BF16)
            vsel_t_ref[:, c:c + LANES] = vs_ref[c:c + LANES, :].T.astype(BF16)
            vwin_t_ref[:, c:c + LANES] = vw_ref[c:c + LANES, :].T.astype(BF16)

    q = q_ref[...] * (HEAD_DIM ** -0.5)
    q_t = jnp.concatenate([q[:, h * HEAD_DIM:(h + 1) * HEAD_DIM].T for h in range(hpg)],
                          axis=1).astype(BF16)
    t_lane = qi * tq + (lax.broadcasted_iota(jnp.int32, (1, lanes), 1) & (tq - 1))

    first_row = pl.multiple_of(n_cmp - qi * (tq // CMP_STRIDE), 8)
    logit = _dot(kcmp_ref[...], q_t) + tabc_ref[pl.ds(first_row, n_cmp), :]
    e = jnp.exp(logit - jnp.max(logit, axis=0, keepdims=True))
    any_valid = (t_lane >= CMP_BLOCK - 1).astype(F32)
    p = e / jnp.sum(e, axis=0, keepdims=True) * any_valid
    o_cmp = _dot(vcmp_t_ref[...], p.astype(BF16))

    psum = p[:, 0:tq]
    for h in range(1, hpg):
        psum = psum + p[:, h * tq:(h + 1) * tq]
    p_hi = psum.astype(BF16)
    r1 = psum - p_hi.astype(F32)
    p_mid = r1.astype(BF16)
    p_lo = (r1 - p_mid.astype(F32)).astype(BF16)
    ov = overlap_ref[...]
    imp = _dot(ov, p_hi) + _dot(ov, p_mid) + _dot(ov, p_lo)
    cur = lax.shift_right_logical(t_lane[:, 0:tq], int(math.log2(SEL_BLOCK)))
    jb = lax.broadcasted_iota(jnp.int32, (n_sb, tq), 0)
    forced = (jb == 0) | (jb == cur) | (jb == cur - 1)
    score = jnp.where(jb > cur, FUTURE_SCORE, imp + jnp.where(forced, FORCE_BONUS, 0.0))
    rank = jnp.zeros((n_sb, tq), jnp.int32)
    for j2 in range(n_sb):
        row = score[j2:j2 + 1, :]
        rank += ((row > score) | ((row == score) & (j2 < jb))).astype(jnp.int32)
    sel_neg = jnp.where(rank < SEL_TOP, 0.0, NEG).astype(BF16)
    n_aug = ksel_ref.shape[1] - HEAD_DIM
    q_sel = jnp.concatenate([q_t, jnp.concatenate([sel_neg] * hpg, axis=1),
                             jnp.zeros((n_aug - n_sb, lanes), BF16)], axis=0)

    def sweep(k_ref, v_t_ref, q_rhs, lo, edge):
        def rows(kb):
            return pl.ds(pl.multiple_of(kb * ATT_BLK, ATT_BLK), ATT_BLK)

        def scores(kb, m):
            dblk = qi - kb
            tab = jnp.minimum(dblk, TAB_FAR)
            if edge is not None:
                tab = jnp.where(dblk == edge, TAB_EDGE, tab)
            s = _dot(k_ref[rows(kb), :], q_rhs) + tab_ref[tab]
            s_ref[rows(kb), :] = s
            return jnp.maximum(m, jnp.max(s, axis=0, keepdims=True))

        def over_chunks(body, init):
            n = qi + 1 - lo
            out = lax.fori_loop(0, n // 2, lambda i, c: body(lo + 2 * i + 1, body(lo + 2 * i, c)), init)
            return lax.cond(n % 2 == 1, lambda c: body(qi, c), lambda c: c, out)

        m = over_chunks(scores, jnp.full((1, lanes), NEG, F32))

        def accumulate(kb, carry):
            l, acc = carry
            pr = jnp.exp(s_ref[rows(kb), :] - m)
            return (l + jnp.sum(pr, axis=0, keepdims=True),
                    acc + _dot(v_t_ref[:, rows(kb)], pr.astype(BF16)))

        l, acc = over_chunks(accumulate, (jnp.zeros((1, lanes), F32), jnp.zeros((HEAD_DIM, lanes), F32)))
        return acc / l

    o_sel = sweep(ksel_ref, vsel_t_ref, q_sel, 0, None)
    n_back = WINDOW // ATT_BLK
    o_win = sweep(kwin_ref, vwin_t_ref, q_t, jnp.maximum(qi - n_back, 0), n_back)

    for c in range(0, tq, LANES):
        gates_t_ref[:, c:c + LANES] = jax.nn.sigmoid(gates_ref[c:c + LANES, :]).T
    for h in range(hpg):
        head = g * hpg + h
        sl = slice(h * tq, (h + 1) * tq)
        y = (gates_t_ref[pl.ds(head, 1), :] * o_cmp[:, sl]
             + gates_t_ref[pl.ds(N_ATT_HEADS + head, 1), :] * o_sel[:, sl]
             + gates_t_ref[pl.ds(2 * N_ATT_HEADS + head, 1), :] * o_win[:, sl])
        for c in range(0, tq, LANES):
            o_ref[c:c + LANES, h * HEAD_DIM:(h + 1) * HEAD_DIM] = y[:, c:c + LANES].T.astype(o_ref.dtype)

    for src, dst in zip(cast_in, cast_out):
        dst[...] = src[...].astype(BF16)


def _nsa(proj, gates, batch, seq, cmp_pos_k, w_ck1, w_ck2, cmp_pos_v, w_cv1, w_cv2, rel_table, to_cast=()):
    n_qt = seq // ATT_BLK
    hpg = HEADS_PER_GROUP
    n_cmp = seq // CMP_STRIDE
    n_sb = seq // SEL_BLOCK
    n_aug = LANES
    assert CMP_BLOCK == 2 * CMP_STRIDE and n_cmp % 8 == 0 and n_sb % 16 == 0 and n_sb <= n_aug
    assert (ATT_BLK // CMP_STRIDE) % 8 == 0 and n_qt * (ATT_BLK // CMP_STRIDE) <= n_cmp

    sweep_tab, cmp_tab = _bias_tables(rel_table, n_cmp)
    cs = np.arange(n_cmp)[None, :] * CMP_STRIDE
    ss = np.arange(n_sb)[:, None] * SEL_BLOCK
    overlap = (cs < ss + SEL_BLOCK) & (cs + CMP_BLOCK > ss) & (np.arange(n_cmp)[None, :] < n_cmp - 1)
    expand = np.arange(seq)[:, None] // SEL_BLOCK == np.arange(n_aug)[None, :]
    overlap = jnp.asarray(overlap, BF16)
    expand = jnp.asarray(expand, BF16)

    kv_col0 = COL_KV // HEAD_DIM

    def kv_spec(idx):
        return pl.BlockSpec((seq, HEAD_DIM), lambda b, g, qi, idx=idx: (b, kv_col0 + idx * N_KV_GROUPS + g))

    def whole(arr):
        return pl.BlockSpec(arr.shape, lambda b, g, qi, nd=arr.ndim: (0,) * nd)

    in_specs = [
        pl.BlockSpec((ATT_BLK, hpg * HEAD_DIM), lambda b, g, qi: (b * n_qt + qi, g)),
        kv_spec(0), kv_spec(1), kv_spec(2), kv_spec(3), kv_spec(4), kv_spec(5),
        pl.BlockSpec((ATT_BLK, LANES), lambda b, g, qi: (b * n_qt + qi, 0)),
        whole(cmp_pos_k), whole(w_ck1), whole(w_ck2), whole(cmp_pos_v), whole(w_cv1), whole(w_cv2),
        pl.BlockSpec((None,) + sweep_tab.shape[1:], lambda b, g, qi: (g, 0, 0, 0)),
        pl.BlockSpec((None,) + cmp_tab.shape[1:], lambda b, g, qi: (g, 0, 0)),
        whole(overlap), whole(expand),
    ]
    assert len(in_specs) == N_NSA_INPUTS
    n_steps = batch * N_KV_GROUPS * n_qt
    cast_specs = [_stripe_spec(a, n_steps, lambda b, g, qi: (b * N_KV_GROUPS + g) * n_qt + qi) for a in to_cast]
    outs = pl.pallas_call(
        functools.partial(_nsa_body, n_cast=len(to_cast)),
        grid=(batch, N_KV_GROUPS, n_qt),
        in_specs=in_specs + cast_specs,
        out_specs=[pl.BlockSpec((ATT_BLK, hpg * HEAD_DIM), lambda b, g, qi: (b * n_qt + qi, g))] + cast_specs,
        out_shape=[jax.ShapeDtypeStruct((batch * seq, D_ATT), BF16)]
        + [jax.ShapeDtypeStruct(a.shape, BF16) for a in to_cast],
        scratch_shapes=[pltpu.VMEM((n_cmp, HEAD_DIM), BF16), pltpu.VMEM((HEAD_DIM, n_cmp), BF16),
                        pltpu.VMEM((seq, HEAD_DIM + n_aug), BF16), pltpu.VMEM((HEAD_DIM, seq), BF16),
                        pltpu.VMEM((seq, HEAD_DIM), BF16), pltpu.VMEM((HEAD_DIM, seq), BF16),
                        pltpu.VMEM((LANES, ATT_BLK), F32), pltpu.VMEM((seq, hpg * ATT_BLK), F32)],
        compiler_params=pltpu.CompilerParams(
            dimension_semantics=(("arbitrary",) * 3 if to_cast else ("parallel", "parallel", "arbitrary")),
            vmem_limit_bytes=VMEM_LIMIT_BYTES),
        name="nsa",
    )(proj, proj, proj, proj, proj, proj, proj, gates,
      cmp_pos_k, w_ck1.astype(BF16), w_ck2.astype(BF16), cmp_pos_v, w_cv1.astype(BF16), w_cv2.astype(BF16),
      sweep_tab, cmp_tab, overlap, expand, *to_cast)
    return outs[0], outs[1:]


def _pool_body(u_ref, w_ref, scale_ref, o_ref):
    gi = pl.program_id(1)
    u = u_ref[...]
    seq = u.shape[0]
    t = lax.broadcasted_iota(jnp.int32, (seq, 1), 0)

    def shifted(x, k):
        return jnp.where(t >= k, pltpu.roll(x, k, 0), 0.0)

    sums = [u]
    for n in range(int(math.log2(POOL_WINDOWS[-1]))):
        sums.append(sums[-1] + shifted(sums[-1], 2 ** n))
    wsum = sums[int(math.log2(POOL_WINDOWS[0]))]
    for idx in range(1, len(POOL_WINDOWS)):
        wsum = jnp.where(gi >= idx, sums[int(math.log2(POOL_WINDOWS[idx]))], wsum)
    window = POOL_WINDOWS[0]
    for idx in range(1, len(POOL_WINDOWS)):
        window = jnp.where(gi >= idx, POOL_WINDOWS[idx], window)
    cnt = jnp.minimum(t + 1, window).astype(F32)
    d = (wsum / cnt - u).astype(BF16)
    o_ref[...] = (_dot(d, w_ref[0]) * scale_ref[...]).astype(o_ref.dtype)


def _pool(proj, batch, seq, w_pool, pool_scale):
    n_g, dg, _ = w_pool.shape
    assert all(w == 2 ** int(math.log2(w)) for w in POOL_WINDOWS) and list(POOL_WINDOWS) == sorted(POOL_WINDOWS)
    col0 = COL_U // dg
    return pl.pallas_call(
        _pool_body,
        grid=(batch, n_g),
        in_specs=[
            pl.BlockSpec((seq, dg), lambda b, gi: (b, col0 + gi)),
            pl.BlockSpec((1, dg, dg), lambda b, gi: (gi, 0, 0)),
            pl.BlockSpec((1, dg), lambda b, gi: (0, gi)),
        ],
        out_specs=pl.BlockSpec((seq, dg), lambda b, gi: (b, gi)),
        out_shape=jax.ShapeDtypeStruct((batch * seq, n_g * dg), BF16),
        compiler_params=pltpu.CompilerParams(
            dimension_semantics=("parallel", "parallel"), vmem_limit_bytes=VMEM_LIMIT_BYTES),
        name="pool",
    )(proj, w_pool.astype(BF16), pool_scale.reshape(1, -1))


def _outproj_body(x_ref, ya_ref, yp_ref, wa_ref, wp_ref, o_ref):
    o_ref[...] = x_ref[...] + _dot(ya_ref[...], wa_ref[...]) + _dot(yp_ref[...], wp_ref[...])


def _outproj(x, y_att, y_pool, w_out):
    n, d = x.shape
    da = y_att.shape[1]
    dp = y_pool.shape[1]
    w = w_out
    return pl.pallas_call(
        _outproj_body,
        grid=(d // OUT_TN, n // OUT_TM),
        in_specs=[
            pl.BlockSpec((OUT_TM, OUT_TN), lambda j, i: (i, j)),
            pl.BlockSpec((OUT_TM, da), lambda j, i: (i, 0)),
            pl.BlockSpec((OUT_TM, dp), lambda j, i: (i, 0)),
            pl.BlockSpec((da, OUT_TN), lambda j, i: (0, j)),
            pl.BlockSpec((dp, OUT_TN), lambda j, i: (da // dp, j)),
        ],
        out_specs=pl.BlockSpec((OUT_TM, OUT_TN), lambda j, i: (i, j)),
        out_shape=jax.ShapeDtypeStruct((n, d), F32),
        compiler_params=pltpu.CompilerParams(
            dimension_semantics=("parallel", "parallel"), vmem_limit_bytes=VMEM_LIMIT_BYTES),
        name="out_proj",
    )(x, y_att, y_pool, w, w)


def kernel(x, norm_ffn1, w_ffn1_gate, w_ffn1_up, w_ffn1_down, norm_mix, w_in, cmp_pos_k, w_cmp_k1, w_cmp_k2,
           cmp_pos_v, w_cmp_v1, w_cmp_v2, w_pool, pool_scale, w_out, rel_table, norm_ffn2, w_ffn2_gate,
           w_ffn2_up, w_ffn2_down, norm_final):
    batch, seq, d = x.shape
    depth = norm_ffn1.shape[0]
    xf = x.reshape(batch * seq, d)
    gf = norm_final.reshape(1, d)
    for l in range(depth):
        xf, (w_in_t,) = _ffn(
            xf, norm_ffn1[l].reshape(1, d), w_ffn1_gate[l].astype(BF16), w_ffn1_up[l].astype(BF16),
            w_ffn1_down[l].astype(BF16), gf, final_norm=False, to_cast=(w_in[l].T,))
        proj, gates = _inproj(xf, norm_mix[l].reshape(1, d), w_in_t)
        y_att, (w2_gate, w2_up, w2_down, w_out_b) = _nsa(
            proj, gates, batch, seq, cmp_pos_k[l], w_cmp_k1[l], w_cmp_k2[l], cmp_pos_v[l], w_cmp_v1[l], w_cmp_v2[l],
            rel_table, to_cast=(w_ffn2_gate[l], w_ffn2_up[l], w_ffn2_down[l], w_out[l]))
        y_pool = _pool(proj, batch, seq, w_pool[l], pool_scale[l])
        xf = _outproj(xf, y_att, y_pool, w_out_b)
        xf, _ = _ffn(xf, norm_ffn2[l].reshape(1, d), w2_gate, w2_up, w2_down, gf, final_norm=(l == depth - 1))
    if depth == 0:
        raise ValueError("depth must be positive")
    return xf.reshape(batch, seq, d)
```

```python
import functools
import math

import jax
import jax.numpy as jnp
import numpy as np
from jax import lax
from jax.experimental import pallas as pl
from jax.experimental.pallas import tpu as pltpu

HEAD_DIM = 128
N_ATT_HEADS = 16
N_KV_GROUPS = 4
HEADS_PER_GROUP = N_ATT_HEADS // N_KV_GROUPS
D_ATT = N_ATT_HEADS * HEAD_DIM
D_KV = N_KV_GROUPS * HEAD_DIM
N_POOL_GROUPS = 4
POOL_WINDOWS = (2, 4, 8, 16)
CMP_BLOCK = 32
CMP_STRIDE = 16
SEL_BLOCK = 64
SEL_TOP = 8
WINDOW = 512
REL_BUCKETS = 32
REL_MAX_DIST = 128
EPS = 1e-6
NEG = -1e30
FORCE_BONUS = 1e4
FUTURE_SCORE = -1e9

LANES = 128
MXU_DIM = 256
VMEM_LIMIT_BYTES = 60 * 2 ** 20

FFN_TM = 512
FFN_TF = 2 * MXU_DIM
FFN_TN = 1024
NORM_ROWS = 32
CAST_ROWS = 16
PROJ_TM = 1024
PROJ_ROWS = 512
PROJ_TN = 512
PROJ_TG = MXU_DIM
ATT_BLK = 256
OUT_TM = 1024
OUT_TN = 1024

BF16 = jnp.bfloat16
F32 = jnp.float32


def _dot(a, b):
    return jnp.dot(a, b, preferred_element_type=F32)


def _dot_nt(a, b):
    return lax.dot_general(a, b, (((1,), (1,)), ((), ())), preferred_element_type=F32)


def _rms(x, g):
    return x * lax.rsqrt(jnp.mean(x * x, axis=-1, keepdims=True) + EPS) * g


def _prefetch_row_tile(x_hbm, xbuf_ref, sem):
    i = pl.program_id(0)
    j = pl.program_id(1)
    tm = xbuf_ref.shape[0]

    def x_copy(tile):
        return pltpu.make_async_copy(x_hbm.at[pl.ds(pl.multiple_of(tile * tm, tm), tm), :], xbuf_ref, sem)

    @pl.when((i == 0) & (j == 0))
    def _():
        x_copy(0).start()

    @pl.when(j == 0)
    def _():
        x_copy(i).wait()

    @pl.when((j == 1) & (i + 1 < pl.num_programs(0)))
    def _():
        x_copy(i + 1).start()


def _ffn_body(*refs, final_norm, n_cast, tf, tail):
    x_hbm, g_ref, wg_ref, wu_ref, wd_ref, gf_ref = refs[:6]
    cast_in = refs[6:6 + n_cast]
    o_ref = refs[6 + n_cast]
    cast_out = refs[7 + n_cast:7 + 2 * n_cast]
    h_ref, x_tile, sem = refs[7 + 2 * n_cast:]
    j = pl.program_id(1)
    last = pl.num_programs(1) - 1
    tm = o_ref.shape[0]
    _prefetch_row_tile(x_hbm, x_tile, sem)

    @pl.when(j == 0)
    def _():
        for r in range(0, tm, NORM_ROWS):
            x = x_tile[r:r + NORM_ROWS, :]
            h_ref[r:r + NORM_ROWS, :] = _rms(x, g_ref[...]).astype(BF16)
            o_ref[r:r + NORM_ROWS, :] = x

    def accumulate(width):
        h = h_ref[...]
        gate = _dot(h, wg_ref[:, :width])
        up = _dot(h, wu_ref[:, :width])
        act = (0.5 * (gate * jax.nn.sigmoid(gate)) * up).astype(BF16)
        for c in range(0, o_ref.shape[1], FFN_TN):
            o_ref[:, c:c + FFN_TN] += _dot(act, wd_ref[:width, c:c + FFN_TN])

    if tail == tf:
        accumulate(tf)
    else:
        pl.when(j < last)(functools.partial(accumulate, tf))
        pl.when(j == last)(functools.partial(accumulate, tail))

    if final_norm:
        @pl.when(j == last)
        def _():
            for r in range(0, tm, NORM_ROWS):
                o_ref[r:r + NORM_ROWS, :] = _rms(o_ref[r:r + NORM_ROWS, :], gf_ref[...])

    for src, dst in zip(cast_in, cast_out):
        dst[...] = src[...].astype(BF16)


def _stripe_spec(a, n_steps, step_of):
    r, c = a.shape
    rows = CAST_ROWS
    while pl.cdiv(r, rows) > n_steps:
        rows += CAST_ROWS
    n_blocks = pl.cdiv(r, rows)
    return pl.BlockSpec((rows, c), lambda *idx: (jnp.minimum(step_of(*idx), n_blocks - 1), 0))


def _ffn(x, g, w_gate, w_up, w_down, gf, *, final_norm, to_cast=()):
    n, d = x.shape
    d_ff = w_gate.shape[1]
    tf = FFN_TF
    n_i, n_j = n // FFN_TM, pl.cdiv(d_ff, tf)
    tail = d_ff - (n_j - 1) * tf
    assert tail % MXU_DIM == 0 and n_j >= 2
    cast_specs = [_stripe_spec(a, n_i * n_j, lambda i, j: i * n_j + j) for a in to_cast]
    outs = pl.pallas_call(
        functools.partial(_ffn_body, final_norm=final_norm, n_cast=len(to_cast), tf=tf, tail=tail),
        grid=(n_i, n_j),
        in_specs=[
            pl.BlockSpec(memory_space=pl.ANY),
            pl.BlockSpec((1, d), lambda i, j: (0, 0)),
            pl.BlockSpec((d, tf), lambda i, j: (0, j)),
            pl.BlockSpec((d, tf), lambda i, j: (0, j)),
            pl.BlockSpec((tf, d), lambda i, j: (j, 0)),
            pl.BlockSpec((1, d), lambda i, j: (0, 0)),
        ] + cast_specs,
        out_specs=[pl.BlockSpec((FFN_TM, d), lambda i, j: (i, 0))] + cast_specs,
        out_shape=[jax.ShapeDtypeStruct((n, d), F32)] + [jax.ShapeDtypeStruct(a.shape, BF16) for a in to_cast],
        scratch_shapes=[pltpu.VMEM((FFN_TM, d), BF16), pltpu.VMEM((FFN_TM, d), F32), pltpu.SemaphoreType.DMA(())],
        compiler_params=pltpu.CompilerParams(
            dimension_semantics=("arbitrary", "arbitrary"), vmem_limit_bytes=VMEM_LIMIT_BYTES),
        name="ffn_final" if final_norm else "ffn",
    )(x, g, w_gate, w_up, w_down, gf, *to_cast)
    return outs[0], outs[1:]


COL_Q = 0
COL_KV = D_ATT
COL_U = D_ATT + 6 * D_KV
D_POOL_COLS = 2048
COL_GATES = COL_U + D_POOL_COLS
N_GATES = 3 * N_ATT_HEADS


def _inproj_body(x_hbm, g_ref, wa_ref, wu_ref, wg_ref, o_ref, og_ref, h_ref, xbuf_ref, sem, *, n_a):
    j = pl.program_id(1)
    _prefetch_row_tile(x_hbm, xbuf_ref, sem)

    @pl.when(j == 0)
    def _():
        for r in range(0, xbuf_ref.shape[0], NORM_ROWS):
            h_ref[r:r + NORM_ROWS, :] = _rms(xbuf_ref[r:r + NORM_ROWS, :], g_ref[...]).astype(BF16)

    def project(w_ref, dst_ref):
        for r in range(0, dst_ref.shape[0], PROJ_ROWS):
            dst_ref[r:r + PROJ_ROWS, :] = _dot_nt(h_ref[r:r + PROJ_ROWS, :], w_ref[...])

    pl.when(j < n_a)(functools.partial(project, wa_ref, o_ref))
    pl.when(j >= n_a)(functools.partial(project, wu_ref, o_ref))
    pl.when(j == pl.num_programs(1) - 1)(functools.partial(project, wg_ref, og_ref))


def _inproj(x, g, w_in_t):
    n, d = x.shape
    assert w_in_t.shape[0] == COL_GATES + N_GATES and N_GATES <= PROJ_TG
    assert COL_U % PROJ_TN == 0 and D_POOL_COLS % PROJ_TN == 0
    wu = w_in_t[COL_U + N_GATES:]
    wg = jnp.pad(w_in_t[COL_U:COL_U + N_GATES], ((0, PROJ_TG - N_GATES), (0, 0)))
    n_a = COL_U // PROJ_TN
    n_u = D_POOL_COLS // PROJ_TN
    return pl.pallas_call(
        functools.partial(_inproj_body, n_a=n_a),
        grid=(n // PROJ_TM, n_a + n_u),
        in_specs=[
            pl.BlockSpec(memory_space=pl.ANY),
            pl.BlockSpec((1, d), lambda i, j: (0, 0)),
            pl.BlockSpec((PROJ_TN, d), lambda i, j: (jnp.minimum(j, n_a - 1), 0)),
            pl.BlockSpec((PROJ_TN, d), lambda i, j: (jnp.maximum(j - n_a, 0), 0)),
            pl.BlockSpec((PROJ_TG, d), lambda i, j: (0, 0), pipeline_mode=pl.Buffered(1)),
        ],
        out_specs=[pl.BlockSpec((PROJ_TM, PROJ_TN), lambda i, j: (i, j)),
                   pl.BlockSpec((PROJ_TM, PROJ_TG), lambda i, j: (i, 0))],
        out_shape=[jax.ShapeDtypeStruct((n, COL_GATES), F32), jax.ShapeDtypeStruct((n, PROJ_TG), F32)],
        scratch_shapes=[pltpu.VMEM((PROJ_TM, d), BF16), pltpu.VMEM((PROJ_TM, d), F32), pltpu.SemaphoreType.DMA(())],
        compiler_params=pltpu.CompilerParams(
            dimension_semantics=("arbitrary", "arbitrary"), vmem_limit_bytes=VMEM_LIMIT_BYTES),
        name="in_proj",
    )(x, g, w_in_t, wu, wg)


def _rel_bucket_np(n):
    max_exact = REL_BUCKETS // 2
    n = np.maximum(n, 0)
    nf = np.maximum(n, 1).astype(np.float32)
    large = max_exact + (np.log(nf / max_exact) / math.log(REL_MAX_DIST / max_exact)
                         * (REL_BUCKETS - max_exact)).astype(np.int32)
    large = np.minimum(large, REL_BUCKETS - 1)
    return np.where(n < max_exact, n, large)


FAR_DIST = int(np.max(np.nonzero(_rel_bucket_np(np.arange(4 * REL_MAX_DIST)) < REL_BUCKETS - 1)[0])) + 1
CMP_ROW_OFF = -(-(FAR_DIST + CMP_BLOCK - 1) // CMP_STRIDE) - 1
TAB_DIAG, TAB_PREV, TAB_FAR, TAB_EDGE = 0, 1, 2, 3


def _pick(rel_table, idx):
    onehot = jnp.asarray(np.arange(REL_BUCKETS)[:, None] == idx[None, :], F32)
    return jnp.einsum("kh,kn->hn", rel_table.astype(F32), onehot, precision=lax.Precision.HIGHEST)


def _group_lanes(t):
    t = t.reshape((N_KV_GROUPS, HEADS_PER_GROUP) + t.shape[1:])
    t = jnp.moveaxis(t, 1, -2)
    return t.reshape(t.shape[:-2] + (HEADS_PER_GROUP * t.shape[-1],))


def _toeplitz_body(p_ref, o_ref):
    blk = o_ref.shape[-1]
    for t in range(o_ref.shape[0]):
        y = pltpu.roll(jnp.broadcast_to(p_ref[t:t + 1, :], (blk, 2 * blk)), 0, 1, stride=1, stride_axis=0)
        o_ref[t] = y[:, :blk]


def _bias_tables(rel_table, n_cmp):
    blk = ATT_BLK
    n_heads = rel_table.shape[1]
    far = REL_BUCKETS - 1
    assert blk >= FAR_DIST and WINDOW % blk == 0
    m = np.arange(2 * blk)
    d = np.where(m < blk, m, m - 2 * blk)
    idx = np.stack([_rel_bucket_np(d), _rel_bucket_np(blk + d), np.full_like(d, far), np.full_like(d, far)])
    mask = np.stack([np.where(d < 0, NEG, 0.0), np.zeros_like(d, np.float64), np.zeros_like(d, np.float64),
                     np.where(d >= 0, NEG, 0.0)]).astype(np.float32)
    profiles = _pick(rel_table, idx.reshape(-1)).reshape(n_heads, 4, 2 * blk) + mask[None]
    sweep = pl.pallas_call(
        _toeplitz_body,
        grid=(N_KV_GROUPS, HEADS_PER_GROUP),
        in_specs=[pl.BlockSpec((None, 4, 2 * blk), lambda g, h: (g * HEADS_PER_GROUP + h, 0, 0))],
        out_specs=pl.BlockSpec((None, 4, blk, blk), lambda g, h: (g, 0, 0, h)),
        out_shape=jax.ShapeDtypeStruct((N_KV_GROUPS, 4, blk, HEADS_PER_GROUP * blk), F32),
        name="bias_tables",
    )(profiles)

    rows = np.arange(n_cmp - CMP_ROW_OFF, n_cmp + (blk - CMP_BLOCK) // CMP_STRIDE + 1)
    dist = np.arange(blk)[None, :] - ((rows[:, None] - n_cmp) * CMP_STRIDE + CMP_BLOCK - 1)
    band = _pick(rel_table, _rel_bucket_np(dist).reshape(-1)).reshape(n_heads, len(rows), blk)
    band = band + np.where(dist < 0, NEG, 0.0).astype(np.float32)[None]
    before = jnp.broadcast_to(rel_table[far].astype(F32)[:, None, None], (n_heads, int(rows[0]), blk))
    after = jnp.full((n_heads, 2 * n_cmp - int(rows[-1]) - 1, blk), NEG, F32)
    cmp_tab = _group_lanes(jnp.concatenate([before, band, after], axis=1))
    return sweep, cmp_tab


def _compress(kv_ref, pos_ref, w1_ref, w2_ref):
    seq, dk = kv_ref.shape
    n_half = seq // CMP_STRIDE
    first = jnp.zeros((n_half, w1_ref.shape[1]), F32)
    second = jnp.zeros((n_half, w1_ref.shape[1]), F32)
    for l in range(CMP_STRIDE):
        rows = kv_ref[pl.ds(l, n_half, stride=CMP_STRIDE), :]
        a = (rows + pos_ref[l:l + 1, :]).astype(BF16)
        b = (rows + pos_ref[CMP_STRIDE + l:CMP_STRIDE + l + 1, :]).astype(BF16)
        first += _dot(a, w1_ref[l * dk:(l + 1) * dk, :])
        second += _dot(b, w1_ref[(CMP_STRIDE + l) * dk:(CMP_STRIDE + l + 1) * dk, :])
    pre = first + pltpu.roll(second, n_half - 1, 0)
    hid = (pre * jax.nn.sigmoid(pre)).astype(BF16)
    return _dot(hid, w2_ref[...])


N_NSA_INPUTS = 18


def _nsa_body(*refs, n_cast):
    (q_ref, kc_ref, vc_ref, ks_ref, vs_ref, kw_ref, vw_ref, gates_ref,
     posk_ref, wk1_ref, wk2_ref, posv_ref, wv1_ref, wv2_ref,
     tab_ref, tabc_ref, overlap_ref, expand_ref) = refs[:N_NSA_INPUTS]
    cast_in = refs[N_NSA_INPUTS:N_NSA_INPUTS + n_cast]
    o_ref = refs[N_NSA_INPUTS + n_cast]
    cast_out = refs[N_NSA_INPUTS + n_cast + 1:N_NSA_INPUTS + 2 * n_cast + 1]
    (kcmp_ref, vcmp_t_ref, ksel_ref, vsel_t_ref, kwin_ref, vwin_t_ref, gates_t_ref,
     s_ref) = refs[N_NSA_INPUTS + 2 * n_cast + 1:]
    g = pl.program_id(1)
    qi = pl.program_id(2)
    tq = ATT_BLK
    hpg = HEADS_PER_GROUP
    lanes = hpg * tq
    n_cmp = kcmp_ref.shape[0]
    n_sb = overlap_ref.shape[0]
    seq = ks_ref.shape[0]
    assert tq & (tq - 1) == 0 and SEL_BLOCK & (SEL_BLOCK - 1) == 0

    @pl.when(qi == 0)
    def _():
        kcmp_ref[...] = _compress(kc_ref, posk_ref, wk1_ref, wk2_ref).astype(BF16)
        vcmp_t_ref[...] = _compress(vc_ref, posv_ref, wv1_ref, wv2_ref).astype(BF16).T
        for c in range(0, seq, LANES):
            ksel_ref[c:c + LANES, 0:HEAD_DIM] = ks_ref[c:c + LANES, :].astype(BF16)
            ksel_ref[c:c + LANES, HEAD_DIM:] = expand_ref[c:c + LANES, :]
            kwin_ref[c:c + LANES, :] = kw_ref[c:c + LANES, :].astype(BF16)
            vsel_t_ref[:, c:c + LANES] = vs_ref[c:c + LANES, :].T.astype(BF16)
            vwin_t_ref[:, c:c + LANES] = vw_ref[c:c + LANES, :].T.astype(BF16)

    q = q_ref[...] * (HEAD_DIM ** -0.5)
    q_t = jnp.concatenate([q[:, h * HEAD_DIM:(h + 1) * HEAD_DIM].T for h in range(hpg)],
                          axis=1).astype(BF16)
    t_lane = qi * tq + (lax.broadcasted_iota(jnp.int32, (1, lanes), 1) & (tq - 1))

    first_row = pl.multiple_of(n_cmp - qi * (tq // CMP_STRIDE), 8)
    logit = _dot(kcmp_ref[...], q_t) + tabc_ref[pl.ds(first_row, n_cmp), :]
    e = jnp.exp(logit - jnp.max(logit, axis=0, keepdims=True))
    any_valid = (t_lane >= CMP_BLOCK - 1).astype(F32)
    p = e / jnp.sum(e, axis=0, keepdims=True) * any_valid
    o_cmp = _dot(vcmp_t_ref[...], p.astype(BF16))

    psum = p[:, 0:tq]
    for h in range(1, hpg):
        psum = psum + p[:, h * tq:(h + 1) * tq]
    p_hi = psum.astype(BF16)
    r1 = psum - p_hi.astype(F32)
    p_mid = r1.astype(BF16)
    p_lo = (r1 - p_mid.astype(F32)).astype(BF16)
    ov = overlap_ref[...]
    imp = _dot(ov, p_hi) + _dot(ov, p_mid) + _dot(ov, p_lo)
    cur = lax.shift_right_logical(t_lane[:, 0:tq], int(math.log2(SEL_BLOCK)))
    jb = lax.broadcasted_iota(jnp.int32, (n_sb, tq), 0)
    forced = (jb == 0) | (jb == cur) | (jb == cur - 1)
    score = jnp.where(jb > cur, FUTURE_SCORE, imp + jnp.where(forced, FORCE_BONUS, 0.0))
    rank = jnp.zeros((n_sb, tq), jnp.int32)
    for j2 in range(n_sb):
        row = score[j2:j2 + 1, :]
        rank += ((row > score) | ((row == score) & (j2 < jb))).astype(jnp.int32)
    sel_neg = jnp.where(rank < SEL_TOP, 0.0, NEG).astype(BF16)
    n_aug = ksel_ref.shape[1] - HEAD_DIM
    q_sel = jnp.concatenate([q_t, jnp.concatenate([sel_neg] * hpg, axis=1),
                             jnp.zeros((n_aug - n_sb, lanes), BF16)], axis=0)

    def sweep(k_ref, v_t_ref, q_rhs, lo, edge):
        def rows(kb):
            return pl.ds(pl.multiple_of(kb * ATT_BLK, ATT_BLK), ATT_BLK)

        def scores(kb, m):
            dblk = qi - kb
            tab = jnp.minimum(dblk, TAB_FAR)
            if edge is not None:
                tab = jnp.where(dblk == edge, TAB_EDGE, tab)
            s = _dot(k_ref[rows(kb), :], q_rhs) + tab_ref[tab]
            s_ref[rows(kb), :] = s
            return jnp.maximum(m, jnp.max(s, axis=0, keepdims=True))

        def over_chunks(body, init):
            n = qi + 1 - lo

            def looped(c):
                out = lax.fori_loop(0, n // 2, lambda i, c: body(lo + 2 * i + 1, body(lo + 2 * i, c)), c)
                return lax.cond(n % 2 == 1, lambda c: body(qi, c), lambda c: c, out)

            if edge is None:
                return looped(init)

            def straight(c):
                for back in range(edge, -1, -1):
                    c = body(qi - back, c)
                return c

            return lax.cond(qi >= edge, straight, looped, init)

        m = over_chunks(scores, jnp.full((1, lanes), NEG, F32))

        def accumulate(kb, carry):
            l, acc = carry
            pr = jnp.exp(s_ref[rows(kb), :] - m)
            return (l + jnp.sum(pr, axis=0, keepdims=True),
                    acc + _dot(v_t_ref[:, rows(kb)], pr.astype(BF16)))

        l, acc = over_chunks(accumulate, (jnp.zeros((1, lanes), F32), jnp.zeros((HEAD_DIM, lanes), F32)))
        return acc / l

    o_sel = sweep(ksel_ref, vsel_t_ref, q_sel, 0, None)
    n_back = WINDOW // ATT_BLK
    o_win = sweep(kwin_ref, vwin_t_ref, q_t, jnp.maximum(qi - n_back, 0), n_back)

    for c in range(0, tq, LANES):
        gates_t_ref[:, c:c + LANES] = jax.nn.sigmoid(gates_ref[c:c + LANES, :]).T
    for h in range(hpg):
        head = g * hpg + h
        sl = slice(h * tq, (h + 1) * tq)
        y = (gates_t_ref[pl.ds(head, 1), :] * o_cmp[:, sl]
             + gates_t_ref[pl.ds(N_ATT_HEADS + head, 1), :] * o_sel[:, sl]
             + gates_t_ref[pl.ds(2 * N_ATT_HEADS + head, 1), :] * o_win[:, sl])
        for c in range(0, tq, LANES):
            o_ref[c:c + LANES, h * HEAD_DIM:(h + 1) * HEAD_DIM] = y[:, c:c + LANES].T.astype(o_ref.dtype)

    for src, dst in zip(cast_in, cast_out):
        dst[...] = src[...].astype(BF16)


def _nsa(proj, gates, batch, seq, cmp_pos_k, w_ck1, w_ck2, cmp_pos_v, w_cv1, w_cv2, rel_table, to_cast=()):
    n_qt = seq // ATT_BLK
    hpg = HEADS_PER_GROUP
    n_cmp = seq // CMP_STRIDE
    n_sb = seq // SEL_BLOCK
    n_aug = LANES
    assert CMP_BLOCK == 2 * CMP_STRIDE and n_cmp % 8 == 0 and n_sb % 16 == 0 and n_sb <= n_aug
    assert (ATT_BLK // CMP_STRIDE) % 8 == 0 and n_qt * (ATT_BLK // CMP_STRIDE) <= n_cmp

    sweep_tab, cmp_tab = _bias_tables(rel_table, n_cmp)
    cs = np.arange(n_cmp)[None, :] * CMP_STRIDE
    ss = np.arange(n_sb)[:, None] * SEL_BLOCK
    overlap = (cs < ss + SEL_BLOCK) & (cs + CMP_BLOCK > ss) & (np.arange(n_cmp)[None, :] < n_cmp - 1)
    expand = np.arange(seq)[:, None] // SEL_BLOCK == np.arange(n_aug)[None, :]
    overlap = jnp.asarray(overlap, BF16)
    expand = jnp.asarray(expand, BF16)

    kv_col0 = COL_KV // HEAD_DIM

    def kv_spec(idx):
        return pl.BlockSpec((seq, HEAD_DIM), lambda b, g, qi, idx=idx: (b, kv_col0 + idx * N_KV_GROUPS + g))

    def whole(arr):
        return pl.BlockSpec(arr.shape, lambda b, g, qi, nd=arr.ndim: (0,) * nd)

    in_specs = [
        pl.BlockSpec((ATT_BLK, hpg * HEAD_DIM), lambda b, g, qi: (b * n_qt + qi, g)),
        kv_spec(0), kv_spec(1), kv_spec(2), kv_spec(3), kv_spec(4), kv_spec(5),
        pl.BlockSpec((ATT_BLK, LANES), lambda b, g, qi: (b * n_qt + qi, 0)),
        whole(cmp_pos_k), whole(w_ck1), whole(w_ck2), whole(cmp_pos_v), whole(w_cv1), whole(w_cv2),
        pl.BlockSpec((None,) + sweep_tab.shape[1:], lambda b, g, qi: (g, 0, 0, 0)),
        pl.BlockSpec((None,) + cmp_tab.shape[1:], lambda b, g, qi: (g, 0, 0)),
        whole(overlap), whole(expand),
    ]
    assert len(in_specs) == N_NSA_INPUTS
    n_steps = batch * N_KV_GROUPS * n_qt
    cast_specs = [_stripe_spec(a, n_steps, lambda b, g, qi: (b * N_KV_GROUPS + g) * n_qt + qi) for a in to_cast]
    outs = pl.pallas_call(
        functools.partial(_nsa_body, n_cast=len(to_cast)),
        grid=(batch, N_KV_GROUPS, n_qt),
        in_specs=in_specs + cast_specs,
        out_specs=[pl.BlockSpec((ATT_BLK, hpg * HEAD_DIM), lambda b, g, qi: (b * n_qt + qi, g))] + cast_specs,
        out_shape=[jax.ShapeDtypeStruct((batch * seq, D_ATT), BF16)]
        + [jax.ShapeDtypeStruct(a.shape, BF16) for a in to_cast],
        scratch_shapes=[pltpu.VMEM((n_cmp, HEAD_DIM), BF16), pltpu.VMEM((HEAD_DIM, n_cmp), BF16),
                        pltpu.VMEM((seq, HEAD_DIM + n_aug), BF16), pltpu.VMEM((HEAD_DIM, seq), BF16),
                        pltpu.VMEM((seq, HEAD_DIM), BF16), pltpu.VMEM((HEAD_DIM, seq), BF16),
                        pltpu.VMEM((LANES, ATT_BLK), F32), pltpu.VMEM((seq, hpg * ATT_BLK), F32)],
        compiler_params=pltpu.CompilerParams(
            dimension_semantics=(("arbitrary",) * 3 if to_cast else ("parallel", "parallel", "arbitrary")),
            vmem_limit_bytes=VMEM_LIMIT_BYTES),
        name="nsa",
    )(proj, proj, proj, proj, proj, proj, proj, gates,
      cmp_pos_k, w_ck1.astype(BF16), w_ck2.astype(BF16), cmp_pos_v, w_cv1.astype(BF16), w_cv2.astype(BF16),
      sweep_tab, cmp_tab, overlap, expand, *to_cast)
    return outs[0], outs[1:]


def _pool_body(u_ref, w_ref, scale_ref, o_ref):
    gi = pl.program_id(1)
    u = u_ref[...]
    seq = u.shape[0]
    t = lax.broadcasted_iota(jnp.int32, (seq, 1), 0)

    def shifted(x, k):
        return jnp.where(t >= k, pltpu.roll(x, k, 0), 0.0)

    sums = [u]
    for n in range(int(math.log2(POOL_WINDOWS[-1]))):
        sums.append(sums[-1] + shifted(sums[-1], 2 ** n))
    wsum = sums[int(math.log2(POOL_WINDOWS[0]))]
    for idx in range(1, len(POOL_WINDOWS)):
        wsum = jnp.where(gi >= idx, sums[int(math.log2(POOL_WINDOWS[idx]))], wsum)
    window = POOL_WINDOWS[0]
    for idx in range(1, len(POOL_WINDOWS)):
        window = jnp.where(gi >= idx, POOL_WINDOWS[idx], window)
    cnt = jnp.minimum(t + 1, window).astype(F32)
    d = (wsum / cnt - u).astype(BF16)
    o_ref[...] = (_dot(d, w_ref[0]) * scale_ref[...]).astype(o_ref.dtype)


def _pool(proj, batch, seq, w_pool, pool_scale):
    n_g, dg, _ = w_pool.shape
    assert all(w == 2 ** int(math.log2(w)) for w in POOL_WINDOWS) and list(POOL_WINDOWS) == sorted(POOL_WINDOWS)
    col0 = COL_U // dg
    return pl.pallas_call(
        _pool_body,
        grid=(batch, n_g),
        in_specs=[
            pl.BlockSpec((seq, dg), lambda b, gi: (b, col0 + gi)),
            pl.BlockSpec((1, dg, dg), lambda b, gi: (gi, 0, 0)),
            pl.BlockSpec((1, dg), lambda b, gi: (0, gi)),
        ],
        out_specs=pl.BlockSpec((seq, dg), lambda b, gi: (b, gi)),
        out_shape=jax.ShapeDtypeStruct((batch * seq, n_g * dg), BF16),
        compiler_params=pltpu.CompilerParams(
            dimension_semantics=("parallel", "parallel"), vmem_limit_bytes=VMEM_LIMIT_BYTES),
        name="pool",
    )(proj, w_pool.astype(BF16), pool_scale.reshape(1, -1))


def _outproj_body(x_ref, ya_ref, yp_ref, wa_ref, wp_ref, o_ref):
    o_ref[...] = x_ref[...] + _dot(ya_ref[...], wa_ref[...]) + _dot(yp_ref[...], wp_ref[...])


def _outproj(x, y_att, y_pool, w_out):
    n, d = x.shape
    da = y_att.shape[1]
    dp = y_pool.shape[1]
    w = w_out
    return pl.pallas_call(
        _outproj_body,
        grid=(d // OUT_TN, n // OUT_TM),
        in_specs=[
            pl.BlockSpec((OUT_TM, OUT_TN), lambda j, i: (i, j)),
            pl.BlockSpec((OUT_TM, da), lambda j, i: (i, 0)),
            pl.BlockSpec((OUT_TM, dp), lambda j, i: (i, 0)),
            pl.BlockSpec((da, OUT_TN), lambda j, i: (0, j)),
            pl.BlockSpec((dp, OUT_TN), lambda j, i: (da // dp, j)),
        ],
        out_specs=pl.BlockSpec((OUT_TM, OUT_TN), lambda j, i: (i, j)),
        out_shape=jax.ShapeDtypeStruct((n, d), F32),
        compiler_params=pltpu.CompilerParams(
            dimension_semantics=("parallel", "parallel"), vmem_limit_bytes=VMEM_LIMIT_BYTES),
        name="out_proj",
    )(x, y_att, y_pool, w, w)


def kernel(x, norm_ffn1, w_ffn1_gate, w_ffn1_up, w_ffn1_down, norm_mix, w_in, cmp_pos_k, w_cmp_k1, w_cmp_k2,
           cmp_pos_v, w_cmp_v1, w_cmp_v2, w_pool, pool_scale, w_out, rel_table, norm_ffn2, w_ffn2_gate,
           w_ffn2_up, w_ffn2_down, norm_final):
    batch, seq, d = x.shape
    depth = norm_ffn1.shape[0]
    xf = x.reshape(batch * seq, d)
    gf = norm_final.reshape(1, d)
    for l in range(depth):
        xf, (w_in_t,) = _ffn(
            xf, norm_ffn1[l].reshape(1, d), w_ffn1_gate[l].astype(BF16), w_ffn1_up[l].astype(BF16),
            w_ffn1_down[l].astype(BF16), gf, final_norm=False, to_cast=(w_in[l].T,))
        proj, gates = _inproj(xf, norm_mix[l].reshape(1, d), w_in_t)
        y_att, (w2_gate, w2_up, w2_down, w_out_b) = _nsa(
            proj, gates, batch, seq, cmp_pos_k[l], w_cmp_k1[l], w_cmp_k2[l], cmp_pos_v[l], w_cmp_v1[l], w_cmp_v2[l],
            rel_table, to_cast=(w_ffn2_gate[l], w_ffn2_up[l], w_ffn2_down[l], w_out[l]))
        y_pool = _pool(proj, batch, seq, w_pool[l], pool_scale[l])
        xf = _outproj(xf, y_att, y_pool, w_out_b)
        xf, _ = _ffn(xf, norm_ffn2[l].reshape(1, d), w2_gate, w2_up, w2_down, gf, final_norm=(l == depth - 1))
    if depth == 0:
        raise ValueError("depth must be positive")
    return xf.reshape(batch, seq, d)
```

```python
import functools
import math

import jax
import jax.numpy as jnp
import numpy as np
from jax import lax
from jax.experimental import pallas as pl
from jax.experimental.pallas import tpu as pltpu

HEAD_DIM = 128
N_ATT_HEADS = 16
N_KV_GROUPS = 4
HEADS_PER_GROUP = N_ATT_HEADS // N_KV_GROUPS
D_ATT = N_ATT_HEADS * HEAD_DIM
D_KV = N_KV_GROUPS * HEAD_DIM
N_POOL_GROUPS = 4
POOL_WINDOWS = (2, 4, 8, 16)
CMP_BLOCK = 32
CMP_STRIDE = 16
SEL_BLOCK = 64
SEL_TOP = 8
WINDOW = 512
REL_BUCKETS = 32
REL_MAX_DIST = 128
EPS = 1e-6
NEG = -1e30
FORCE_BONUS = 1e4
FUTURE_SCORE = -1e9

LANES = 128
MXU_DIM = 256
VMEM_LIMIT_BYTES = 60 * 2 ** 20

FFN_TM = 512
FFN_TF = 2 * MXU_DIM
FFN_TN = 1024
NORM_ROWS = 32
CAST_ROWS = 16
PROJ_TM = 1024
PROJ_ROWS = 512
PROJ_TN = 512
PROJ_TG = MXU_DIM
ATT_BLK = 256
OUT_TM = 1024
OUT_TN = 1024

BF16 = jnp.bfloat16
F32 = jnp.float32


def _dot(a, b):
    return jnp.dot(a, b, preferred_element_type=F32)


def _dot_nt(a, b):
    return lax.dot_general(a, b, (((1,), (1,)), ((), ())), preferred_element_type=F32)


def _rms(x, g):
    return x * lax.rsqrt(jnp.mean(x * x, axis=-1, keepdims=True) + EPS) * g


def _prefetch_row_tile(x_hbm, xbuf_ref, sem):
    i = pl.program_id(0)
    j = pl.program_id(1)
    tm = xbuf_ref.shape[0]

    def x_copy(tile):
        return pltpu.make_async_copy(x_hbm.at[pl.ds(pl.multiple_of(tile * tm, tm), tm), :], xbuf_ref, sem)

    @pl.when((i == 0) & (j == 0))
    def _():
        x_copy(0).start()

    @pl.when(j == 0)
    def _():
        x_copy(i).wait()

    @pl.when((j == 1) & (i + 1 < pl.num_programs(0)))
    def _():
        x_copy(i + 1).start()


def _ffn_body(*refs, final_norm, n_cast, tf, tail):
    x_hbm, g_ref, wg_ref, wu_ref, wd_ref, gf_ref = refs[:6]
    cast_in = refs[6:6 + n_cast]
    o_ref = refs[6 + n_cast]
    cast_out = refs[7 + n_cast:7 + 2 * n_cast]
    h_ref, x_tile, sem = refs[7 + 2 * n_cast:]
    j = pl.program_id(1)
    last = pl.num_programs(1) - 1
    tm = o_ref.shape[0]
    _prefetch_row_tile(x_hbm, x_tile, sem)

    @pl.when(j == 0)
    def _():
        for r in range(0, tm, NORM_ROWS):
            x = x_tile[r:r + NORM_ROWS, :]
            h_ref[r:r + NORM_ROWS, :] = _rms(x, g_ref[...]).astype(BF16)
            o_ref[r:r + NORM_ROWS, :] = x

    def accumulate(width):
        h = h_ref[...]
        gate = _dot(h, wg_ref[:, :width])
        up = _dot(h, wu_ref[:, :width])
        act = (0.5 * (gate * jax.nn.sigmoid(gate)) * up).astype(BF16)
        for c in range(0, o_ref.shape[1], FFN_TN):
            o_ref[:, c:c + FFN_TN] += _dot(act, wd_ref[:width, c:c + FFN_TN])

    if tail == tf:
        accumulate(tf)
    else:
        pl.when(j < last)(functools.partial(accumulate, tf))
        pl.when(j == last)(functools.partial(accumulate, tail))

    if final_norm:
        @pl.when(j == last)
        def _():
            for r in range(0, tm, NORM_ROWS):
                o_ref[r:r + NORM_ROWS, :] = _rms(o_ref[r:r + NORM_ROWS, :], gf_ref[...])

    for src, dst in zip(cast_in, cast_out):
        dst[...] = src[...].astype(BF16)


def _stripe_spec(a, n_steps, step_of):
    r, c = a.shape
    rows = CAST_ROWS
    while pl.cdiv(r, rows) > n_steps:
        rows += CAST_ROWS
    n_blocks = pl.cdiv(r, rows)
    return pl.BlockSpec((rows, c), lambda *idx: (jnp.minimum(step_of(*idx), n_blocks - 1), 0))


def _ffn(x, g, w_gate, w_up, w_down, gf, *, final_norm, to_cast=()):
    n, d = x.shape
    d_ff = w_gate.shape[1]
    tf = FFN_TF
    n_i, n_j = n // FFN_TM, pl.cdiv(d_ff, tf)
    tail = d_ff - (n_j - 1) * tf
    assert tail % MXU_DIM == 0 and n_j >= 2
    cast_specs = [_stripe_spec(a, n_i * n_j, lambda i, j: i * n_j + j) for a in to_cast]
    outs = pl.pallas_call(
        functools.partial(_ffn_body, final_norm=final_norm, n_cast=len(to_cast), tf=tf, tail=tail),
        grid=(n_i, n_j),
        in_specs=[
            pl.BlockSpec(memory_space=pl.ANY),
            pl.BlockSpec((1, d), lambda i, j: (0, 0)),
            pl.BlockSpec((d, tf), lambda i, j: (0, j)),
            pl.BlockSpec((d, tf), lambda i, j: (0, j)),
            pl.BlockSpec((tf, d), lambda i, j: (j, 0)),
            pl.BlockSpec((1, d), lambda i, j: (0, 0)),
        ] + cast_specs,
        out_specs=[pl.BlockSpec((FFN_TM, d), lambda i, j: (i, 0))] + cast_specs,
        out_shape=[jax.ShapeDtypeStruct((n, d), F32)] + [jax.ShapeDtypeStruct(a.shape, BF16) for a in to_cast],
        scratch_shapes=[pltpu.VMEM((FFN_TM, d), BF16), pltpu.VMEM((FFN_TM, d), F32), pltpu.SemaphoreType.DMA(())],
        compiler_params=pltpu.CompilerParams(
            dimension_semantics=("arbitrary", "arbitrary"), vmem_limit_bytes=VMEM_LIMIT_BYTES),
        name="ffn_final" if final_norm else "ffn",
    )(x, g, w_gate, w_up, w_down, gf, *to_cast)
    return outs[0], outs[1:]


COL_Q = 0
COL_KV = D_ATT
COL_U = D_ATT + 6 * D_KV
D_POOL_COLS = 2048
COL_GATES = COL_U + D_POOL_COLS
N_GATES = 3 * N_ATT_HEADS


def _inproj_body(x_hbm, g_ref, wa_ref, wu_ref, wg_ref, o_ref, og_ref, h_ref, xbuf_ref, sem, *, n_a):
    j = pl.program_id(1)
    _prefetch_row_tile(x_hbm, xbuf_ref, sem)

    @pl.when(j == 0)
    def _():
        for r in range(0, xbuf_ref.shape[0], NORM_ROWS):
            h_ref[r:r + NORM_ROWS, :] = _rms(xbuf_ref[r:r + NORM_ROWS, :], g_ref[...]).astype(BF16)

    def project(w_ref, dst_ref):
        for r in range(0, dst_ref.shape[0], PROJ_ROWS):
            dst_ref[r:r + PROJ_ROWS, :] = _dot_nt(h_ref[r:r + PROJ_ROWS, :], w_ref[...])

    pl.when(j < n_a)(functools.partial(project, wa_ref, o_ref))
    pl.when(j >= n_a)(functools.partial(project, wu_ref, o_ref))
    pl.when(j == pl.num_programs(1) - 1)(functools.partial(project, wg_ref, og_ref))


def _inproj(x, g, w_in_t):
    n, d = x.shape
    assert w_in_t.shape[0] == COL_GATES + N_GATES and N_GATES <= PROJ_TG
    assert COL_U % PROJ_TN == 0 and D_POOL_COLS % PROJ_TN == 0
    wu = w_in_t[COL_U + N_GATES:]
    wg = jnp.pad(w_in_t[COL_U:COL_U + N_GATES], ((0, PROJ_TG - N_GATES), (0, 0)))
    n_a = COL_U // PROJ_TN
    n_u = D_POOL_COLS // PROJ_TN
    return pl.pallas_call(
        functools.partial(_inproj_body, n_a=n_a),
        grid=(n // PROJ_TM, n_a + n_u),
        in_specs=[
            pl.BlockSpec(memory_space=pl.ANY),
            pl.BlockSpec((1, d), lambda i, j: (0, 0)),
            pl.BlockSpec((PROJ_TN, d), lambda i, j: (jnp.minimum(j, n_a - 1), 0)),
            pl.BlockSpec((PROJ_TN, d), lambda i, j: (jnp.maximum(j - n_a, 0), 0)),
            pl.BlockSpec((PROJ_TG, d), lambda i, j: (0, 0), pipeline_mode=pl.Buffered(1)),
        ],
        out_specs=[pl.BlockSpec((PROJ_TM, PROJ_TN), lambda i, j: (i, j)),
                   pl.BlockSpec((PROJ_TM, PROJ_TG), lambda i, j: (i, 0))],
        out_shape=[jax.ShapeDtypeStruct((n, COL_GATES), F32), jax.ShapeDtypeStruct((n, PROJ_TG), F32)],
        scratch_shapes=[pltpu.VMEM((PROJ_TM, d), BF16), pltpu.VMEM((PROJ_TM, d), F32), pltpu.SemaphoreType.DMA(())],
        compiler_params=pltpu.CompilerParams(
            dimension_semantics=("arbitrary", "arbitrary"), vmem_limit_bytes=VMEM_LIMIT_BYTES),
        name="in_proj",
    )(x, g, w_in_t, wu, wg)


def _rel_bucket_np(n):
    max_exact = REL_BUCKETS // 2
    n = np.maximum(n, 0)
    nf = np.maximum(n, 1).astype(np.float32)
    large = max_exact + (np.log(nf / max_exact) / math.log(REL_MAX_DIST / max_exact)
                         * (REL_BUCKETS - max_exact)).astype(np.int32)
    large = np.minimum(large, REL_BUCKETS - 1)
    return np.where(n < max_exact, n, large)


FAR_DIST = int(np.max(np.nonzero(_rel_bucket_np(np.arange(4 * REL_MAX_DIST)) < REL_BUCKETS - 1)[0])) + 1
CMP_ROW_OFF = -(-(FAR_DIST + CMP_BLOCK - 1) // CMP_STRIDE) - 1
TAB_DIAG, TAB_PREV, TAB_FAR, TAB_EDGE = 0, 1, 2, 3


def _pick(rel_table, idx):
    onehot = jnp.asarray(np.arange(REL_BUCKETS)[:, None] == idx[None, :], F32)
    return jnp.einsum("kh,kn->hn", rel_table.astype(F32), onehot, precision=lax.Precision.HIGHEST)


def _group_lanes(t):
    t = t.reshape((N_KV_GROUPS, HEADS_PER_GROUP) + t.shape[1:])
    t = jnp.moveaxis(t, 1, -2)
    return t.reshape(t.shape[:-2] + (HEADS_PER_GROUP * t.shape[-1],))


def _toeplitz_body(p_ref, o_ref):
    blk = o_ref.shape[-1]
    for t in range(o_ref.shape[0]):
        y = pltpu.roll(jnp.broadcast_to(p_ref[t:t + 1, :], (blk, 2 * blk)), 0, 1, stride=1, stride_axis=0)
        o_ref[t] = y[:, :blk]


def _bias_tables(rel_table, n_cmp):
    blk = ATT_BLK
    n_heads = rel_table.shape[1]
    far = REL_BUCKETS - 1
    assert blk >= FAR_DIST and WINDOW % blk == 0
    m = np.arange(2 * blk)
    d = np.where(m < blk, m, m - 2 * blk)
    idx = np.stack([_rel_bucket_np(d), _rel_bucket_np(blk + d), np.full_like(d, far), np.full_like(d, far)])
    mask = np.stack([np.where(d < 0, NEG, 0.0), np.zeros_like(d, np.float64), np.zeros_like(d, np.float64),
                     np.where(d >= 0, NEG, 0.0)]).astype(np.float32)
    profiles = _pick(rel_table, idx.reshape(-1)).reshape(n_heads, 4, 2 * blk) + mask[None]
    sweep = pl.pallas_call(
        _toeplitz_body,
        grid=(N_KV_GROUPS, HEADS_PER_GROUP),
        in_specs=[pl.BlockSpec((None, 4, 2 * blk), lambda g, h: (g * HEADS_PER_GROUP + h, 0, 0))],
        out_specs=pl.BlockSpec((None, 4, blk, blk), lambda g, h: (g, 0, 0, h)),
        out_shape=jax.ShapeDtypeStruct((N_KV_GROUPS, 4, blk, HEADS_PER_GROUP * blk), F32),
        name="bias_tables",
    )(profiles)

    rows = np.arange(n_cmp - CMP_ROW_OFF, n_cmp + (blk - CMP_BLOCK) // CMP_STRIDE + 1)
    dist = np.arange(blk)[None, :] - ((rows[:, None] - n_cmp) * CMP_STRIDE + CMP_BLOCK - 1)
    band = _pick(rel_table, _rel_bucket_np(dist).reshape(-1)).reshape(n_heads, len(rows), blk)
    band = band + np.where(dist < 0, NEG, 0.0).astype(np.float32)[None]
    before = jnp.broadcast_to(rel_table[far].astype(F32)[:, None, None], (n_heads, int(rows[0]), blk))
    after = jnp.full((n_heads, 2 * n_cmp - int(rows[-1]) - 1, blk), NEG, F32)
    cmp_tab = _group_lanes(jnp.concatenate([before, band, after], axis=1))
    return sweep, cmp_tab


def _compress(kv_ref, pos_ref, w1_ref, w2_ref):
    seq, dk = kv_ref.shape
    n_half = seq // CMP_STRIDE
    first = jnp.zeros((n_half, w1_ref.shape[1]), F32)
    second = jnp.zeros((n_half, w1_ref.shape[1]), F32)
    for l in range(CMP_STRIDE):
        rows = kv_ref[pl.ds(l, n_half, stride=CMP_STRIDE), :]
        a = (rows + pos_ref[l:l + 1, :]).astype(BF16)
        b = (rows + pos_ref[CMP_STRIDE + l:CMP_STRIDE + l + 1, :]).astype(BF16)
        first += _dot(a, w1_ref[l * dk:(l + 1) * dk, :])
        second += _dot(b, w1_ref[(CMP_STRIDE + l) * dk:(CMP_STRIDE + l + 1) * dk, :])
    pre = first + pltpu.roll(second, n_half - 1, 0)
    hid = (pre * jax.nn.sigmoid(pre)).astype(BF16)
    return _dot(hid, w2_ref[...])


N_NSA_INPUTS = 18


def _nsa_body(*refs, n_cast):
    (q_ref, kc_ref, vc_ref, ks_ref, vs_ref, kw_ref, vw_ref, gates_ref,
     posk_ref, wk1_ref, wk2_ref, posv_ref, wv1_ref, wv2_ref,
     tab_ref, tabc_ref, overlap_ref, expand_ref) = refs[:N_NSA_INPUTS]
    cast_in = refs[N_NSA_INPUTS:N_NSA_INPUTS + n_cast]
    o_ref = refs[N_NSA_INPUTS + n_cast]
    cast_out = refs[N_NSA_INPUTS + n_cast + 1:N_NSA_INPUTS + 2 * n_cast + 1]
    (kcmp_ref, vcmp_t_ref, ksel_ref, vsel_t_ref, kwin_ref, vwin_t_ref, gates_t_ref,
     s_ref) = refs[N_NSA_INPUTS + 2 * n_cast + 1:]
    g = pl.program_id(1)
    qi = pl.program_id(2)
    tq = ATT_BLK
    hpg = HEADS_PER_GROUP
    lanes = hpg * tq
    n_cmp = kcmp_ref.shape[0]
    n_sb = overlap_ref.shape[0]
    seq = ks_ref.shape[0]
    assert tq & (tq - 1) == 0 and SEL_BLOCK & (SEL_BLOCK - 1) == 0

    @pl.when(qi == 0)
    def _():
        kcmp_ref[...] = _compress(kc_ref, posk_ref, wk1_ref, wk2_ref).astype(BF16)
        vcmp_t_ref[...] = _compress(vc_ref, posv_ref, wv1_ref, wv2_ref).astype(BF16).T
        for c in range(0, seq, LANES):
            ksel_ref[c:c + LANES, 0:HEAD_DIM] = ks_ref[c:c + LANES, :].astype(BF16)
            ksel_ref[c:c + LANES, HEAD_DIM:] = expand_ref[c:c + LANES, :]
            kwin_ref[c:c + LANES, :] = kw_ref[c:c + LANES, :].astype(BF16)
            vsel_t_ref[:, c:c + LANES] = vs_ref[c:c + LANES, :].T.astype(BF16)
            vwin_t_ref[:, c:c + LANES] = vw_ref[c:c + LANES, :].T.astype(BF16)

    q = q_ref[...] * (HEAD_DIM ** -0.5)
    q_t = jnp.concatenate([q[:, h * HEAD_DIM:(h + 1) * HEAD_DIM].T for h in range(hpg)],
                          axis=1).astype(BF16)
    t_lane = qi * tq + (lax.broadcasted_iota(jnp.int32, (1, lanes), 1) & (tq - 1))

    first_row = pl.multiple_of(n_cmp - qi * (tq // CMP_STRIDE), 8)
    logit = _dot(kcmp_ref[...], q_t) + tabc_ref[pl.ds(first_row, n_cmp), :]
    e = jnp.exp(logit - jnp.max(logit, axis=0, keepdims=True))
    any_valid = (t_lane >= CMP_BLOCK - 1).astype(F32)
    p = e / jnp.sum(e, axis=0, keepdims=True) * any_valid
    o_cmp = _dot(vcmp_t_ref[...], p.astype(BF16))

    psum = p[:, 0:tq]
    for h in range(1, hpg):
        psum = psum + p[:, h * tq:(h + 1) * tq]
    p_hi = psum.astype(BF16)
    r1 = psum - p_hi.astype(F32)
    p_mid = r1.astype(BF16)
    p_lo = (r1 - p_mid.astype(F32)).astype(BF16)
    ov = overlap_ref[...]
    imp = _dot(ov, p_hi) + _dot(ov, p_mid) + _dot(ov, p_lo)
    cur = lax.shift_right_logical(t_lane[:, 0:tq], int(math.log2(SEL_BLOCK)))
    jb = lax.broadcasted_iota(jnp.int32, (n_sb, tq), 0)
    forced = (jb == 0) | (jb == cur) | (jb == cur - 1)
    score = jnp.where(jb > cur, FUTURE_SCORE, imp + jnp.where(forced, FORCE_BONUS, 0.0))
    rank = jnp.zeros((n_sb, tq), jnp.int32)
    for j2 in range(n_sb):
        row = score[j2:j2 + 1, :]
        rank += ((row > score) | ((row == score) & (j2 < jb))).astype(jnp.int32)
    sel_neg = jnp.where(rank < SEL_TOP, 0.0, NEG).astype(BF16)
    n_aug = ksel_ref.shape[1] - HEAD_DIM
    q_sel = jnp.concatenate([q_t, jnp.concatenate([sel_neg] * hpg, axis=1),
                             jnp.zeros((n_aug - n_sb, lanes), BF16)], axis=0)

    def sweep(k_ref, v_t_ref, q_rhs, lo, edge):
        def rows(kb):
            return pl.ds(pl.multiple_of(kb * ATT_BLK, ATT_BLK), ATT_BLK)

        def scores(kb, m):
            dblk = qi - kb
            tab = jnp.minimum(dblk, TAB_FAR)
            if edge is not None:
                tab = jnp.where(dblk == edge, TAB_EDGE, tab)
            s = _dot(k_ref[rows(kb), :], q_rhs) + tab_ref[tab]
            s_ref[rows(kb), :] = s
            return jnp.maximum(m, jnp.max(s, axis=0, keepdims=True))

        def over_chunks(body, init):
            n = qi + 1 - lo

            def run(first, count):
                def f(c):
                    for t in range(count):
                        c = body(first + t, c)
                    return c
                return f

            def looped(c):
                out = lax.fori_loop(0, n // 4, lambda i, c: run(lo + 4 * i, 4)(c), c)
                rest, first = n % 4, lo + 4 * (n // 4)
                return lax.cond(rest >= 2,
                                lambda c: lax.cond(rest == 3, run(first, 3), run(first, 2), c),
                                lambda c: lax.cond(rest == 1, run(first, 1), run(first, 0), c), out)

            if edge is None:
                return looped(init)

            def straight(c):
                for back in range(edge, -1, -1):
                    c = body(qi - back, c)
                return c

            return lax.cond(qi >= edge, straight, looped, init)

        m = over_chunks(scores, jnp.full((1, lanes), NEG, F32))

        def accumulate(kb, carry):
            l, acc = carry
            pr = jnp.exp(s_ref[rows(kb), :] - m)
            return (l + jnp.sum(pr, axis=0, keepdims=True),
                    acc + _dot(v_t_ref[:, rows(kb)], pr.astype(BF16)))

        l, acc = over_chunks(accumulate, (jnp.zeros((1, lanes), F32), jnp.zeros((HEAD_DIM, lanes), F32)))
        return acc / l

    o_sel = sweep(ksel_ref, vsel_t_ref, q_sel, 0, None)
    n_back = WINDOW // ATT_BLK
    o_win = sweep(kwin_ref, vwin_t_ref, q_t, jnp.maximum(qi - n_back, 0), n_back)

    for c in range(0, tq, LANES):
        gates_t_ref[:, c:c + LANES] = jax.nn.sigmoid(gates_ref[c:c + LANES, :]).T
    for h in range(hpg):
        head = g * hpg + h
        sl = slice(h * tq, (h + 1) * tq)
        y = (gates_t_ref[pl.ds(head, 1), :] * o_cmp[:, sl]
             + gates_t_ref[pl.ds(N_ATT_HEADS + head, 1), :] * o_sel[:, sl]
             + gates_t_ref[pl.ds(2 * N_ATT_HEADS + head, 1), :] * o_win[:, sl])
        for c in range(0, tq, LANES):
            o_ref[c:c + LANES, h * HEAD_DIM:(h + 1) * HEAD_DIM] = y[:, c:c + LANES].T.astype(o_ref.dtype)

    for src, dst in zip(cast_in, cast_out):
        dst[...] = src[...].astype(BF16)


def _nsa(proj, gates, batch, seq, cmp_pos_k, w_ck1, w_ck2, cmp_pos_v, w_cv1, w_cv2, rel_table, to_cast=()):
    n_qt = seq // ATT_BLK
    hpg = HEADS_PER_GROUP
    n_cmp = seq // CMP_STRIDE
    n_sb = seq // SEL_BLOCK
    n_aug = LANES
    assert CMP_BLOCK == 2 * CMP_STRIDE and n_cmp % 8 == 0 and n_sb % 16 == 0 and n_sb <= n_aug
    assert (ATT_BLK // CMP_STRIDE) % 8 == 0 and n_qt * (ATT_BLK // CMP_STRIDE) <= n_cmp

    sweep_tab, cmp_tab = _bias_tables(rel_table, n_cmp)
    cs = np.arange(n_cmp)[None, :] * CMP_STRIDE
    ss = np.arange(n_sb)[:, None] * SEL_BLOCK
    overlap = (cs < ss + SEL_BLOCK) & (cs + CMP_BLOCK > ss) & (np.arange(n_cmp)[None, :] < n_cmp - 1)
    expand = np.arange(seq)[:, None] // SEL_BLOCK == np.arange(n_aug)[None, :]
    overlap = jnp.asarray(overlap, BF16)
    expand = jnp.asarray(expand, BF16)

    kv_col0 = COL_KV // HEAD_DIM

    def kv_spec(idx):
        return pl.BlockSpec((seq, HEAD_DIM), lambda b, g, qi, idx=idx: (b, kv_col0 + idx * N_KV_GROUPS + g))

    def whole(arr):
        return pl.BlockSpec(arr.shape, lambda b, g, qi, nd=arr.ndim: (0,) * nd)

    in_specs = [
        pl.BlockSpec((ATT_BLK, hpg * HEAD_DIM), lambda b, g, qi: (b * n_qt + qi, g)),
        kv_spec(0), kv_spec(1), kv_spec(2), kv_spec(3), kv_spec(4), kv_spec(5),
        pl.BlockSpec((ATT_BLK, LANES), lambda b, g, qi: (b * n_qt + qi, 0)),
        whole(cmp_pos_k), whole(w_ck1), whole(w_ck2), whole(cmp_pos_v), whole(w_cv1), whole(w_cv2),
        pl.BlockSpec((None,) + sweep_tab.shape[1:], lambda b, g, qi: (g, 0, 0, 0)),
        pl.BlockSpec((None,) + cmp_tab.shape[1:], lambda b, g, qi: (g, 0, 0)),
        whole(overlap), whole(expand),
    ]
    assert len(in_specs) == N_NSA_INPUTS
    n_steps = batch * N_KV_GROUPS * n_qt
    cast_specs = [_stripe_spec(a, n_steps, lambda b, g, qi: (b * N_KV_GROUPS + g) * n_qt + qi) for a in to_cast]
    outs = pl.pallas_call(
        functools.partial(_nsa_body, n_cast=len(to_cast)),
        grid=(batch, N_KV_GROUPS, n_qt),
        in_specs=in_specs + cast_specs,
        out_specs=[pl.BlockSpec((ATT_BLK, hpg * HEAD_DIM), lambda b, g, qi: (b * n_qt + qi, g))] + cast_specs,
        out_shape=[jax.ShapeDtypeStruct((batch * seq, D_ATT), BF16)]
        + [jax.ShapeDtypeStruct(a.shape, BF16) for a in to_cast],
        scratch_shapes=[pltpu.VMEM((n_cmp, HEAD_DIM), BF16), pltpu.VMEM((HEAD_DIM, n_cmp), BF16),
                        pltpu.VMEM((seq, HEAD_DIM + n_aug), BF16), pltpu.VMEM((HEAD_DIM, seq), BF16),
                        pltpu.VMEM((seq, HEAD_DIM), BF16), pltpu.VMEM((HEAD_DIM, seq), BF16),
                        pltpu.VMEM((LANES, ATT_BLK), F32), pltpu.VMEM((seq, hpg * ATT_BLK), F32)],
        compiler_params=pltpu.CompilerParams(
            dimension_semantics=(("arbitrary",) * 3 if to_cast else ("parallel", "parallel", "arbitrary")),
            vmem_limit_bytes=VMEM_LIMIT_BYTES),
        name="nsa",
    )(proj, proj, proj, proj, proj, proj, proj, gates,
      cmp_pos_k, w_ck1.astype(BF16), w_ck2.astype(BF16), cmp_pos_v, w_cv1.astype(BF16), w_cv2.astype(BF16),
      sweep_tab, cmp_tab, overlap, expand, *to_cast)
    return outs[0], outs[1:]


def _pool_body(u_ref, w_ref, scale_ref, o_ref):
    gi = pl.program_id(1)
    u = u_ref[...]
    seq = u.shape[0]
    t = lax.broadcasted_iota(jnp.int32, (seq, 1), 0)

    def shifted(x, k):
        return jnp.where(t >= k, pltpu.roll(x, k, 0), 0.0)

    sums = [u]
    for n in range(int(math.log2(POOL_WINDOWS[-1]))):
        sums.append(sums[-1] + shifted(sums[-1], 2 ** n))
    wsum = sums[int(math.log2(POOL_WINDOWS[0]))]
    for idx in range(1, len(POOL_WINDOWS)):
        wsum = jnp.where(gi >= idx, sums[int(math.log2(POOL_WINDOWS[idx]))], wsum)
    window = POOL_WINDOWS[0]
    for idx in range(1, len(POOL_WINDOWS)):
        window = jnp.where(gi >= idx, POOL_WINDOWS[idx], window)
    cnt = jnp.minimum(t + 1, window).astype(F32)
    d = (wsum / cnt - u).astype(BF16)
    o_ref[...] = (_dot(d, w_ref[0]) * scale_ref[...]).astype(o_ref.dtype)


def _pool(proj, batch, seq, w_pool, pool_scale):
    n_g, dg, _ = w_pool.shape
    assert all(w == 2 ** int(math.log2(w)) for w in POOL_WINDOWS) and list(POOL_WINDOWS) == sorted(POOL_WINDOWS)
    col0 = COL_U // dg
    return pl.pallas_call(
        _pool_body,
        grid=(batch, n_g),
        in_specs=[
            pl.BlockSpec((seq, dg), lambda b, gi: (b, col0 + gi)),
            pl.BlockSpec((1, dg, dg), lambda b, gi: (gi, 0, 0)),
            pl.BlockSpec((1, dg), lambda b, gi: (0, gi)),
        ],
        out_specs=pl.BlockSpec((seq, dg), lambda b, gi: (b, gi)),
        out_shape=jax.ShapeDtypeStruct((batch * seq, n_g * dg), BF16),
        compiler_params=pltpu.CompilerParams(
            dimension_semantics=("parallel", "parallel"), vmem_limit_bytes=VMEM_LIMIT_BYTES),
        name="pool",
    )(proj, w_pool.astype(BF16), pool_scale.reshape(1, -1))


def _outproj_body(x_ref, ya_ref, yp_ref, wa_ref, wp_ref, o_ref):
    o_ref[...] = x_ref[...] + _dot(ya_ref[...], wa_ref[...]) + _dot(yp_ref[...], wp_ref[...])


def _outproj(x, y_att, y_pool, w_out):
    n, d = x.shape
    da = y_att.shape[1]
    dp = y_pool.shape[1]
    w = w_out
    return pl.pallas_call(
        _outproj_body,
        grid=(d // OUT_TN, n // OUT_TM),
        in_specs=[
            pl.BlockSpec((OUT_TM, OUT_TN), lambda j, i: (i, j)),
            pl.BlockSpec((OUT_TM, da), lambda j, i: (i, 0)),
            pl.BlockSpec((OUT_TM, dp), lambda j, i: (i, 0)),
            pl.BlockSpec((da, OUT_TN), lambda j, i: (0, j)),
            pl.BlockSpec((dp, OUT_TN), lambda j, i: (da // dp, j)),
        ],
        out_specs=pl.BlockSpec((OUT_TM, OUT_TN), lambda j, i: (i, j)),
        out_shape=jax.ShapeDtypeStruct((n, d), F32),
        compiler_params=pltpu.CompilerParams(
            dimension_semantics=("parallel", "parallel"), vmem_limit_bytes=VMEM_LIMIT_BYTES),
        name="out_proj",
    )(x, y_att, y_pool, w, w)


def kernel(x, norm_ffn1, w_ffn1_gate, w_ffn1_up, w_ffn1_down, norm_mix, w_in, cmp_pos_k, w_cmp_k1, w_cmp_k2,
           cmp_pos_v, w_cmp_v1, w_cmp_v2, w_pool, pool_scale, w_out, rel_table, norm_ffn2, w_ffn2_gate,
           w_ffn2_up, w_ffn2_down, norm_final):
    batch, seq, d = x.shape
    depth = norm_ffn1.shape[0]
    xf = x.reshape(batch * seq, d)
    gf = norm_final.reshape(1, d)
    for l in range(depth):
        xf, (w_in_t,) = _ffn(
            xf, norm_ffn1[l].reshape(1, d), w_ffn1_gate[l].astype(BF16), w_ffn1_up[l].astype(BF16),
            w_ffn1_down[l].astype(BF16), gf, final_norm=False, to_cast=(w_in[l].T,))
        proj, gates = _inproj(xf, norm_mix[l].reshape(1, d), w_in_t)
        y_att, (w2_gate, w2_up, w2_down, w_out_b) = _nsa(
            proj, gates, batch, seq, cmp_pos_k[l], w_cmp_k1[l], w_cmp_k2[l], cmp_pos_v[l], w_cmp_v1[l], w_cmp_v2[l],
            rel_table, to_cast=(w_ffn2_gate[l], w_ffn2_up[l], w_ffn2_down[l], w_out[l]))
        y_pool = _pool(proj, batch, seq, w_pool[l], pool_scale[l])
        xf = _outproj(xf, y_att, y_pool, w_out_b)
        xf, _ = _ffn(xf, norm_ffn2[l].reshape(1, d), w2_gate, w2_up, w2_down, gf, final_norm=(l == depth - 1))
    if depth == 0:
        raise ValueError("depth must be positive")
    return xf.reshape(batch, seq, d)
```

```python
import functools
import math

import jax
import jax.numpy as jnp
import numpy as np
from jax import lax
from jax.experimental import pallas as pl
from jax.experimental.pallas import tpu as pltpu

HEAD_DIM = 128
N_ATT_HEADS = 16
N_KV_GROUPS = 4
HEADS_PER_GROUP = N_ATT_HEADS // N_KV_GROUPS
D_ATT = N_ATT_HEADS * HEAD_DIM
D_KV = N_KV_GROUPS * HEAD_DIM
N_POOL_GROUPS = 4
POOL_WINDOWS = (2, 4, 8, 16)
CMP_BLOCK = 32
CMP_STRIDE = 16
SEL_BLOCK = 64
SEL_TOP = 8
WINDOW = 512
REL_BUCKETS = 32
REL_MAX_DIST = 128
EPS = 1e-6
NEG = -1e30
FORCE_BONUS = 1e4
FUTURE_SCORE = -1e9

LANES = 128
MXU_DIM = 256
VMEM_LIMIT_BYTES = 60 * 2 ** 20

FFN_TM = 512
FFN_TF = 2 * MXU_DIM
FFN_TN = 1024
NORM_ROWS = 32
CAST_ROWS = 16
PROJ_TM = 1024
PROJ_ROWS = 512
PROJ_TN = 512
PROJ_TG = MXU_DIM
ATT_BLK = 256
OUT_TM = 1024
OUT_TN = 1024

BF16 = jnp.bfloat16
F32 = jnp.float32


def _dot(a, b):
    return jnp.dot(a, b, preferred_element_type=F32)


def _dot_nt(a, b):
    return lax.dot_general(a, b, (((1,), (1,)), ((), ())), preferred_element_type=F32)


def _rms(x, g):
    return x * lax.rsqrt(jnp.mean(x * x, axis=-1, keepdims=True) + EPS) * g


def _prefetch_row_tile(x_hbm, xbuf_ref, sem):
    i = pl.program_id(0)
    j = pl.program_id(1)
    tm = xbuf_ref.shape[0]

    def x_copy(tile):
        return pltpu.make_async_copy(x_hbm.at[pl.ds(pl.multiple_of(tile * tm, tm), tm), :], xbuf_ref, sem)

    @pl.when((i == 0) & (j == 0))
    def _():
        x_copy(0).start()

    @pl.when(j == 0)
    def _():
        x_copy(i).wait()

    @pl.when((j == 1) & (i + 1 < pl.num_programs(0)))
    def _():
        x_copy(i + 1).start()


def _ffn_body(*refs, final_norm, n_cast, tf, tail):
    x_hbm, g_ref, wg_ref, wu_ref, wd_ref, gf_ref = refs[:6]
    cast_in = refs[6:6 + n_cast]
    o_ref = refs[6 + n_cast]
    cast_out = refs[7 + n_cast:7 + 2 * n_cast]
    h_ref, x_tile, sem = refs[7 + 2 * n_cast:]
    j = pl.program_id(1)
    last = pl.num_programs(1) - 1
    tm = o_ref.shape[0]
    _prefetch_row_tile(x_hbm, x_tile, sem)

    @pl.when(j == 0)
    def _():
        for r in range(0, tm, NORM_ROWS):
            x = x_tile[r:r + NORM_ROWS, :]
            h_ref[r:r + NORM_ROWS, :] = _rms(x, g_ref[...]).astype(BF16)
            o_ref[r:r + NORM_ROWS, :] = x

    def accumulate(width):
        h = h_ref[...]
        gate = _dot(h, wg_ref[:, :width])
        up = _dot(h, wu_ref[:, :width])
        act = (0.5 * (gate * jax.nn.sigmoid(gate)) * up).astype(BF16)
        for c in range(0, o_ref.shape[1], FFN_TN):
            o_ref[:, c:c + FFN_TN] += _dot(act, wd_ref[:width, c:c + FFN_TN])

    if tail == tf:
        accumulate(tf)
    else:
        pl.when(j < last)(functools.partial(accumulate, tf))
        pl.when(j == last)(functools.partial(accumulate, tail))

    if final_norm:
        @pl.when(j == last)
        def _():
            for r in range(0, tm, NORM_ROWS):
                o_ref[r:r + NORM_ROWS, :] = _rms(o_ref[r:r + NORM_ROWS, :], gf_ref[...])

    for src, dst in zip(cast_in, cast_out):
        dst[...] = src[...].astype(BF16)


def _stripe_spec(a, n_steps, step_of):
    r, c = a.shape
    rows = CAST_ROWS
    while pl.cdiv(r, rows) > n_steps:
        rows += CAST_ROWS
    n_blocks = pl.cdiv(r, rows)
    return pl.BlockSpec((rows, c), lambda *idx: (jnp.minimum(step_of(*idx), n_blocks - 1), 0))


def _ffn(x, g, w_gate, w_up, w_down, gf, *, final_norm, to_cast=()):
    n, d = x.shape
    d_ff = w_gate.shape[1]
    tf = FFN_TF
    n_i, n_j = n // FFN_TM, pl.cdiv(d_ff, tf)
    tail = d_ff - (n_j - 1) * tf
    assert tail % MXU_DIM == 0 and n_j >= 2
    cast_specs = [_stripe_spec(a, n_i * n_j, lambda i, j: i * n_j + j) for a in to_cast]
    outs = pl.pallas_call(
        functools.partial(_ffn_body, final_norm=final_norm, n_cast=len(to_cast), tf=tf, tail=tail),
        grid=(n_i, n_j),
        in_specs=[
            pl.BlockSpec(memory_space=pl.ANY),
            pl.BlockSpec((1, d), lambda i, j: (0, 0)),
            pl.BlockSpec((d, tf), lambda i, j: (0, j)),
            pl.BlockSpec((d, tf), lambda i, j: (0, j)),
            pl.BlockSpec((tf, d), lambda i, j: (j, 0)),
            pl.BlockSpec((1, d), lambda i, j: (0, 0)),
        ] + cast_specs,
        out_specs=[pl.BlockSpec((FFN_TM, d), lambda i, j: (i, 0))] + cast_specs,
        out_shape=[jax.ShapeDtypeStruct((n, d), F32)] + [jax.ShapeDtypeStruct(a.shape, BF16) for a in to_cast],
        scratch_shapes=[pltpu.VMEM((FFN_TM, d), BF16), pltpu.VMEM((FFN_TM, d), F32), pltpu.SemaphoreType.DMA(())],
        compiler_params=pltpu.CompilerParams(
            dimension_semantics=("arbitrary", "arbitrary"), vmem_limit_bytes=VMEM_LIMIT_BYTES),
        name="ffn_final" if final_norm else "ffn",
    )(x, g, w_gate, w_up, w_down, gf, *to_cast)
    return outs[0], outs[1:]


COL_Q = 0
COL_KV = D_ATT
COL_U = D_ATT + 6 * D_KV
D_POOL_COLS = 2048
COL_GATES = COL_U + D_POOL_COLS
N_GATES = 3 * N_ATT_HEADS


def _inproj_body(x_hbm, g_ref, wa_ref, wu_ref, wg_ref, o_ref, og_ref, h_ref, xbuf_ref, sem, *, n_a):
    j = pl.program_id(1)
    _prefetch_row_tile(x_hbm, xbuf_ref, sem)

    @pl.when(j == 0)
    def _():
        for r in range(0, xbuf_ref.shape[0], NORM_ROWS):
            h_ref[r:r + NORM_ROWS, :] = _rms(xbuf_ref[r:r + NORM_ROWS, :], g_ref[...]).astype(BF16)

    def project(w_ref, dst_ref):
        for r in range(0, dst_ref.shape[0], PROJ_ROWS):
            dst_ref[r:r + PROJ_ROWS, :] = _dot_nt(h_ref[r:r + PROJ_ROWS, :], w_ref[...])

    pl.when(j < n_a)(functools.partial(project, wa_ref, o_ref))
    pl.when(j >= n_a)(functools.partial(project, wu_ref, o_ref))
    pl.when(j == pl.num_programs(1) - 1)(functools.partial(project, wg_ref, og_ref))


def _inproj(x, g, w_in_t):
    n, d = x.shape
    assert w_in_t.shape[0] == COL_GATES + N_GATES and N_GATES <= PROJ_TG
    assert COL_U % PROJ_TN == 0 and D_POOL_COLS % PROJ_TN == 0
    wu = w_in_t[COL_U + N_GATES:]
    wg = jnp.pad(w_in_t[COL_U:COL_U + N_GATES], ((0, PROJ_TG - N_GATES), (0, 0)))
    n_a = COL_U // PROJ_TN
    n_u = D_POOL_COLS // PROJ_TN
    return pl.pallas_call(
        functools.partial(_inproj_body, n_a=n_a),
        grid=(n // PROJ_TM, n_a + n_u),
        in_specs=[
            pl.BlockSpec(memory_space=pl.ANY),
            pl.BlockSpec((1, d), lambda i, j: (0, 0)),
            pl.BlockSpec((PROJ_TN, d), lambda i, j: (jnp.minimum(j, n_a - 1), 0)),
            pl.BlockSpec((PROJ_TN, d), lambda i, j: (jnp.maximum(j - n_a, 0), 0)),
            pl.BlockSpec((PROJ_TG, d), lambda i, j: (0, 0), pipeline_mode=pl.Buffered(1)),
        ],
        out_specs=[pl.BlockSpec((PROJ_TM, PROJ_TN), lambda i, j: (i, j)),
                   pl.BlockSpec((PROJ_TM, PROJ_TG), lambda i, j: (i, 0))],
        out_shape=[jax.ShapeDtypeStruct((n, COL_GATES), F32), jax.ShapeDtypeStruct((n, PROJ_TG), F32)],
        scratch_shapes=[pltpu.VMEM((PROJ_TM, d), BF16), pltpu.VMEM((PROJ_TM, d), F32), pltpu.SemaphoreType.DMA(())],
        compiler_params=pltpu.CompilerParams(
            dimension_semantics=("arbitrary", "arbitrary"), vmem_limit_bytes=VMEM_LIMIT_BYTES),
        name="in_proj",
    )(x, g, w_in_t, wu, wg)


def _rel_bucket_np(n):
    max_exact = REL_BUCKETS // 2
    n = np.maximum(n, 0)
    nf = np.maximum(n, 1).astype(np.float32)
    large = max_exact + (np.log(nf / max_exact) / math.log(REL_MAX_DIST / max_exact)
                         * (REL_BUCKETS - max_exact)).astype(np.int32)
    large = np.minimum(large, REL_BUCKETS - 1)
    return np.where(n < max_exact, n, large)


FAR_DIST = int(np.max(np.nonzero(_rel_bucket_np(np.arange(4 * REL_MAX_DIST)) < REL_BUCKETS - 1)[0])) + 1
CMP_ROW_OFF = -(-(FAR_DIST + CMP_BLOCK - 1) // CMP_STRIDE) - 1
TAB_DIAG, TAB_PREV, TAB_FAR, TAB_EDGE = 0, 1, 2, 3


def _pick(rel_table, idx):
    onehot = jnp.asarray(np.arange(REL_BUCKETS)[:, None] == idx[None, :], F32)
    return jnp.einsum("kh,kn->hn", rel_table.astype(F32), onehot, precision=lax.Precision.HIGHEST)


def _group_lanes(t):
    t = t.reshape((N_KV_GROUPS, HEADS_PER_GROUP) + t.shape[1:])
    t = jnp.moveaxis(t, 1, -2)
    return t.reshape(t.shape[:-2] + (HEADS_PER_GROUP * t.shape[-1],))


def _toeplitz_body(p_ref, o_ref):
    blk = o_ref.shape[-1]
    for t in range(o_ref.shape[0]):
        y = pltpu.roll(jnp.broadcast_to(p_ref[t:t + 1, :], (blk, 2 * blk)), 0, 1, stride=1, stride_axis=0)
        o_ref[t] = y[:, :blk]


def _bias_tables(rel_table, n_cmp):
    blk = ATT_BLK
    n_heads = rel_table.shape[1]
    far = REL_BUCKETS - 1
    assert blk >= FAR_DIST and WINDOW % blk == 0
    m = np.arange(2 * blk)
    d = np.where(m < blk, m, m - 2 * blk)
    idx = np.stack([_rel_bucket_np(d), _rel_bucket_np(blk + d), np.full_like(d, far), np.full_like(d, far)])
    mask = np.stack([np.where(d < 0, NEG, 0.0), np.zeros_like(d, np.float64), np.zeros_like(d, np.float64),
                     np.where(d >= 0, NEG, 0.0)]).astype(np.float32)
    profiles = _pick(rel_table, idx.reshape(-1)).reshape(n_heads, 4, 2 * blk) + mask[None]
    sweep = pl.pallas_call(
        _toeplitz_body,
        grid=(N_KV_GROUPS, HEADS_PER_GROUP),
        in_specs=[pl.BlockSpec((None, 4, 2 * blk), lambda g, h: (g * HEADS_PER_GROUP + h, 0, 0))],
        out_specs=pl.BlockSpec((None, 4, blk, blk), lambda g, h: (g, 0, 0, h)),
        out_shape=jax.ShapeDtypeStruct((N_KV_GROUPS, 4, blk, HEADS_PER_GROUP * blk), F32),
        name="bias_tables",
    )(profiles)

    rows = np.arange(n_cmp - CMP_ROW_OFF, n_cmp + (blk - CMP_BLOCK) // CMP_STRIDE + 1)
    dist = np.arange(blk)[None, :] - ((rows[:, None] - n_cmp) * CMP_STRIDE + CMP_BLOCK - 1)
    band = _pick(rel_table, _rel_bucket_np(dist).reshape(-1)).reshape(n_heads, len(rows), blk)
    band = band + np.where(dist < 0, NEG, 0.0).astype(np.float32)[None]
    before = jnp.broadcast_to(rel_table[far].astype(F32)[:, None, None], (n_heads, int(rows[0]), blk))
    after = jnp.full((n_heads, 2 * n_cmp - int(rows[-1]) - 1, blk), NEG, F32)
    cmp_tab = _group_lanes(jnp.concatenate([before, band, after], axis=1))
    return sweep, cmp_tab


def _compress(kv_ref, pos_ref, w1_ref, w2_ref):
    seq, dk = kv_ref.shape
    n_half = seq // CMP_STRIDE
    first = jnp.zeros((n_half, w1_ref.shape[1]), F32)
    second = jnp.zeros((n_half, w1_ref.shape[1]), F32)
    for l in range(CMP_STRIDE):
        rows = kv_ref[pl.ds(l, n_half, stride=CMP_STRIDE), :]
        a = (rows + pos_ref[l:l + 1, :]).astype(BF16)
        b = (rows + pos_ref[CMP_STRIDE + l:CMP_STRIDE + l + 1, :]).astype(BF16)
        first += _dot(a, w1_ref[l * dk:(l + 1) * dk, :])
        second += _dot(b, w1_ref[(CMP_STRIDE + l) * dk:(CMP_STRIDE + l + 1) * dk, :])
    pre = first + pltpu.roll(second, n_half - 1, 0)
    hid = (pre * jax.nn.sigmoid(pre)).astype(BF16)
    return _dot(hid, w2_ref[...])


N_NSA_INPUTS = 18


def _nsa_body(*refs, n_cast):
    (q_ref, kc_ref, vc_ref, ks_ref, vs_ref, kw_ref, vw_ref, gates_ref,
     posk_ref, wk1_ref, wk2_ref, posv_ref, wv1_ref, wv2_ref,
     tab_ref, tabc_ref, overlap_ref, expand_ref) = refs[:N_NSA_INPUTS]
    cast_in = refs[N_NSA_INPUTS:N_NSA_INPUTS + n_cast]
    o_ref = refs[N_NSA_INPUTS + n_cast]
    cast_out = refs[N_NSA_INPUTS + n_cast + 1:N_NSA_INPUTS + 2 * n_cast + 1]
    (kcmp_ref, vcmp_t_ref, ksel_ref, vsel_t_ref, kwin_ref, vwin_t_ref, gates_t_ref,
     s_ref) = refs[N_NSA_INPUTS + 2 * n_cast + 1:]
    g = pl.program_id(1)
    qi = pl.program_id(2)
    tq = ATT_BLK
    hpg = HEADS_PER_GROUP
    lanes = hpg * tq
    n_cmp = kcmp_ref.shape[0]
    n_sb = overlap_ref.shape[0]
    seq = ks_ref.shape[0]
    assert tq & (tq - 1) == 0 and SEL_BLOCK & (SEL_BLOCK - 1) == 0

    @pl.when(qi == 0)
    def _():
        kcmp_ref[...] = _compress(kc_ref, posk_ref, wk1_ref, wk2_ref).astype(BF16)
        vcmp_t_ref[...] = _compress(vc_ref, posv_ref, wv1_ref, wv2_ref).astype(BF16).T
        for c in range(0, seq, LANES):
            ksel_ref[c:c + LANES, 0:HEAD_DIM] = ks_ref[c:c + LANES, :].astype(BF16)
            ksel_ref[c:c + LANES, HEAD_DIM:] = expand_ref[c:c + LANES, :]
            kwin_ref[c:c + LANES, :] = kw_ref[c:c + LANES, :].astype(BF16)
            vsel_t_ref[:, c:c + LANES] = vs_ref[c:c + LANES, :].T.astype(BF16)
            vwin_t_ref[:, c:c + LANES] = vw_ref[c:c + LANES, :].T.astype(BF16)

    q = q_ref[...] * (HEAD_DIM ** -0.5)
    q_t = jnp.concatenate([q[:, h * HEAD_DIM:(h + 1) * HEAD_DIM].T for h in range(hpg)],
                          axis=1).astype(BF16)
    t_lane = qi * tq + (lax.broadcasted_iota(jnp.int32, (1, lanes), 1) & (tq - 1))

    first_row = pl.multiple_of(n_cmp - qi * (tq // CMP_STRIDE), 8)
    logit = _dot(kcmp_ref[...], q_t) + tabc_ref[pl.ds(first_row, n_cmp), :]
    e = jnp.exp(logit - jnp.max(logit, axis=0, keepdims=True))
    any_valid = (t_lane >= CMP_BLOCK - 1).astype(F32)
    p = e / jnp.sum(e, axis=0, keepdims=True) * any_valid
    o_cmp = _dot(vcmp_t_ref[...], p.astype(BF16))

    psum = p[:, 0:tq]
    for h in range(1, hpg):
        psum = psum + p[:, h * tq:(h + 1) * tq]
    p_hi = psum.astype(BF16)
    r1 = psum - p_hi.astype(F32)
    p_mid = r1.astype(BF16)
    p_lo = (r1 - p_mid.astype(F32)).astype(BF16)
    ov = overlap_ref[...]
    imp = _dot(ov, p_hi) + _dot(ov, p_mid) + _dot(ov, p_lo)
    cur = lax.shift_right_logical(t_lane[:, 0:tq], int(math.log2(SEL_BLOCK)))
    jb = lax.broadcasted_iota(jnp.int32, (n_sb, tq), 0)
    forced = (jb == 0) | (jb == cur) | (jb == cur - 1)
    score = jnp.where(jb > cur, FUTURE_SCORE, imp + jnp.where(forced, FORCE_BONUS, 0.0))
    rank = jnp.zeros((n_sb, tq), jnp.int32)
    for j2 in range(n_sb):
        row = score[j2:j2 + 1, :]
        rank += ((row > score) | ((row == score) & (j2 < jb))).astype(jnp.int32)
    sel_neg = jnp.where(rank < SEL_TOP, 0.0, NEG).astype(BF16)
    n_aug = ksel_ref.shape[1] - HEAD_DIM
    q_sel = jnp.concatenate([q_t, jnp.concatenate([sel_neg] * hpg, axis=1),
                             jnp.zeros((n_aug - n_sb, lanes), BF16)], axis=0)

    def sweep(k_ref, v_t_ref, q_rhs, lo, edge):
        def rows(kb):
            return pl.ds(pl.multiple_of(kb * ATT_BLK, ATT_BLK), ATT_BLK)

        def scores(kb, m):
            dblk = qi - kb
            tab = jnp.minimum(dblk, TAB_FAR)
            if edge is not None:
                tab = jnp.where(dblk == edge, TAB_EDGE, tab)
            s = _dot(k_ref[rows(kb), :], q_rhs) + tab_ref[tab]
            s_ref[rows(kb), :] = s
            return jnp.maximum(m, jnp.max(s, axis=0, keepdims=True))

        def over_chunks(body, init):
            n = qi + 1 - lo

            def run(first, count):
                def f(c):
                    for t in range(count):
                        c = body(first + t, c)
                    return c
                return f

            def looped(c):
                out = lax.fori_loop(0, n // 4, lambda i, c: run(lo + 4 * i, 4)(c), c)
                rest, first = n % 4, lo + 4 * (n // 4)
                return lax.cond(rest >= 2,
                                lambda c: lax.cond(rest == 3, run(first, 3), run(first, 2), c),
                                lambda c: lax.cond(rest == 1, run(first, 1), run(first, 0), c), out)

            if edge is None:
                return looped(init)

            def straight(c):
                for back in range(edge, -1, -1):
                    c = body(qi - back, c)
                return c

            return lax.cond(qi >= edge, straight, looped, init)

        m = over_chunks(scores, jnp.full((1, lanes), NEG, F32))

        def accumulate(kb, carry):
            l, acc = carry
            pr = jnp.exp(s_ref[rows(kb), :] - m)
            return (l + jnp.sum(pr, axis=0, keepdims=True),
                    acc + _dot(v_t_ref[:, rows(kb)], pr.astype(BF16)))

        l, acc = over_chunks(accumulate, (jnp.zeros((1, lanes), F32), jnp.zeros((HEAD_DIM, lanes), F32)))
        return acc / l

    o_sel = sweep(ksel_ref, vsel_t_ref, q_sel, 0, None)
    n_back = WINDOW // ATT_BLK
    o_win = sweep(kwin_ref, vwin_t_ref, q_t, jnp.maximum(qi - n_back, 0), n_back)

    for c in range(0, tq, LANES):
        gates_t_ref[:, c:c + LANES] = jax.nn.sigmoid(gates_ref[c:c + LANES, :]).T
    for h in range(hpg):
        head = g * hpg + h
        sl = slice(h * tq, (h + 1) * tq)
        y = (gates_t_ref[pl.ds(head, 1), :] * o_cmp[:, sl]
             + gates_t_ref[pl.ds(N_ATT_HEADS + head, 1), :] * o_sel[:, sl]
             + gates_t_ref[pl.ds(2 * N_ATT_HEADS + head, 1), :] * o_win[:, sl])
        for c in range(0, tq, LANES):
            o_ref[c:c + LANES, h * HEAD_DIM:(h + 1) * HEAD_DIM] = y[:, c:c + LANES].T.astype(o_ref.dtype)

    for src, dst in zip(cast_in, cast_out):
        dst[...] = src[...].astype(BF16)


def _nsa(proj, gates, batch, seq, cmp_pos_k, w_ck1, w_ck2, cmp_pos_v, w_cv1, w_cv2, rel_table, to_cast=()):
    n_qt = seq // ATT_BLK
    hpg = HEADS_PER_GROUP
    n_cmp = seq // CMP_STRIDE
    n_sb = seq // SEL_BLOCK
    n_aug = LANES
    assert CMP_BLOCK == 2 * CMP_STRIDE and n_cmp % 8 == 0 and n_sb % 16 == 0 and n_sb <= n_aug
    assert (ATT_BLK // CMP_STRIDE) % 8 == 0 and n_qt * (ATT_BLK // CMP_STRIDE) <= n_cmp

    sweep_tab, cmp_tab = _bias_tables(rel_table, n_cmp)
    cs = np.arange(n_cmp)[None, :] * CMP_STRIDE
    ss = np.arange(n_sb)[:, None] * SEL_BLOCK
    overlap = (cs < ss + SEL_BLOCK) & (cs + CMP_BLOCK > ss) & (np.arange(n_cmp)[None, :] < n_cmp - 1)
    expand = np.arange(seq)[:, None] // SEL_BLOCK == np.arange(n_aug)[None, :]
    overlap = jnp.asarray(overlap, BF16)
    expand = jnp.asarray(expand, BF16)

    kv_col0 = COL_KV // HEAD_DIM

    def kv_spec(idx):
        return pl.BlockSpec((seq, HEAD_DIM), lambda b, g, qi, idx=idx: (b, kv_col0 + idx * N_KV_GROUPS + g))

    def whole(arr):
        return pl.BlockSpec(arr.shape, lambda b, g, qi, nd=arr.ndim: (0,) * nd)

    in_specs = [
        pl.BlockSpec((ATT_BLK, hpg * HEAD_DIM), lambda b, g, qi: (b * n_qt + qi, g)),
        kv_spec(0), kv_spec(1), kv_spec(2), kv_spec(3), kv_spec(4), kv_spec(5),
        pl.BlockSpec((ATT_BLK, LANES), lambda b, g, qi: (b * n_qt + qi, 0)),
        whole(cmp_pos_k), whole(w_ck1), whole(w_ck2), whole(cmp_pos_v), whole(w_cv1), whole(w_cv2),
        pl.BlockSpec((None,) + sweep_tab.shape[1:], lambda b, g, qi: (g, 0, 0, 0)),
        pl.BlockSpec((None,) + cmp_tab.shape[1:], lambda b, g, qi: (g, 0, 0)),
        whole(overlap), whole(expand),
    ]
    assert len(in_specs) == N_NSA_INPUTS
    n_steps = batch * N_KV_GROUPS * n_qt
    cast_specs = [_stripe_spec(a, n_steps, lambda b, g, qi: (b * N_KV_GROUPS + g) * n_qt + qi) for a in to_cast]
    outs = pl.pallas_call(
        functools.partial(_nsa_body, n_cast=len(to_cast)),
        grid=(batch, N_KV_GROUPS, n_qt),
        in_specs=in_specs + cast_specs,
        out_specs=[pl.BlockSpec((ATT_BLK, hpg * HEAD_DIM), lambda b, g, qi: (b * n_qt + qi, g))] + cast_specs,
        out_shape=[jax.ShapeDtypeStruct((batch * seq, D_ATT), BF16)]
        + [jax.ShapeDtypeStruct(a.shape, BF16) for a in to_cast],
        scratch_shapes=[pltpu.VMEM((n_cmp, HEAD_DIM), BF16), pltpu.VMEM((HEAD_DIM, n_cmp), BF16),
                        pltpu.VMEM((seq, HEAD_DIM + n_aug), BF16), pltpu.VMEM((HEAD_DIM, seq), BF16),
                        pltpu.VMEM((seq, HEAD_DIM), BF16), pltpu.VMEM((HEAD_DIM, seq), BF16),
                        pltpu.VMEM((LANES, ATT_BLK), F32), pltpu.VMEM((seq, hpg * ATT_BLK), F32)],
        compiler_params=pltpu.CompilerParams(
            dimension_semantics=(("arbitrary",) * 3 if to_cast else ("parallel", "parallel", "arbitrary")),
            vmem_limit_bytes=VMEM_LIMIT_BYTES),
        name="nsa",
    )(proj, proj, proj, proj, proj, proj, proj, gates,
      cmp_pos_k, w_ck1.astype(BF16), w_ck2.astype(BF16), cmp_pos_v, w_cv1.astype(BF16), w_cv2.astype(BF16),
      sweep_tab, cmp_tab, overlap, expand, *to_cast)
    return outs[0], outs[1:]


def _pool_body(u_ref, w_ref, scale_ref, o_ref):
    gi = pl.program_id(1)
    u = u_ref[...]
    seq = u.shape[0]
    t = lax.broadcasted_iota(jnp.int32, (seq, 1), 0)

    def shifted(x, k):
        return jnp.where(t >= k, pltpu.roll(x, k, 0), 0.0)

    sums = [u]
    for n in range(int(math.log2(POOL_WINDOWS[-1]))):
        sums.append(sums[-1] + shifted(sums[-1], 2 ** n))
    wsum = sums[int(math.log2(POOL_WINDOWS[0]))]
    for idx in range(1, len(POOL_WINDOWS)):
        wsum = jnp.where(gi >= idx, sums[int(math.log2(POOL_WINDOWS[idx]))], wsum)
    window = POOL_WINDOWS[0]
    for idx in range(1, len(POOL_WINDOWS)):
        window = jnp.where(gi >= idx, POOL_WINDOWS[idx], window)
    cnt = jnp.minimum(t + 1, window).astype(F32)
    d = (wsum / cnt - u).astype(BF16)
    o_ref[...] = (_dot(d, w_ref[0]) * scale_ref[...]).astype(o_ref.dtype)


def _pool(proj, batch, seq, w_pool, pool_scale):
    n_g, dg, _ = w_pool.shape
    assert all(w == 2 ** int(math.log2(w)) for w in POOL_WINDOWS) and list(POOL_WINDOWS) == sorted(POOL_WINDOWS)
    col0 = COL_U // dg
    return pl.pallas_call(
        _pool_body,
        grid=(batch, n_g),
        in_specs=[
            pl.BlockSpec((seq, dg), lambda b, gi: (b, col0 + gi)),
            pl.BlockSpec((1, dg, dg), lambda b, gi: (gi, 0, 0)),
            pl.BlockSpec((1, dg), lambda b, gi: (0, gi)),
        ],
        out_specs=pl.BlockSpec((seq, dg), lambda b, gi: (b, gi)),
        out_shape=jax.ShapeDtypeStruct((batch * seq, n_g * dg), BF16),
        compiler_params=pltpu.CompilerParams(
            dimension_semantics=("parallel", "parallel"), vmem_limit_bytes=VMEM_LIMIT_BYTES),
        name="pool",
    )(proj, w_pool.astype(BF16), pool_scale.reshape(1, -1))


def _outproj_body(x_ref, ya_ref, yp_ref, wa_ref, wp_ref, o_ref):
    o_ref[...] = x_ref[...] + _dot(ya_ref[...], wa_ref[...]) + _dot(yp_ref[...], wp_ref[...])


def _outproj(x, y_att, y_pool, w_out):
    n, d = x.shape
    da = y_att.shape[1]
    dp = y_pool.shape[1]
    w = w_out
    return pl.pallas_call(
        _outproj_body,
        grid=(n // OUT_TM, d // OUT_TN),
        in_specs=[
            pl.BlockSpec((OUT_TM, OUT_TN), lambda i, j: (i, j)),
            pl.BlockSpec((OUT_TM, da), lambda i, j: (i, 0)),
            pl.BlockSpec((OUT_TM, dp), lambda i, j: (i, 0)),
            pl.BlockSpec((da, OUT_TN), lambda i, j: (0, j)),
            pl.BlockSpec((dp, OUT_TN), lambda i, j: (da // dp, j)),
        ],
        out_specs=pl.BlockSpec((OUT_TM, OUT_TN), lambda i, j: (i, j)),
        out_shape=jax.ShapeDtypeStruct((n, d), F32),
        compiler_params=pltpu.CompilerParams(
            dimension_semantics=("parallel", "parallel"), vmem_limit_bytes=VMEM_LIMIT_BYTES),
        name="out_proj",
    )(x, y_att, y_pool, w, w)


def kernel(x, norm_ffn1, w_ffn1_gate, w_ffn1_up, w_ffn1_down, norm_mix, w_in, cmp_pos_k, w_cmp_k1, w_cmp_k2,
           cmp_pos_v, w_cmp_v1, w_cmp_v2, w_pool, pool_scale, w_out, rel_table, norm_ffn2, w_ffn2_gate,
           w_ffn2_up, w_ffn2_down, norm_final):
    batch, seq, d = x.shape
    depth = norm_ffn1.shape[0]
    xf = x.reshape(batch * seq, d)
    gf = norm_final.reshape(1, d)
    for l in range(depth):
        xf, (w_in_t,) = _ffn(
            xf, norm_ffn1[l].reshape(1, d), w_ffn1_gate[l].astype(BF16), w_ffn1_up[l].astype(BF16),
            w_ffn1_down[l].astype(BF16), gf, final_norm=False, to_cast=(w_in[l].T,))
        proj, gates = _inproj(xf, norm_mix[l].reshape(1, d), w_in_t)
        y_att, (w2_gate, w2_up, w2_down, w_out_b) = _nsa(
            proj, gates, batch, seq, cmp_pos_k[l], w_cmp_k1[l], w_cmp_k2[l], cmp_pos_v[l], w_cmp_v1[l], w_cmp_v2[l],
            rel_table, to_cast=(w_ffn2_gate[l], w_ffn2_up[l], w_ffn2_down[l], w_out[l]))
        y_pool = _pool(proj, batch, seq, w_pool[l], pool_scale[l])
        xf = _outproj(xf, y_att, y_pool, w_out_b)
        xf, _ = _ffn(xf, norm_ffn2[l].reshape(1, d), w2_gate, w2_up, w2_down, gf, final_norm=(l == depth - 1))
    if depth == 0:
        raise ValueError("depth must be positive")
    return xf.reshape(batch, seq, d)
```
